```python
import math
import jax, jax.numpy as jnp
from jax import lax
import numpy as np

D_MODEL = 1024
BATCH = 8
SEQ = 4096
DEPTH = 2
DEC_BATCH = 32
DEC_SEQ = 1
PAST_LEN = 16384
PAGE_SIZE = 128

N_A_LAYERS = DEPTH // 2
N_B_LAYERS = DEPTH - N_A_LAYERS
SSM_WIDTH = D_MODEL
SSM_GROUP = 16
SSM_GROUPS = SSM_WIDTH // SSM_GROUP
SSM_STATE = 64
SSM_CHUNK = 128
DT_MIN = 1e-3
DT_MAX = 1e-1
HEAD_DIM = 64
N_HEADS = D_MODEL // HEAD_DIM
KV_HEADS = 4
Q_PER_KV = N_HEADS // KV_HEADS
ATTN_WIDTH = N_HEADS * HEAD_DIM
MOBA_BLOCK = 256
MOBA_TOPK = 3
Q_BLOCK = 32
EPS = 1e-6
NEG_INF = -1e30

kernel_name = 'yoco_s5_moba_decoder_step'


def _rms_norm(x, g):
    xf = x.astype(jnp.float32)
    y = xf * lax.rsqrt(jnp.mean(xf * xf, axis=-1, keepdims=True) + EPS)
    return (y * g.astype(jnp.float32)).astype(x.dtype)


def _alibi_slopes():
    h = jnp.arange(1, N_HEADS + 1, dtype=jnp.float32)
    return jnp.exp2(-8.0 * h / N_HEADS).reshape(KV_HEADS, Q_PER_KV)


def _complex_affine_combine(e1, e2):
    a1r, a1i, b1r, b1i = e1
    a2r, a2i, b2r, b2i = e2
    return (a2r * a1r - a2i * a1i,
            a2r * a1i + a2i * a1r,
            a2r * b1r - a2i * b1i + b2r,
            a2r * b1i + a2i * b1r + b2i)


def _s5_mixer(u, h0_re, h0_im, lam_re, lam_im, log_dt, b_re, b_im, c_re, c_im, d_skip):
    f32 = jnp.float32
    bsz, seqlen, _ = u.shape
    lr = lam_re.astype(f32)
    li = lam_im.astype(f32)
    dt = jnp.exp(log_dt.astype(f32))[:, None]
    mag = jnp.exp(lr * dt)
    ar = mag * jnp.cos(li * dt)
    ai = mag * jnp.sin(li * dt)
    den = lr * lr + li * li
    nr = ar - 1.0
    coef_re = (nr * lr + ai * li) / den
    coef_im = (ai * lr - nr * li) / den
    br_ = b_re.astype(f32)
    bi_ = b_im.astype(f32)
    bb_re = coef_re[..., None] * br_ - coef_im[..., None] * bi_
    bb_im = coef_re[..., None] * bi_ + coef_im[..., None] * br_
    cr = c_re.astype(f32)
    ci = c_im.astype(f32)
    dsk = d_skip.astype(f32).reshape(SSM_GROUPS, SSM_GROUP)
    chunk = SSM_CHUNK if seqlen % SSM_CHUNK == 0 else seqlen
    n_chunks = seqlen // chunk
    uc = u.reshape(bsz, n_chunks, chunk, SSM_GROUPS, SSM_GROUP).transpose(1, 0, 2, 3, 4)

    def step(carry, u_blk):
        hr0, hi0 = carry
        uf = u_blk.astype(f32)
        bur = jnp.einsum('btgp,gnp->btgn', uf, bb_re)
        bui = jnp.einsum('btgp,gnp->btgn', uf, bb_im)
        a_r = jnp.broadcast_to(ar, bur.shape)
        a_i = jnp.broadcast_to(ai, bur.shape)
        acr, aci, hr, hi = lax.associative_scan(_complex_affine_combine, (a_r, a_i, bur, bui), axis=1)
        hr = hr + acr * hr0[:, None] - aci * hi0[:, None]
        hi = hi + acr * hi0[:, None] + aci * hr0[:, None]
        y = (jnp.einsum('btgn,gpn->btgp', hr, cr) - jnp.einsum('btgn,gpn->btgp', hi, ci)
             + dsk * uf)
        return (hr[:, -1], hi[:, -1]), y

    (hr_f, hi_f), ys = lax.scan(step, (h0_re.astype(f32), h0_im.astype(f32)), uc)
    y = ys.transpose(1, 0, 2, 3, 4).reshape(bsz, seqlen, SSM_WIDTH).astype(u.dtype)
    return y, hr_f, hi_f


def _moba_attention(q, k, v, pos0):
    f32 = jnp.float32
    bsz, seqlen = q.shape[0], q.shape[1]
    klen = k.shape[1]
    n_blocks = -(-klen // MOBA_BLOCK)
    qb = Q_BLOCK if seqlen % Q_BLOCK == 0 else seqlen
    n_qb = seqlen // qb
    win = MOBA_BLOCK + qb
    pad = n_blocks * MOBA_BLOCK + qb - klen
    kp = jnp.pad(k, ((0, 0), (0, pad), (0, 0), (0, 0)))
    vp = jnp.pad(v, ((0, 0), (0, pad), (0, 0), (0, 0)))
    k_blk = kp[:, :n_blocks * MOBA_BLOCK].reshape(bsz, n_blocks, MOBA_BLOCK, KV_HEADS, HEAD_DIM)
    v_blk = vp[:, :n_blocks * MOBA_BLOCK].reshape(bsz, n_blocks, MOBA_BLOCK, KV_HEADS, HEAD_DIM)
    k_mean = jnp.mean(k_blk.astype(f32), axis=2)
    kb = k_blk.transpose(0, 3, 1, 2, 4)
    vb = v_blk.transpose(0, 3, 1, 2, 4)
    topk = min(MOBA_TOPK, n_blocks)
    slopes = _alibi_slopes()
    scale = HEAD_DIM ** -0.5
    bi = jnp.arange(bsz)[:, None, None, None, None]
    hi = jnp.arange(KV_HEADS)[None, :, None, None, None]
    qg = q.reshape(bsz, n_qb, qb, KV_HEADS, Q_PER_KV, HEAD_DIM).transpose(1, 0, 2, 3, 4, 5)
    starts = pos0 + qb * jnp.arange(n_qb, dtype=jnp.int32)

    def one(args):
        qs, p0 = args
        pos = p0 + jnp.arange(qb, dtype=jnp.int32)
        cur = pos // MOBA_BLOCK
        gate = jnp.einsum('bqkgd,bnkd->bkgqn', qs.astype(f32), k_mean)
        past = jnp.arange(n_blocks)[None, :] < cur[:, None]
        gate = jnp.where(past, gate, NEG_INF)
        _, idx = lax.top_k(gate, topk)
        sel_ok = idx < cur[:, None]
        kg = kb[bi, hi, idx]
        vg = vb[bi, hi, idx]
        key_pos = idx[..., None] * MOBA_BLOCK + jnp.arange(MOBA_BLOCK, dtype=jnp.int32)
        dist = (pos[:, None, None] - key_pos).astype(f32)
        s_sel = (jnp.einsum('bqkgd,bkgqjsd->bkgqjs', qs, kg).astype(f32) * scale
                 - slopes[None, :, :, None, None, None] * dist)
        s_sel = jnp.where(sel_ok[..., None], s_sel, NEG_INF)
        s_sel = s_sel.reshape(bsz, KV_HEADS, Q_PER_KV, qb, topk * MOBA_BLOCK)
        w0 = (p0 // MOBA_BLOCK) * MOBA_BLOCK
        kw = lax.dynamic_slice_in_dim(kp, w0, win, axis=1)
        vw = lax.dynamic_slice_in_dim(vp, w0, win, axis=1)
        wpos = w0 + jnp.arange(win, dtype=jnp.int32)
        win_ok = (wpos[None, :] // MOBA_BLOCK == cur[:, None]) & (wpos[None, :] <= pos[:, None])
        dist_w = (pos[:, None] - wpos[None, :]).astype(f32)
        s_win = (jnp.einsum('bqkgd,bskd->bkgqs', qs, kw).astype(f32) * scale
                 - slopes[None, :, :, None, None] * dist_w)
        s_win = jnp.where(win_ok, s_win, NEG_INF)
        p = jax.nn.softmax(jnp.concatenate([s_sel, s_win], axis=-1), axis=-1).astype(v.dtype)
        p_sel = p[..., :topk * MOBA_BLOCK].reshape(bsz, KV_HEADS, Q_PER_KV, qb, topk, MOBA_BLOCK)
        p_win = p[..., topk * MOBA_BLOCK:]
        return (jnp.einsum('bkgqjs,bkgqjsd->bqkgd', p_sel, vg)
                + jnp.einsum('bkgqs,bskd->bqkgd', p_win, vw))

    out = lax.map(one, (qg, starts))
    return out.transpose(1, 0, 2, 3, 4, 5).reshape(bsz, seqlen, ATTN_WIDTH)


def _trunk(x, h0_re, h0_im, k_past, v_past, pos0,
           g_pre_a, w_in_a, lambda_re, lambda_im, log_dt, b_re, b_im, c_re, c_im, d_skip,
           w_glu, b_glu, w_out_a, g_post_a, g_kv, w_kv,
           g_pre_b, w_in_b, w_out_b, g_post_b):
    bsz, seqlen, _ = x.shape
    new_re, new_im = [], []
    k_all = v_all = k_new = v_new = None
    for layer in range(DEPTH):
        if layer < N_A_LAYERS:
            i = layer
            h = _rms_norm(x, g_pre_a[i])
            u, gate = jnp.split(h @ w_in_a[i], 2, axis=-1)
            y, hr, hi = _s5_mixer(u, h0_re[i], h0_im[i], lambda_re[i], lambda_im[i], log_dt[i],
                                  b_re[i], b_im[i], c_re[i], c_im[i], d_skip[i])
            gy = jax.nn.gelu(y, approximate=False)
            y = gy * jax.nn.sigmoid(gy @ w_glu[i] + b_glu[i])
            o = (y * jax.nn.silu(gate)) @ w_out_a[i]
            x = x + _rms_norm(o, g_post_a[i])
            new_re.append(hr)
            new_im.append(hi)
            if layer == N_A_LAYERS - 1:
                kv = _rms_norm(x, g_kv) @ w_kv
                k_new, v_new = jnp.split(kv, 2, axis=-1)
                k_new = k_new.reshape(bsz, seqlen, KV_HEADS, HEAD_DIM)
                v_new = v_new.reshape(bsz, seqlen, KV_HEADS, HEAD_DIM)
                if k_past is None:
                    k_all, v_all = k_new, v_new
                else:
                    k_all = jnp.concatenate([k_past.astype(k_new.dtype), k_new], axis=1)
                    v_all = jnp.concatenate([v_past.astype(v_new.dtype), v_new], axis=1)
        else:
            j = layer - N_A_LAYERS
            h = _rms_norm(x, g_pre_b[j])
            q, gate = jnp.split(h @ w_in_b[j], 2, axis=-1)
            q = q.reshape(bsz, seqlen, N_HEADS, HEAD_DIM)
            att = _moba_attention(q, k_all, v_all, pos0)
            o = (att * jax.nn.silu(gate)) @ w_out_b[j]
            x = x + _rms_norm(o, g_post_b[j])
    return x, jnp.stack(new_re), jnp.stack(new_im), k_new, v_new


def setup_inputs(seed: int = 0) -> dict:
    key = jax.random.key(seed)
    ks = jax.random.split(key, 32)
    f32 = jnp.float32
    n_pages = PAST_LEN // PAGE_SIZE
    n_pool = (5 * DEC_BATCH * n_pages) // 4
    nrm = jax.random.normal
    x_prompt = nrm(ks[0], (BATCH, SEQ, D_MODEL), f32)
    x_sample = nrm(ks[1], (DEC_BATCH, DEC_SEQ, D_MODEL), f32)
    state_ssm_re = 0.1 * nrm(ks[2], (N_A_LAYERS, DEC_BATCH, SSM_GROUPS, SSM_STATE), f32)
    state_ssm_im = 0.1 * nrm(ks[3], (N_A_LAYERS, DEC_BATCH, SSM_GROUPS, SSM_STATE), f32)
    cache_k = nrm(ks[4], (n_pool, PAGE_SIZE, KV_HEADS, HEAD_DIM), f32)
    cache_v = nrm(ks[5], (n_pool, PAGE_SIZE, KV_HEADS, HEAD_DIM), f32)
    perm = jax.random.permutation(ks[6], n_pool)
    page_table = perm[:DEC_BATCH * n_pages].astype(jnp.int32).reshape(DEC_BATCH, n_pages)
    g_pre_a = 1.0 + 0.02 * nrm(ks[7], (N_A_LAYERS, D_MODEL), f32)
    w_in_a = nrm(ks[8], (N_A_LAYERS, D_MODEL, 2 * SSM_WIDTH), f32) * D_MODEL ** -0.5
    lambda_re = -0.5 + 0.01 * nrm(ks[9], (N_A_LAYERS, SSM_GROUPS, SSM_STATE), f32)
    lambda_im = (math.pi * jnp.arange(SSM_STATE, dtype=f32)
                 + 0.01 * nrm(ks[10], (N_A_LAYERS, SSM_GROUPS, SSM_STATE), f32))
    log_dt = jax.random.uniform(ks[11], (N_A_LAYERS, SSM_GROUPS), f32,
                                minval=math.log(DT_MIN), maxval=math.log(DT_MAX))
    b_re = nrm(ks[12], (N_A_LAYERS, SSM_GROUPS, SSM_STATE, SSM_GROUP), f32) * (2 * SSM_GROUP) ** -0.5
    b_im = nrm(ks[13], (N_A_LAYERS, SSM_GROUPS, SSM_STATE, SSM_GROUP), f32) * (2 * SSM_GROUP) ** -0.5
    c_re = nrm(ks[14], (N_A_LAYERS, SSM_GROUPS, SSM_GROUP, SSM_STATE), f32) * SSM_STATE ** -0.5
    c_im = nrm(ks[15], (N_A_LAYERS, SSM_GROUPS, SSM_GROUP, SSM_STATE), f32) * SSM_STATE ** -0.5
    d_skip = nrm(ks[16], (N_A_LAYERS, SSM_WIDTH), f32)
    w_glu = nrm(ks[17], (N_A_LAYERS, SSM_WIDTH, SSM_WIDTH), f32) * SSM_WIDTH ** -0.5
    b_glu = 0.01 * nrm(ks[18], (N_A_LAYERS, SSM_WIDTH), f32)
    w_out_a = nrm(ks[19], (N_A_LAYERS, SSM_WIDTH, D_MODEL), f32) * SSM_WIDTH ** -0.5
    g_post_a = 1.0 + 0.02 * nrm(ks[20], (N_A_LAYERS, D_MODEL), f32)
    g_kv = 1.0 + 0.02 * nrm(ks[21], (D_MODEL,), f32)
    w_kv = nrm(ks[22], (D_MODEL, 2 * KV_HEADS * HEAD_DIM), f32) * D_MODEL ** -0.5
    g_pre_b = 1.0 + 0.02 * nrm(ks[23], (N_B_LAYERS, D_MODEL), f32)
    w_in_b = nrm(ks[24], (N_B_LAYERS, D_MODEL, 2 * ATTN_WIDTH), f32) * D_MODEL ** -0.5
    w_out_b = nrm(ks[25], (N_B_LAYERS, ATTN_WIDTH, D_MODEL), f32) * ATTN_WIDTH ** -0.5
    g_post_b = 1.0 + 0.02 * nrm(ks[26], (N_B_LAYERS, D_MODEL), f32)
    return {'x_prompt': x_prompt, 'x_sample': x_sample,
            'state_ssm_re': state_ssm_re, 'state_ssm_im': state_ssm_im,
            'cache_k': cache_k, 'cache_v': cache_v, 'page_table': page_table,
            'g_pre_a': g_pre_a, 'w_in_a': w_in_a, 'lambda_re': lambda_re, 'lambda_im': lambda_im,
            'log_dt': log_dt, 'b_re': b_re, 'b_im': b_im, 'c_re': c_re, 'c_im': c_im,
            'd_skip': d_skip, 'w_glu': w_glu, 'b_glu': b_glu, 'w_out_a': w_out_a,
            'g_post_a': g_post_a, 'g_kv': g_kv, 'w_kv': w_kv,
            'g_pre_b': g_pre_b, 'w_in_b': w_in_b, 'w_out_b': w_out_b, 'g_post_b': g_post_b}


def reference(x_prompt, x_sample, state_ssm_re, state_ssm_im, cache_k, cache_v, page_table,
              g_pre_a, w_in_a, lambda_re, lambda_im, log_dt, b_re, b_im, c_re, c_im, d_skip,
              w_glu, b_glu, w_out_a, g_post_a, g_kv, w_kv,
              g_pre_b, w_in_b, w_out_b, g_post_b):
    h0 = jnp.zeros((N_A_LAYERS, x_prompt.shape[0], SSM_GROUPS, SSM_STATE), jnp.float32)
    y_prompt, ssm_re_p, ssm_im_p, k_p, v_p = _trunk(
        x_prompt, h0, h0, None, None, 0,
        g_pre_a, w_in_a, lambda_re, lambda_im, log_dt, b_re, b_im, c_re, c_im, d_skip,
        w_glu, b_glu, w_out_a, g_post_a, g_kv, w_kv, g_pre_b, w_in_b, w_out_b, g_post_b)
    n_seq, n_pages = page_table.shape
    past_len = n_pages * PAGE_SIZE
    k_past = cache_k[page_table].reshape(n_seq, past_len, KV_HEADS, HEAD_DIM)
    v_past = cache_v[page_table].reshape(n_seq, past_len, KV_HEADS, HEAD_DIM)
    y_sample, ssm_re_s, ssm_im_s, k_s, v_s = _trunk(
        x_sample, state_ssm_re, state_ssm_im, k_past, v_past, past_len,
        g_pre_a, w_in_a, lambda_re, lambda_im, log_dt, b_re, b_im, c_re, c_im, d_skip,
        w_glu, b_glu, w_out_a, g_post_a, g_kv, w_kv, g_pre_b, w_in_b, w_out_b, g_post_b)
    return (y_prompt, y_sample, ssm_re_p, ssm_im_p, k_p, v_p, ssm_re_s, ssm_im_s, k_s, v_s)
```

```python
import functools
import math

import jax
import jax.numpy as jnp
from jax import lax
from jax.experimental import pallas as pl
from jax.experimental.pallas import tpu as pltpu

F32 = jnp.float32
BF16 = jnp.bfloat16
HIGHEST = lax.Precision.HIGHEST

D_MODEL = 1024
SSM_GROUP = 16
SSM_GROUPS = D_MODEL // SSM_GROUP
SSM_STATE = 64
HEAD_DIM = 64
N_HEADS = D_MODEL // HEAD_DIM
KV_HEADS = 4
Q_PER_KV = N_HEADS // KV_HEADS
KV_WIDTH = KV_HEADS * HEAD_DIM
MOBA_BLOCK = 256
MOBA_TOPK = 3
PAGE_SIZE = 128
PAGES_PER_BLOCK = MOBA_BLOCK // PAGE_SIZE
EPS = 1e-6
NEG_INF = -1e30
REMOVED = -3e38
SCALE = HEAD_DIM ** -0.5
S5_CHUNK = 16
LANES = 128
VMEM_LIMIT = 56 * 1024 * 1024


def _cparams(n_grid):
    return pltpu.CompilerParams(dimension_semantics=("arbitrary",) * n_grid,
                                vmem_limit_bytes=VMEM_LIMIT)


def _dot(a, b, prec=None):
    return jnp.dot(a, b, preferred_element_type=F32, precision=prec)


def _dot_nt(a, b, prec=None):
    return lax.dot_general(a, b, (((1,), (1,)), ((), ())),
                           preferred_element_type=F32, precision=prec)


def _rms_scale(x):
    return x * lax.rsqrt(jnp.mean(x * x, axis=-1, keepdims=True) + EPS)


def _sigmoid(x):
    return 1.0 / (1.0 + jnp.exp(-x))


def _full(shape):
    zeros = (0,) * len(shape)
    return pl.BlockSpec(shape, lambda *_: zeros)


def _norm_matmul_body(x_ref, g_ref, w_ref, *out_refs, prec):
    h = (_rms_scale(x_ref[...]) * g_ref[...]).astype(w_ref.dtype)
    off = 0
    for o_ref in out_refs:
        n = o_ref.shape[-1]
        o_ref[...] = _dot(h, w_ref[:, off:off + n], prec).astype(o_ref.dtype)
        off += n


def _norm_matmul(x, g, w, out_widths, out_dtypes, tm, prec):
    m, d = x.shape
    n = w.shape[1]
    return pl.pallas_call(
        functools.partial(_norm_matmul_body, prec=prec),
        grid=(m // tm,),
        in_specs=[pl.BlockSpec((tm, d), lambda i: (i, 0)), _full((1, d)), _full((d, n))],
        out_specs=[pl.BlockSpec((tm, wd), lambda i: (i, 0)) for wd in out_widths],
        out_shape=[jax.ShapeDtypeStruct((m, wd), dt) for wd, dt in zip(out_widths, out_dtypes)],
        compiler_params=_cparams(1),
        name="norm_matmul",
    )(x, g.reshape(1, d), w)


def _s5_prompt_body(u_ref, m_ref, sre_ref, sim_ref, rre_ref, rim_ref, a_ref, d_ref,
                    y_ref, hre_ref, him_ref, er_s, ei_s, hr_s, hi_s, *, n_chunks, bsz):
    u = u_ref[0]
    er_s[...] = _dot(u, sre_ref[0])
    ei_s[...] = _dot(u, sim_ref[0])
    ar = jnp.broadcast_to(a_ref[0, 0:1, :], (bsz, LANES))
    ai = jnp.broadcast_to(a_ref[0, 1:2, :], (bsz, LANES))

    def step(c, carry):
        hr, hi = carry
        rows = pl.ds(pl.multiple_of(c * bsz, bsz), bsz)
        hr_s[rows, :] = hr
        hi_s[rows, :] = hi
        return (ar * hr - ai * hi + er_s[rows, :], ar * hi + ai * hr + ei_s[rows, :])

    zero = jnp.zeros((bsz, LANES), F32)
    hr, hi = lax.fori_loop(0, n_chunks, step, (zero, zero))
    hre_ref[0] = hr
    him_ref[0] = hi
    y = _dot(u, m_ref[0])
    y += _dot(hr_s[...].astype(BF16), rre_ref[0])
    y += _dot(hi_s[...].astype(BF16), rim_ref[0])
    y_ref[0] = y + d_ref[0] * u.astype(F32)


def _s5_prompt(u2, ops, bsz):
    g, rows, width = u2.shape
    n_chunks = rows // bsz
    m, sre, sim, rre, rim, a16, dt = ops
    per_group = lambda shape: pl.BlockSpec((1,) + shape, lambda i: (i, 0, 0))
    return pl.pallas_call(
        functools.partial(_s5_prompt_body, n_chunks=n_chunks, bsz=bsz),
        grid=(g,),
        in_specs=[per_group((rows, width)), per_group((width, width)),
                  per_group((width, LANES)), per_group((width, LANES)),
                  per_group((LANES, width)), per_group((LANES, width)),
                  per_group((2, LANES)), per_group((1, width))],
        out_specs=[per_group((rows, width)), per_group((bsz, LANES)), per_group((bsz, LANES))],
        out_shape=[jax.ShapeDtypeStruct((g, rows, width), F32),
                   jax.ShapeDtypeStruct((g, bsz, LANES), F32),
                   jax.ShapeDtypeStruct((g, bsz, LANES), F32)],
        scratch_shapes=[pltpu.VMEM((rows, LANES), F32) for _ in range(4)],
        compiler_params=_cparams(1),
        name="s5_prompt",
    )(u2, m, sre, sim, rre, rim, a16, dt)


def _s5_discretize(lam_re, lam_im, log_dt, b_re, b_im):
    dt = jnp.exp(log_dt)[:, None]
    mag = jnp.exp(lam_re * dt)
    ar = mag * jnp.cos(lam_im * dt)
    ai = mag * jnp.sin(lam_im * dt)
    den = lam_re * lam_re + lam_im * lam_im
    nr = ar - 1.0
    coef_re = (nr * lam_re + ai * lam_im) / den
    coef_im = (ai * lam_re - nr * lam_im) / den
    bb_re = coef_re[..., None] * b_re - coef_im[..., None] * b_im
    bb_im = coef_re[..., None] * b_im + coef_im[..., None] * b_re
    return ar, ai, bb_re, bb_im


def _s5_prompt_operators(ar, ai, bb_re, bb_im, c_re, c_im, d_skip):
    t = S5_CHUNK
    g, n = ar.shape
    p = SSM_GROUP
    pr, pi = [jnp.ones_like(ar)], [jnp.zeros_like(ar)]
    for _ in range(t):
        pr, pi = pr + [pr[-1] * ar - pi[-1] * ai], pi + [pr[-1] * ai + pi[-1] * ar]
    pr = jnp.stack(pr)
    pi = jnp.stack(pi)
    cp_re = c_re[None] * pr[:, :, None, :] - c_im[None] * pi[:, :, None, :]
    cp_im = c_re[None] * pi[:, :, None, :] + c_im[None] * pr[:, :, None, :]
    k = (jnp.einsum('tgpn,gnq->gtpq', cp_re[:t], bb_re, precision=HIGHEST)
         - jnp.einsum('tgpn,gnq->gtpq', cp_im[:t], bb_im, precision=HIGHEST))
    s_idx = jnp.arange(t)[:, None]
    t_idx = jnp.arange(t)[None, :]
    lag = t_idx - s_idx
    kt = k[:, jnp.clip(lag, 0, t - 1)]
    kt = jnp.where((lag >= 0)[None, :, :, None, None], kt, 0.0)
    m = kt.transpose(0, 1, 4, 2, 3).reshape(g, t * p, t * p)
    pr_rev = pr[t - 1::-1][:t]
    pi_rev = pi[t - 1::-1][:t]
    s_re = pr_rev[:, :, :, None] * bb_re[None] - pi_rev[:, :, :, None] * bb_im[None]
    s_im = pr_rev[:, :, :, None] * bb_im[None] + pi_rev[:, :, :, None] * bb_re[None]
    pad_lanes = lambda x: jnp.pad(x, ((0, 0), (0, 0), (0, LANES - n)))
    s_re = pad_lanes(s_re.transpose(1, 0, 3, 2).reshape(g, t * p, n))
    s_im = pad_lanes(s_im.transpose(1, 0, 3, 2).reshape(g, t * p, n))
    pad_rows = lambda x: jnp.pad(x, ((0, 0), (0, LANES - n), (0, 0)))
    r_re = pad_rows(cp_re[1:].transpose(1, 3, 0, 2).reshape(g, n, t * p))
    r_im = pad_rows(-cp_im[1:].transpose(1, 3, 0, 2).reshape(g, n, t * p))
    a16 = pad_lanes(jnp.stack([pr[t], pi[t]], axis=1))
    dt_ = jnp.tile(d_skip.reshape(g, 1, p), (1, 1, t))
    return (m.astype(BF16), s_re.astype(BF16), s_im.astype(BF16),
            r_re.astype(BF16), r_im.astype(BF16), a16, dt_)


def _s5_step_body(u_ref, h0r_ref, h0i_ref, ar_ref, ai_ref, bbr_ref, bbi_ref, cr_ref, ci_ref,
                  d_ref, y_ref, hr_ref, hi_ref):
    u = u_ref[0]
    ar, ai = ar_ref[0], ai_ref[0]
    h0r, h0i = h0r_ref[0], h0i_ref[0]
    hr = _dot_nt(u, bbr_ref[0], HIGHEST) + ar * h0r - ai * h0i
    hi = _dot_nt(u, bbi_ref[0], HIGHEST) + ar * h0i + ai * h0r
    hr_ref[0] = hr
    hi_ref[0] = hi
    y_ref[0] = (_dot_nt(hr, cr_ref[0], HIGHEST) - _dot_nt(hi, ci_ref[0], HIGHEST)
                + d_ref[0] * u)


def _s5_step(u_g, h0r, h0i, ar, ai, bb_re, bb_im, c_re, c_im, d_skip):
    g, b, p = u_g.shape
    n = ar.shape[-1]
    per_group = lambda shape: pl.BlockSpec((1,) + shape, lambda i: (i, 0, 0))
    return pl.pallas_call(
        _s5_step_body,
        grid=(g,),
        in_specs=[per_group((b, p)), per_group((b, n)), per_group((b, n)),
                  per_group((1, n)), per_group((1, n)),
                  per_group((n, p)), per_group((n, p)),
                  per_group((p, n)), per_group((p, n)), per_group((1, p))],
        out_specs=[per_group((b, p)), per_group((b, n)), per_group((b, n))],
        out_shape=[jax.ShapeDtypeStruct((g, b, p), F32),
                   jax.ShapeDtypeStruct((g, b, n), F32),
                   jax.ShapeDtypeStruct((g, b, n), F32)],
        compiler_params=_cparams(1),
        name="s5_step",
    )(u_g, h0r, h0i, ar.reshape(g, 1, n), ai.reshape(g, 1, n), bb_re, bb_im, c_re, c_im,
      d_skip.reshape(g, 1, p))


def _post_a_body(y_ref, ga_ref, x_ref, wglu_ref, bglu_ref, wout_ref, gpost_ref, x1_ref, *, prec):
    y = y_ref[...]
    wdt = wglu_ref.dtype
    gy = 0.5 * y * (1.0 + lax.erf(y * math.sqrt(0.5)))
    z = _dot(gy.astype(wdt), wglu_ref[...], prec) + bglu_ref[...]
    gate = ga_ref[...]
    t = (gy * _sigmoid(z)) * (gate * _sigmoid(gate))
    o = _dot(t.astype(wdt), wout_ref[...], prec)
    x1_ref[...] = x_ref[...] + _rms_scale(o) * gpost_ref[...]


def _post_a(y, gate, x, w_glu, b_glu, w_out, g_post, tm, prec):
    m, d = x.shape
    tile = pl.BlockSpec((tm, d), lambda i: (i, 0))
    return pl.pallas_call(
        functools.partial(_post_a_body, prec=prec),
        grid=(m // tm,),
        in_specs=[tile, tile, tile, _full((d, d)), _full((1, d)), _full((d, d)), _full((1, d))],
        out_specs=tile,
        out_shape=jax.ShapeDtypeStruct((m, d), F32),
        compiler_params=_cparams(1),
        name="post_a",
    )(y, gate, x, w_glu, b_glu.reshape(1, d), w_out, g_post.reshape(1, d))


def _pre_b_body(x1_ref, gkv_ref, wkv_ref, gb_ref, winb_ref, k_ref, v_ref, q_ref, gate_ref,
                *prompt_refs, prec):
    xn = _rms_scale(x1_ref[...])
    wdt = wkv_ref.dtype
    hk = (xn * gkv_ref[...]).astype(wdt)
    k = _dot(hk, wkv_ref[:, :KV_WIDTH], prec)
    v = _dot(hk, wkv_ref[:, KV_WIDTH:], prec)
    k_ref[...] = k
    v_ref[...] = v
    hq = (xn * gb_ref[...]).astype(wdt)
    q_ref[...] = (_dot(hq, winb_ref[:, :D_MODEL], prec) * SCALE).astype(q_ref.dtype)
    gate_ref[...] = _dot(hq, winb_ref[:, D_MODEL:], prec)
    if prompt_refs:
        kt_ref, vt_ref, km_ref = prompt_refs
        kt_ref[0, 0] = k.T.astype(BF16)
        vt_ref[0, 0] = v.T.astype(BF16)
        km_ref[0] = jnp.mean(k, axis=0, keepdims=True)


def _pre_b(x1, g_kv, w_kv, g_pre_b, w_in_b, tm, prec, q_dtype, prompt_blocks=None):
    m, d = x1.shape
    row = lambda wd: pl.BlockSpec((tm, wd), lambda i: (i, 0))
    out_specs = [row(KV_WIDTH), row(KV_WIDTH), row(d), row(d)]
    out_shape = [jax.ShapeDtypeStruct((m, KV_WIDTH), F32), jax.ShapeDtypeStruct((m, KV_WIDTH), F32),
                 jax.ShapeDtypeStruct((m, d), q_dtype), jax.ShapeDtypeStruct((m, d), F32)]
    if prompt_blocks is not None:
        bsz, nb = prompt_blocks
        assert tm == MOBA_BLOCK and m == bsz * nb * tm
        t_spec = pl.BlockSpec((1, 1, KV_WIDTH, tm), lambda i: (i // nb, i % nb, 0, 0))
        out_specs += [t_spec, t_spec, pl.BlockSpec((1, 1, KV_WIDTH), lambda i: (i, 0, 0))]
        out_shape += [jax.ShapeDtypeStruct((bsz, nb, KV_WIDTH, tm), BF16),
                      jax.ShapeDtypeStruct((bsz, nb, KV_WIDTH, tm), BF16),
                      jax.ShapeDtypeStruct((bsz * nb, 1, KV_WIDTH), F32)]
    return pl.pallas_call(
        functools.partial(_pre_b_body, prec=prec),
        grid=(m // tm,),
        in_specs=[row(d), _full((1, d)), _full((d, 2 * KV_WIDTH)), _full((1, d)), _full((d, 2 * d))],
        out_specs=out_specs,
        out_shape=out_shape,
        compiler_params=_cparams(1),
        name="pre_b",
    )(x1, g_kv.reshape(1, d), w_kv, g_pre_b.reshape(1, d), w_in_b)


def _top3_mask(gs, lane, valid):
    n = gs.shape[-1]
    gs = jnp.where(valid, gs, NEG_INF)
    sel = jnp.zeros(gs.shape, F32)
    picks = []
    for _ in range(MOBA_TOPK):
        mx = jnp.max(gs, axis=-1, keepdims=True)
        first = jnp.min(jnp.where(gs == mx, lane, n), axis=-1, keepdims=True)
        pick = lane == first
        sel = jnp.where(pick & valid, 1.0, sel)
        gs = jnp.where(pick, REMOVED, gs)
        picks.append(first)
    return sel, picks


def _moba_prompt_body(q_ref, g_ref, kt_ref, vt_ref, km_ref, o_ref):
    h = pl.program_id(1)
    c = pl.program_id(2)
    blk = MOBA_BLOCK
    qt = q_ref[0]
    km_t = km_ref[0, 0]
    rq = lax.broadcasted_iota(jnp.int32, (blk, blk), 0)
    col = lax.broadcasted_iota(jnp.int32, (blk, blk), 1)
    d0 = (rq - col).astype(F32)
    causal = col <= rq
    lane = lax.broadcasted_iota(jnp.int32, (blk, LANES), 1)
    valid = lane < c
    outs = []
    for g in range(Q_PER_KV):
        q = qt[:, g * HEAD_DIM:(g + 1) * HEAD_DIM]
        head1 = (h * Q_PER_KV + g + 1).astype(F32)
        slope = jnp.exp2(-8.0 * jnp.full((blk, blk), head1, F32) / N_HEADS)
        sel, _ = _top3_mask(_dot(q.astype(F32), km_t, HIGHEST), lane, valid)
        sd = slope * d0
        s = jnp.where(causal, _dot(q, kt_ref[0, c]) - sd, NEG_INF)
        m = jnp.max(s, axis=-1, keepdims=True)
        p = jnp.exp(s - m)
        l = jnp.sum(p, axis=-1, keepdims=True)
        acc = _dot_nt(p.astype(BF16), vt_ref[0, c])

        def block(n, carry, q=q, sel=sel, sd=sd, slope=slope):
            m, l, acc = carry
            off = ((c - n) * blk).astype(F32)
            s = _dot(q, kt_ref[0, n]) - (sd + slope * off)
            sel_n = jnp.sum(jnp.where(lane == n, sel, 0.0), axis=-1, keepdims=True)
            s = jnp.where(sel_n > 0.0, s, NEG_INF)
            m2 = jnp.maximum(m, jnp.max(s, axis=-1, keepdims=True))
            a = jnp.exp(m - m2)
            p = jnp.exp(s - m2)
            l = a * l + jnp.sum(p, axis=-1, keepdims=True)
            acc = a * acc + _dot_nt(p.astype(BF16), vt_ref[0, n])
            return m2, l, acc

        m, l, acc = lax.fori_loop(0, c, block, (m, l, acc))
        outs.append(acc / l)
    att = jnp.concatenate(outs, axis=1)
    gate = g_ref[0]
    o_ref[0] = (att * (gate * _sigmoid(gate))).astype(o_ref.dtype)


def _moba_prompt(q, gate, kt, vt, km_t):
    bsz, seqlen, d = q.shape
    nb = seqlen // MOBA_BLOCK
    width = Q_PER_KV * HEAD_DIM
    tile = pl.BlockSpec((1, MOBA_BLOCK, width), lambda b, h, c: (b, c, h))
    kv_spec = pl.BlockSpec((1, nb, HEAD_DIM, MOBA_BLOCK), lambda b, h, c: (b, 0, h, 0))
    return pl.pallas_call(
        _moba_prompt_body,
        grid=(bsz, KV_HEADS, nb),
        in_specs=[tile, tile, kv_spec, kv_spec,
                  pl.BlockSpec((1, 1, HEAD_DIM, LANES), lambda b, h, c: (b, h, 0, 0))],
        out_specs=tile,
        out_shape=jax.ShapeDtypeStruct((bsz, seqlen, d), BF16),
        compiler_params=_cparams(3),
        name="moba_prompt",
    )(q, gate, kt, vt, km_t)


_KM_BLOCKS = 8


def _block_mean_body(pt_ref, *refs):
    del pt_ref
    pages, o_ref = refs[:-1], refs[-1]
    rows = []
    for j in range(_KM_BLOCKS):
        tot = sum(jnp.sum(pages[PAGES_PER_BLOCK * j + r][0], axis=0, keepdims=True)
                  for r in range(PAGES_PER_BLOCK))
        rows.append(tot * (1.0 / MOBA_BLOCK))
    o_ref[0] = jnp.concatenate(rows, axis=0)


def _block_means(cache2, page_table, n_blocks):
    n_seq = page_table.shape[0]
    per_step = _KM_BLOCKS * PAGES_PER_BLOCK
    page_spec = lambda j: pl.BlockSpec(
        (1, PAGE_SIZE, KV_WIDTH), lambda b, s, pt, j=j: (pt[b, s * per_step + j], 0, 0))
    return pl.pallas_call(
        _block_mean_body,
        grid_spec=pltpu.PrefetchScalarGridSpec(
            num_scalar_prefetch=1,
            grid=(n_seq, n_blocks // _KM_BLOCKS),
            in_specs=[page_spec(j) for j in range(per_step)],
            out_specs=pl.BlockSpec((1, _KM_BLOCKS, KV_WIDTH), lambda b, s, pt: (b, s, 0))),
        out_shape=jax.ShapeDtypeStruct((n_seq, n_blocks, KV_WIDTH), F32),
        compiler_params=_cparams(2),
        name="block_means",
    )(page_table, *([cache2] * per_step))


def _sample_topk_body(q_ref, km_ref, idx_ref, *, n_past):
    gs = _dot_nt(q_ref[0], km_ref[0], HIGHEST)
    lane = lax.broadcasted_iota(jnp.int32, gs.shape, 1)
    _, picks = _top3_mask(gs, lane, lane < n_past)
    out_lane = lax.broadcasted_iota(jnp.int32, idx_ref.shape[1:], 1)
    out = jnp.zeros(idx_ref.shape[1:], jnp.int32)
    for r, first in enumerate(picks):
        out = jnp.where(out_lane == r, first, out)
    idx_ref[0] = out


def _sample_topk(q_exp, km):
    n_seq, n_heads, width = q_exp.shape
    n_past = km.shape[1]
    return pl.pallas_call(
        functools.partial(_sample_topk_body, n_past=n_past),
        grid=(n_seq,),
        in_specs=[pl.BlockSpec((1, n_heads, width), lambda b: (b, 0, 0)),
                  pl.BlockSpec((1, n_past, width), lambda b: (b, 0, 0))],
        out_specs=pl.BlockSpec((1, n_heads, LANES), lambda b: (b, 0, 0)),
        out_shape=jax.ShapeDtypeStruct((n_seq, n_heads, LANES), jnp.int32),
        compiler_params=_cparams(1),
        name="sample_topk",
    )(q_exp, km)


_Q_ROWS = 8


def _moba_sample_body(pt_ref, idx_ref, q_ref, kn_ref, vn_ref, *refs, n_past):
    del pt_ref
    n_sel = MOBA_TOPK * PAGES_PER_BLOCK
    k_pages, v_pages, o_ref = refs[:n_sel], refs[n_sel:2 * n_sel], refs[-1]
    b = pl.program_id(0)
    h = pl.program_id(1)
    keys = MOBA_TOPK * MOBA_BLOCK
    q = jnp.broadcast_to(q_ref[0, 0], (_Q_ROWS, KV_WIDTH))
    k_sel = jnp.concatenate([r[0] for r in k_pages], axis=0)
    v_sel = jnp.concatenate([r[0] for r in v_pages], axis=0)
    col = lax.broadcasted_iota(jnp.int32, (_Q_ROWS, keys), 1)
    slot = col // MOBA_BLOCK
    base = (b * N_HEADS + h) * MOBA_TOPK
    idx = jnp.zeros((_Q_ROWS, keys), jnp.int32)
    for r in range(MOBA_TOPK):
        idx = jnp.where(slot == r, idx_ref[base + r], idx)
    pos = n_past * MOBA_BLOCK
    dist = (pos - (idx * MOBA_BLOCK + col % MOBA_BLOCK)).astype(F32)
    slope = jnp.exp2(-8.0 * jnp.full((_Q_ROWS, keys), (h + 1).astype(F32), F32) / N_HEADS)
    s = _dot_nt(q, k_sel, HIGHEST) - slope * dist
    s = jnp.where(idx < n_past, s, NEG_INF)
    s_own = jnp.sum(q * kn_ref[0], axis=-1, keepdims=True)
    m = jnp.maximum(jnp.max(s, axis=-1, keepdims=True), s_own)
    p = jnp.exp(s - m)
    p_own = jnp.exp(s_own - m)
    l = jnp.sum(p, axis=-1, keepdims=True) + p_own
    out = (_dot(p, v_sel, HIGHEST) + p_own * vn_ref[0]) / l
    lane = lax.broadcasted_iota(jnp.int32, out.shape, 1)
    out = jnp.where(lane // HEAD_DIM == h // Q_PER_KV, out, 0.0)
    folded = sum(out[:, j * HEAD_DIM:(j + 1) * HEAD_DIM] for j in range(KV_HEADS))
    o_ref[0, 0] = folded[0:1]


def _moba_sample(q_exp, k_new, v_new, cache_k2, cache_v2, page_table, idx_flat, n_past):
    n_seq = page_table.shape[0]

    def page_spec(j):
        r, pg = divmod(j, PAGES_PER_BLOCK)

        def index(b, h, pt, idx):
            blk = idx[(b * N_HEADS + h) * MOBA_TOPK + r]
            return (pt[b, blk * PAGES_PER_BLOCK + pg], 0, 0)

        return pl.BlockSpec((1, PAGE_SIZE, KV_WIDTH), index)

    n_sel = MOBA_TOPK * PAGES_PER_BLOCK
    new_spec = pl.BlockSpec((1, 1, KV_WIDTH), lambda b, h, pt, idx: (b, 0, 0))
    return pl.pallas_call(
        functools.partial(_moba_sample_body, n_past=n_past),
        grid_spec=pltpu.PrefetchScalarGridSpec(
            num_scalar_prefetch=2,
            grid=(n_seq, N_HEADS),
            in_specs=[pl.BlockSpec((1, 1, 1, KV_WIDTH), lambda b, h, pt, idx: (b, h, 0, 0)),
                      new_spec, new_spec]
                     + [page_spec(j) for j in range(n_sel)] * 2,
            out_specs=pl.BlockSpec((1, 1, 1, HEAD_DIM), lambda b, h, pt, idx: (b, h, 0, 0))),
        out_shape=jax.ShapeDtypeStruct((n_seq, N_HEADS, 1, HEAD_DIM), F32),
        compiler_params=_cparams(2),
        name="moba_sample",
    )(page_table, idx_flat, q_exp.reshape(n_seq, N_HEADS, 1, KV_WIDTH),
      k_new.reshape(n_seq, 1, KV_WIDTH), v_new.reshape(n_seq, 1, KV_WIDTH),
      *([cache_k2] * n_sel), *([cache_v2] * n_sel))


def _out_b_body(a_ref, *refs, prec, gated):
    if gated:
        x1_ref, w_ref, g_ref, y_ref = refs
        a = a_ref[...]
    else:
        gate_ref, x1_ref, w_ref, g_ref, y_ref = refs
        gate = gate_ref[...]
        a = (a_ref[...] * (gate * _sigmoid(gate))).astype(w_ref.dtype)
    o = _dot(a, w_ref[...], prec)
    y_ref[...] = x1_ref[...] + _rms_scale(o) * g_ref[...]


def _out_b(att, gate, x1, w, g_post, tm, prec):
    m, d = x1.shape
    tile = pl.BlockSpec((tm, d), lambda i: (i, 0))
    gated = gate is None
    acts = [att, x1] if gated else [att, gate, x1]
    return pl.pallas_call(
        functools.partial(_out_b_body, prec=prec, gated=gated),
        grid=(m // tm,),
        in_specs=[tile] * len(acts) + [_full((d, d)), _full((1, d))],
        out_specs=tile,
        out_shape=jax.ShapeDtypeStruct((m, d), F32),
        compiler_params=_cparams(1),
        name="out_b",
    )(*acts, w, g_post.reshape(1, d))


def _prompt_trunk(x, p):
    bsz, seqlen, d = x.shape
    m = bsz * seqlen
    nb = seqlen // MOBA_BLOCK
    nc = seqlen // S5_CHUNK
    g, pch = SSM_GROUPS, SSM_GROUP
    x2 = x.reshape(m, d)
    u, gate_a = _norm_matmul(x2, p['g_pre_a'], p['w_in_a'].astype(BF16), (d, d), (BF16, F32), 512, None)
    ar, ai, bb_re, bb_im = _s5_discretize(p['lambda_re'], p['lambda_im'], p['log_dt'], p['b_re'], p['b_im'])
    ops = _s5_prompt_operators(ar, ai, bb_re, bb_im, p['c_re'], p['c_im'], p['d_skip'])
    u2 = u.reshape(bsz, nc, S5_CHUNK, g, pch).transpose(3, 1, 0, 2, 4).reshape(g, nc * bsz, S5_CHUNK * pch)
    y2, h_re, h_im = _s5_prompt(u2, ops, bsz)
    y = y2.reshape(g, nc, bsz, S5_CHUNK, pch).transpose(2, 1, 3, 0, 4).reshape(m, d)
    ssm_re = h_re[:, :, :SSM_STATE].transpose(1, 0, 2)[None]
    ssm_im = h_im[:, :, :SSM_STATE].transpose(1, 0, 2)[None]
    x1 = _post_a(y, gate_a, x2, p['w_glu'].astype(BF16), p['b_glu'], p['w_out_a'].astype(BF16),
                 p['g_post_a'], 256, None)
    k, v, q, gate_b, kt, vt, km = _pre_b(x1, p['g_kv'], p['w_kv'].astype(BF16), p['g_pre_b'],
                                         p['w_in_b'].astype(BF16), MOBA_BLOCK, None, BF16, (bsz, nb))
    km_t = km.reshape(bsz, nb, KV_HEADS, HEAD_DIM).transpose(0, 2, 3, 1)
    km_t = jnp.pad(km_t, ((0, 0), (0, 0), (0, 0), (0, LANES - nb)))
    att = _moba_prompt(q.reshape(bsz, seqlen, d), gate_b.reshape(bsz, seqlen, d), kt, vt, km_t)
    y_out = _out_b(att.reshape(m, d), None, x1, p['w_out_b'].astype(BF16), p['g_post_b'], 512, None)
    return (y_out.reshape(bsz, seqlen, d), ssm_re, ssm_im,
            k.reshape(bsz, seqlen, KV_HEADS, HEAD_DIM), v.reshape(bsz, seqlen, KV_HEADS, HEAD_DIM))


def _sample_trunk(x, h0_re, h0_im, cache_k, cache_v, page_table, p):
    n_seq, seqlen, d = x.shape
    assert seqlen == 1
    g, pch = SSM_GROUPS, SSM_GROUP
    n_past = page_table.shape[1] // PAGES_PER_BLOCK
    x2 = x.reshape(n_seq, d)
    u, gate_a = _norm_matmul(x2, p['g_pre_a'], p['w_in_a'], (d, d), (F32, F32), n_seq, HIGHEST)
    ar, ai, bb_re, bb_im = _s5_discretize(p['lambda_re'], p['lambda_im'], p['log_dt'], p['b_re'], p['b_im'])
    y_g, hr, hi = _s5_step(u.reshape(n_seq, g, pch).transpose(1, 0, 2),
                           h0_re.transpose(1, 0, 2), h0_im.transpose(1, 0, 2),
                           ar, ai, bb_re, bb_im, p['c_re'], p['c_im'], p['d_skip'])
    y = y_g.transpose(1, 0, 2).reshape(n_seq, d)
    x1 = _post_a(y, gate_a, x2, p['w_glu'], p['b_glu'], p['w_out_a'], p['g_post_a'], n_seq, HIGHEST)
    k, v, q, gate_b = _pre_b(x1, p['g_kv'], p['w_kv'], p['g_pre_b'], p['w_in_b'], n_seq, HIGHEST, F32)
    cache_k2 = cache_k.reshape(cache_k.shape[0], PAGE_SIZE, KV_WIDTH)
    cache_v2 = cache_v.reshape(cache_v.shape[0], PAGE_SIZE, KV_WIDTH)
    km = _block_means(cache_k2, page_table, n_past)
    eye = jnp.eye(KV_HEADS, dtype=F32)
    q_exp = (q.reshape(n_seq, KV_HEADS, Q_PER_KV, 1, HEAD_DIM) * eye[None, :, None, :, None])
    q_exp = q_exp.reshape(n_seq, N_HEADS, KV_WIDTH)
    idx = _sample_topk(q_exp, km)[:, :, :MOBA_TOPK].reshape(-1)
    att = _moba_sample(q_exp, k, v, cache_k2, cache_v2, page_table, idx, n_past)
    y_out = _out_b(att.reshape(n_seq, d), gate_b, x1, p['w_out_b'], p['g_post_b'], n_seq, HIGHEST)
    return (y_out.reshape(n_seq, 1, d), hr.transpose(1, 0, 2)[None], hi.transpose(1, 0, 2)[None],
            k.reshape(n_seq, 1, KV_HEADS, HEAD_DIM), v.reshape(n_seq, 1, KV_HEADS, HEAD_DIM))


def kernel(x_prompt, x_sample, state_ssm_re, state_ssm_im, cache_k, cache_v, page_table, g_pre_a, w_in_a, lambda_re, lambda_im, log_dt, b_re, b_im, c_re, c_im, d_skip, w_glu, b_glu, w_out_a, g_post_a, g_kv, w_kv, g_pre_b, w_in_b, w_out_b, g_post_b):
    p = {'g_pre_a': g_pre_a[0], 'w_in_a': w_in_a[0], 'lambda_re': lambda_re[0], 'lambda_im': lambda_im[0],
         'log_dt': log_dt[0], 'b_re': b_re[0], 'b_im': b_im[0], 'c_re': c_re[0], 'c_im': c_im[0],
         'd_skip': d_skip[0], 'w_glu': w_glu[0], 'b_glu': b_glu[0], 'w_out_a': w_out_a[0],
         'g_post_a': g_post_a[0], 'g_kv': g_kv, 'w_kv': w_kv, 'g_pre_b': g_pre_b[0],
         'w_in_b': w_in_b[0], 'w_out_b': w_out_b[0], 'g_post_b': g_post_b[0]}
    y_p, re_p, im_p, k_p, v_p = _prompt_trunk(x_prompt, p)
    y_s, re_s, im_s, k_s, v_s = _sample_trunk(x_sample, state_ssm_re[0], state_ssm_im[0],
                                              cache_k, cache_v, page_table, p)
    return (y_p, y_s, re_p, im_p, k_p, v_p, re_s, im_s, k_s, v_s)
```

```python
import functools
import math

import jax
import jax.numpy as jnp
from jax import lax
from jax.experimental import pallas as pl
from jax.experimental.pallas import tpu as pltpu

F32 = jnp.float32
BF16 = jnp.bfloat16
HIGHEST = lax.Precision.HIGHEST

D_MODEL = 1024
SSM_GROUP = 16
SSM_GROUPS = D_MODEL // SSM_GROUP
SSM_STATE = 64
HEAD_DIM = 64
N_HEADS = D_MODEL // HEAD_DIM
KV_HEADS = 4
Q_PER_KV = N_HEADS // KV_HEADS
KV_WIDTH = KV_HEADS * HEAD_DIM
MOBA_BLOCK = 256
MOBA_TOPK = 3
PAGE_SIZE = 128
PAGES_PER_BLOCK = MOBA_BLOCK // PAGE_SIZE
EPS = 1e-6
NEG_INF = -1e30
REMOVED = -3e38
SCALE = HEAD_DIM ** -0.5
S5_CHUNK = 16
LANES = 128
VMEM_LIMIT = 56 * 1024 * 1024


def _cparams(n_grid):
    return pltpu.CompilerParams(dimension_semantics=("arbitrary",) * n_grid,
                                vmem_limit_bytes=VMEM_LIMIT)


def _dot(a, b, prec=None):
    return jnp.dot(a, b, preferred_element_type=F32, precision=prec)


def _dot_nt(a, b, prec=None):
    return lax.dot_general(a, b, (((1,), (1,)), ((), ())),
                           preferred_element_type=F32, precision=prec)


def _rms_scale(x):
    return x * lax.rsqrt(jnp.mean(x * x, axis=-1, keepdims=True) + EPS)


def _sigmoid(x):
    return 1.0 / (1.0 + jnp.exp(-x))


def _full(shape):
    zeros = (0,) * len(shape)
    return pl.BlockSpec(shape, lambda *_: zeros)


def _norm_matmul_body(x_ref, g_ref, w_ref, *out_refs, prec):
    h = (_rms_scale(x_ref[...]) * g_ref[...]).astype(w_ref.dtype)
    off = 0
    for o_ref in out_refs:
        n = o_ref.shape[-1]
        o_ref[...] = _dot(h, w_ref[:, off:off + n], prec).astype(o_ref.dtype)
        off += n


def _norm_matmul(x, g, w, out_widths, out_dtypes, tm, prec):
    m, d = x.shape
    n = w.shape[1]
    return pl.pallas_call(
        functools.partial(_norm_matmul_body, prec=prec),
        grid=(m // tm,),
        in_specs=[pl.BlockSpec((tm, d), lambda i: (i, 0)), _full((1, d)), _full((d, n))],
        out_specs=[pl.BlockSpec((tm, wd), lambda i: (i, 0)) for wd in out_widths],
        out_shape=[jax.ShapeDtypeStruct((m, wd), dt) for wd, dt in zip(out_widths, out_dtypes)],
        compiler_params=_cparams(1),
        name="norm_matmul",
    )(x, g.reshape(1, d), w)


def _s5_prompt_body(u_ref, m_ref, sre_ref, sim_ref, rre_ref, rim_ref, a_ref, d_ref,
                    y_ref, hre_ref, him_ref, er_s, ei_s, hr_s, hi_s, *, n_chunks, bsz):
    u = u_ref[0]
    er_s[...] = _dot(u, sre_ref[0])
    ei_s[...] = _dot(u, sim_ref[0])
    ar = jnp.broadcast_to(a_ref[0, 0:1, :], (bsz, LANES))
    ai = jnp.broadcast_to(a_ref[0, 1:2, :], (bsz, LANES))

    def step(c, carry):
        hr, hi = carry
        rows = pl.ds(pl.multiple_of(c * bsz, bsz), bsz)
        hr_s[rows, :] = hr
        hi_s[rows, :] = hi
        return (ar * hr - ai * hi + er_s[rows, :], ar * hi + ai * hr + ei_s[rows, :])

    zero = jnp.zeros((bsz, LANES), F32)
    hr, hi = lax.fori_loop(0, n_chunks, step, (zero, zero))
    hre_ref[0] = hr
    him_ref[0] = hi
    y = _dot(u, m_ref[0])
    y += _dot(hr_s[...].astype(BF16), rre_ref[0])
    y += _dot(hi_s[...].astype(BF16), rim_ref[0])
    y_ref[0] = y + d_ref[0] * u.astype(F32)


def _s5_prompt(u2, ops, bsz):
    g, rows, width = u2.shape
    n_chunks = rows // bsz
    m, sre, sim, rre, rim, a16, dt = ops
    per_group = lambda shape: pl.BlockSpec((1,) + shape, lambda i: (i, 0, 0))
    return pl.pallas_call(
        functools.partial(_s5_prompt_body, n_chunks=n_chunks, bsz=bsz),
        grid=(g,),
        in_specs=[per_group((rows, width)), per_group((width, width)),
                  per_group((width, LANES)), per_group((width, LANES)),
                  per_group((LANES, width)), per_group((LANES, width)),
                  per_group((2, LANES)), per_group((1, width))],
        out_specs=[per_group((rows, width)), per_group((bsz, LANES)), per_group((bsz, LANES))],
        out_shape=[jax.ShapeDtypeStruct((g, rows, width), F32),
                   jax.ShapeDtypeStruct((g, bsz, LANES), F32),
                   jax.ShapeDtypeStruct((g, bsz, LANES), F32)],
        scratch_shapes=[pltpu.VMEM((rows, LANES), F32) for _ in range(4)],
        compiler_params=_cparams(1),
        name="s5_prompt",
    )(u2, m, sre, sim, rre, rim, a16, dt)


def _s5_discretize(lam_re, lam_im, log_dt, b_re, b_im):
    dt = jnp.exp(log_dt)[:, None]
    mag = jnp.exp(lam_re * dt)
    ar = mag * jnp.cos(lam_im * dt)
    ai = mag * jnp.sin(lam_im * dt)
    den = lam_re * lam_re + lam_im * lam_im
    nr = ar - 1.0
    coef_re = (nr * lam_re + ai * lam_im) / den
    coef_im = (ai * lam_re - nr * lam_im) / den
    bb_re = coef_re[..., None] * b_re - coef_im[..., None] * b_im
    bb_im = coef_re[..., None] * b_im + coef_im[..., None] * b_re
    return ar, ai, bb_re, bb_im


def _s5_prompt_operators(ar, ai, bb_re, bb_im, c_re, c_im, d_skip):
    t = S5_CHUNK
    g, n = ar.shape
    p = SSM_GROUP
    pr, pi = [jnp.ones_like(ar)], [jnp.zeros_like(ar)]
    for _ in range(t):
        pr, pi = pr + [pr[-1] * ar - pi[-1] * ai], pi + [pr[-1] * ai + pi[-1] * ar]
    pr = jnp.stack(pr)
    pi = jnp.stack(pi)
    cp_re = c_re[None] * pr[:, :, None, :] - c_im[None] * pi[:, :, None, :]
    cp_im = c_re[None] * pi[:, :, None, :] + c_im[None] * pr[:, :, None, :]
    k = (jnp.einsum('tgpn,gnq->gtpq', cp_re[:t], bb_re, precision=HIGHEST)
         - jnp.einsum('tgpn,gnq->gtpq', cp_im[:t], bb_im, precision=HIGHEST))
    s_idx = jnp.arange(t)[:, None]
    t_idx = jnp.arange(t)[None, :]
    lag = t_idx - s_idx
    kt = k[:, jnp.clip(lag, 0, t - 1)]
    kt = jnp.where((lag >= 0)[None, :, :, None, None], kt, 0.0)
    m = kt.transpose(0, 1, 4, 2, 3).reshape(g, t * p, t * p)
    pr_rev = pr[t - 1::-1][:t]
    pi_rev = pi[t - 1::-1][:t]
    s_re = pr_rev[:, :, :, None] * bb_re[None] - pi_rev[:, :, :, None] * bb_im[None]
    s_im = pr_rev[:, :, :, None] * bb_im[None] + pi_rev[:, :, :, None] * bb_re[None]
    pad_lanes = lambda x: jnp.pad(x, ((0, 0), (0, 0), (0, LANES - n)))
    s_re = pad_lanes(s_re.transpose(1, 0, 3, 2).reshape(g, t * p, n))
    s_im = pad_lanes(s_im.transpose(1, 0, 3, 2).reshape(g, t * p, n))
    pad_rows = lambda x: jnp.pad(x, ((0, 0), (0, LANES - n), (0, 0)))
    r_re = pad_rows(cp_re[1:].transpose(1, 3, 0, 2).reshape(g, n, t * p))
    r_im = pad_rows(-cp_im[1:].transpose(1, 3, 0, 2).reshape(g, n, t * p))
    a16 = pad_lanes(jnp.stack([pr[t], pi[t]], axis=1))
    dt_ = jnp.tile(d_skip.reshape(g, 1, p), (1, 1, t))
    return (m.astype(BF16), s_re.astype(BF16), s_im.astype(BF16),
            r_re.astype(BF16), r_im.astype(BF16), a16, dt_)


def _s5_step_body(u_ref, h0r_ref, h0i_ref, ar_ref, ai_ref, bbr_ref, bbi_ref, cr_ref, ci_ref,
                  d_ref, y_ref, hr_ref, hi_ref):
    u = u_ref[0]
    ar, ai = ar_ref[0], ai_ref[0]
    h0r, h0i = h0r_ref[0], h0i_ref[0]
    hr = _dot_nt(u, bbr_ref[0], HIGHEST) + ar * h0r - ai * h0i
    hi = _dot_nt(u, bbi_ref[0], HIGHEST) + ar * h0i + ai * h0r
    hr_ref[0] = hr
    hi_ref[0] = hi
    y_ref[0] = (_dot_nt(hr, cr_ref[0], HIGHEST) - _dot_nt(hi, ci_ref[0], HIGHEST)
                + d_ref[0] * u)


def _s5_step(u_g, h0r, h0i, ar, ai, bb_re, bb_im, c_re, c_im, d_skip):
    g, b, p = u_g.shape
    n = ar.shape[-1]
    per_group = lambda shape: pl.BlockSpec((1,) + shape, lambda i: (i, 0, 0))
    return pl.pallas_call(
        _s5_step_body,
        grid=(g,),
        in_specs=[per_group((b, p)), per_group((b, n)), per_group((b, n)),
                  per_group((1, n)), per_group((1, n)),
                  per_group((n, p)), per_group((n, p)),
                  per_group((p, n)), per_group((p, n)), per_group((1, p))],
        out_specs=[per_group((b, p)), per_group((b, n)), per_group((b, n))],
        out_shape=[jax.ShapeDtypeStruct((g, b, p), F32),
                   jax.ShapeDtypeStruct((g, b, n), F32),
                   jax.ShapeDtypeStruct((g, b, n), F32)],
        compiler_params=_cparams(1),
        name="s5_step",
    )(u_g, h0r, h0i, ar.reshape(g, 1, n), ai.reshape(g, 1, n), bb_re, bb_im, c_re, c_im,
      d_skip.reshape(g, 1, p))


def _post_a_body(y_ref, ga_ref, x_ref, wglu_ref, bglu_ref, wout_ref, gpost_ref, x1_ref, *, prec):
    y = y_ref[...]
    wdt = wglu_ref.dtype
    gy = 0.5 * y * (1.0 + lax.erf(y * math.sqrt(0.5)))
    z = _dot(gy.astype(wdt), wglu_ref[...], prec) + bglu_ref[...]
    gate = ga_ref[...]
    t = (gy * _sigmoid(z)) * (gate * _sigmoid(gate))
    o = _dot(t.astype(wdt), wout_ref[...], prec)
    x1_ref[...] = x_ref[...] + _rms_scale(o) * gpost_ref[...]


def _post_a(y, gate, x, w_glu, b_glu, w_out, g_post, tm, prec):
    m, d = x.shape
    tile = pl.BlockSpec((tm, d), lambda i: (i, 0))
    return pl.pallas_call(
        functools.partial(_post_a_body, prec=prec),
        grid=(m // tm,),
        in_specs=[tile, tile, tile, _full((d, d)), _full((1, d)), _full((d, d)), _full((1, d))],
        out_specs=tile,
        out_shape=jax.ShapeDtypeStruct((m, d), F32),
        compiler_params=_cparams(1),
        name="post_a",
    )(y, gate, x, w_glu, b_glu.reshape(1, d), w_out, g_post.reshape(1, d))


ALIBI_ROWS = 16
FEAT_OFF = HEAD_DIM
V_ROWS = HEAD_DIM + 16


def _aug_width(nb):
    nbp = -(-nb // 16) * 16
    return -(-(FEAT_OFF + ALIBI_ROWS + nbp) // LANES) * LANES, nbp


def _pre_b_body(x1_ref, gkv_ref, wkv_ref, gb_ref, winb_ref, k_ref, v_ref, gate_ref, *rest, prec, nb):
    xn = _rms_scale(x1_ref[...])
    wdt = wkv_ref.dtype
    hk = (xn * gkv_ref[...]).astype(wdt)
    k = _dot(hk, wkv_ref[:, :KV_WIDTH], prec)
    v = _dot(hk, wkv_ref[:, KV_WIDTH:], prec)
    k_ref[...] = k
    v_ref[...] = v
    hq = (xn * gb_ref[...]).astype(wdt)
    q = _dot(hq, winb_ref[:, :D_MODEL], prec) * SCALE
    gate_ref[...] = _dot(hq, winb_ref[:, D_MODEL:], prec)
    if nb is None:
        q_ref, = rest
        q_ref[...] = q
        return
    qt_ref, ka_ref, vta_ref, km_ref = rest
    tm = q.shape[0]
    qt_ref[0, 0] = q.T.astype(BF16)
    km_ref[0] = jnp.mean(k, axis=0, keepdims=True)
    n = pl.program_id(0) % nb
    width = ka_ref.shape[-1] - FEAT_OFF
    lane = lax.broadcasted_iota(jnp.int32, (tm, width), 1)
    rowi = lax.broadcasted_iota(jnp.int32, (tm, width), 0)
    feat = jnp.where(lane < 3, n, jnp.where(lane < 6, rowi, jnp.where(lane < 9, 1, 0)))
    feat = jnp.where(lane - ALIBI_ROWS == n, 1, feat).astype(F32).astype(BF16)
    kb = k.astype(BF16)
    vt = v.T
    ones_row = (lax.broadcasted_iota(jnp.int32, (V_ROWS - HEAD_DIM, tm), 0) == 0).astype(F32).astype(BF16)
    for h in range(KV_HEADS):
        ka_ref[0, h, :, :FEAT_OFF] = kb[:, h * HEAD_DIM:(h + 1) * HEAD_DIM]
        ka_ref[0, h, :, FEAT_OFF:] = feat
        vta_ref[0, 0, h * V_ROWS:h * V_ROWS + HEAD_DIM, :] = vt[h * HEAD_DIM:(h + 1) * HEAD_DIM].astype(BF16)
        vta_ref[0, 0, h * V_ROWS + HEAD_DIM:(h + 1) * V_ROWS, :] = ones_row


def _pre_b(x1, g_kv, w_kv, g_pre_b, w_in_b, tm, prec, prompt_blocks=None):
    m, d = x1.shape
    row = lambda wd: pl.BlockSpec((tm, wd), lambda i: (i, 0))
    out_specs = [row(KV_WIDTH), row(KV_WIDTH), row(d)]
    out_shape = [jax.ShapeDtypeStruct((m, KV_WIDTH), F32), jax.ShapeDtypeStruct((m, KV_WIDTH), F32),
                 jax.ShapeDtypeStruct((m, d), F32)]
    nb = None
    if prompt_blocks is None:
        out_specs += [row(d)]
        out_shape += [jax.ShapeDtypeStruct((m, d), F32)]
    else:
        bsz, nb = prompt_blocks
        assert tm == MOBA_BLOCK and m == bsz * nb * tm
        aug, _ = _aug_width(nb)
        out_specs += [pl.BlockSpec((1, 1, d, tm), lambda i: (i // nb, i % nb, 0, 0)),
                      pl.BlockSpec((1, KV_HEADS, tm, aug), lambda i: (i // nb, 0, i % nb, 0)),
                      pl.BlockSpec((1, 1, KV_HEADS * V_ROWS, tm), lambda i: (i // nb, i % nb, 0, 0)),
                      pl.BlockSpec((1, 1, KV_WIDTH), lambda i: (i, 0, 0))]
        out_shape += [jax.ShapeDtypeStruct((bsz, nb, d, tm), BF16),
                      jax.ShapeDtypeStruct((bsz, KV_HEADS, nb * tm, aug), BF16),
                      jax.ShapeDtypeStruct((bsz, nb, KV_HEADS * V_ROWS, tm), BF16),
                      jax.ShapeDtypeStruct((bsz * nb, 1, KV_WIDTH), F32)]
    return pl.pallas_call(
        functools.partial(_pre_b_body, prec=prec, nb=nb),
        grid=(m // tm,),
        in_specs=[row(d), _full((1, d)), _full((d, 2 * KV_WIDTH)), _full((1, d)), _full((d, 2 * d))],
        out_specs=out_specs,
        out_shape=out_shape,
        compiler_params=_cparams(1),
        name="pre_b",
    )(x1, g_kv.reshape(1, d), w_kv, g_pre_b.reshape(1, d), w_in_b)


def _top3_mask(gs, idx, valid, axis):
    n = gs.shape[axis]
    gs = jnp.where(valid, gs, NEG_INF)
    sel = jnp.zeros(gs.shape, F32)
    picks = []
    for _ in range(MOBA_TOPK):
        mx = jnp.max(gs, axis=axis, keepdims=True)
        first = jnp.min(jnp.where(gs == mx, idx, n), axis=axis, keepdims=True)
        pick = idx == first
        sel = jnp.where(pick & valid, 1.0, sel)
        gs = jnp.where(pick, REMOVED, gs)
        picks.append(first)
    return sel, picks


def _split3(x):
    hi = x.astype(BF16).astype(F32)
    mid = (x - hi).astype(BF16).astype(F32)
    lo = (x - hi - mid).astype(BF16).astype(F32)
    return [hi, mid, lo]


def _moba_prompt_body(qt_ref, g_ref, ka_ref, vta_ref, km_ref, o_ref, qa_s, qd_s):
    h = pl.program_id(1)
    c = pl.program_id(2)
    blk = MOBA_BLOCK
    nb = km_ref.shape[2]
    aug, nbp = _aug_width(nb)
    km = km_ref[0, 0]
    row = lax.broadcasted_iota(jnp.int32, (nb, blk), 0)
    frow = lax.broadcasted_iota(jnp.int32, (ALIBI_ROWS, blk), 0)
    pos_q = (c * blk + lax.broadcasted_iota(jnp.int32, (1, blk), 1)).astype(F32)
    for g in range(Q_PER_KV):
        lanes = slice(g * blk, (g + 1) * blk)
        qg = qt_ref[0, 0, g * HEAD_DIM:(g + 1) * HEAD_DIM, :]
        sel, _ = _top3_mask(_dot(km, qg.astype(F32), HIGHEST), row, row < c, 0)
        bias = jnp.where(sel > 0.0, 0.0, NEG_INF)
        if nbp > nb:
            bias = jnp.concatenate([bias, jnp.zeros((nbp - nb, blk), F32)], axis=0)
        head1 = (h * Q_PER_KV + g + 1).astype(F32)
        slope = jnp.exp2(-8.0 * jnp.full((1, blk), head1, F32) / N_HEADS)
        terms = _split3(slope * float(blk)) + _split3(slope) + _split3(-slope * pos_q)
        feat = jnp.zeros((ALIBI_ROWS, blk), F32)
        for r, term in enumerate(terms):
            feat = jnp.where(frow == r, term, feat)
        for ref, b_rows in ((qa_s, bias.astype(BF16)), (qd_s, jnp.zeros((nbp, blk), BF16))):
            ref[:FEAT_OFF, lanes] = qg
            ref[FEAT_OFF:FEAT_OFF + ALIBI_ROWS, lanes] = feat.astype(BF16)
            ref[FEAT_OFF + ALIBI_ROWS:FEAT_OFF + ALIBI_ROWS + nbp, lanes] = b_rows
            if FEAT_OFF + ALIBI_ROWS + nbp < aug:
                ref[FEAT_OFF + ALIBI_ROWS + nbp:, lanes] = jnp.zeros((aug - FEAT_OFF - ALIBI_ROWS - nbp, blk), BF16)

    def keys(n):
        return ka_ref[0, 0, pl.ds(pl.multiple_of(n * blk, blk), blk), :]

    s = _dot(keys(c), qd_s[...])
    key = lax.broadcasted_iota(jnp.int32, s.shape, 0)
    qry = lax.broadcasted_iota(jnp.int32, s.shape, 1) & (blk - 1)
    s = jnp.where(key <= qry, s, NEG_INF)
    m = jnp.max(s, axis=0, keepdims=True)
    acc = _dot(vta_ref[0, c], jnp.exp(s - m).astype(BF16))

    def block_pair(i, carry):
        m, acc = carry
        n0, n1 = 2 * i, 2 * i + 1
        s0 = _dot(keys(n0), qa_s[...])
        s1 = _dot(keys(n1), qa_s[...])
        m2 = jnp.maximum(m, jnp.maximum(jnp.max(s0, axis=0, keepdims=True), jnp.max(s1, axis=0, keepdims=True)))
        p0 = jnp.exp(s0 - m2).astype(BF16)
        p1 = jnp.exp(s1 - m2).astype(BF16)
        return m2, jnp.exp(m - m2) * acc + _dot(vta_ref[0, n0], p0) + _dot(vta_ref[0, n1], p1)

    m, acc = lax.fori_loop(0, (c + 1) // 2, block_pair, (m, acc))
    out_t = acc[:HEAD_DIM] / acc[HEAD_DIM:HEAD_DIM + 1]
    att = jnp.concatenate([out_t[:, g * blk:(g + 1) * blk] for g in range(Q_PER_KV)], axis=0).T
    gate = g_ref[0]
    o_ref[0] = (att * (gate * _sigmoid(gate))).astype(o_ref.dtype)


def _moba_prompt(qt, gate, ka, vta, km):
    bsz, seqlen, d = gate.shape
    nb = seqlen // MOBA_BLOCK
    width = Q_PER_KV * HEAD_DIM
    aug, _ = _aug_width(nb)
    tile = pl.BlockSpec((1, MOBA_BLOCK, width), lambda b, h, c: (b, c, h))
    return pl.pallas_call(
        _moba_prompt_body,
        grid=(bsz, KV_HEADS, nb),
        in_specs=[pl.BlockSpec((1, 1, width, MOBA_BLOCK), lambda b, h, c: (b, c, h, 0)),
                  tile,
                  pl.BlockSpec((1, 1, seqlen, aug), lambda b, h, c: (b, h, 0, 0)),
                  pl.BlockSpec((1, nb, V_ROWS, MOBA_BLOCK), lambda b, h, c: (b, 0, h, 0)),
                  pl.BlockSpec((1, 1, nb, HEAD_DIM), lambda b, h, c: (b, h, 0, 0))],
        out_specs=tile,
        out_shape=jax.ShapeDtypeStruct((bsz, seqlen, d), BF16),
        scratch_shapes=[pltpu.VMEM((aug, Q_PER_KV * MOBA_BLOCK), BF16) for _ in range(2)],
        compiler_params=_cparams(3),
        name="moba_prompt",
    )(qt, gate, ka, vta, km)


_KM_BLOCKS = 8


def _block_mean_body(pt_ref, *refs):
    del pt_ref
    pages, o_ref = refs[:-1], refs[-1]
    rows = []
    for j in range(_KM_BLOCKS):
        tot = sum(jnp.sum(pages[PAGES_PER_BLOCK * j + r][0], axis=0, keepdims=True)
                  for r in range(PAGES_PER_BLOCK))
        rows.append(tot * (1.0 / MOBA_BLOCK))
    o_ref[0] = jnp.concatenate(rows, axis=0)


def _block_means(cache2, page_table, n_blocks):
    n_seq = page_table.shape[0]
    per_step = _KM_BLOCKS * PAGES_PER_BLOCK
    page_spec = lambda j: pl.BlockSpec(
        (1, PAGE_SIZE, KV_WIDTH), lambda b, s, pt, j=j: (pt[b, s * per_step + j], 0, 0))
    return pl.pallas_call(
        _block_mean_body,
        grid_spec=pltpu.PrefetchScalarGridSpec(
            num_scalar_prefetch=1,
            grid=(n_seq, n_blocks // _KM_BLOCKS),
            in_specs=[page_spec(j) for j in range(per_step)],
            out_specs=pl.BlockSpec((1, _KM_BLOCKS, KV_WIDTH), lambda b, s, pt: (b, s, 0))),
        out_shape=jax.ShapeDtypeStruct((n_seq, n_blocks, KV_WIDTH), F32),
        compiler_params=_cparams(2),
        name="block_means",
    )(page_table, *([cache2] * per_step))


def _sample_topk_body(q_ref, km_ref, idx_ref, *, n_past):
    gs = _dot_nt(q_ref[0], km_ref[0], HIGHEST)
    lane = lax.broadcasted_iota(jnp.int32, gs.shape, 1)
    _, picks = _top3_mask(gs, lane, lane < n_past, 1)
    out_lane = lax.broadcasted_iota(jnp.int32, idx_ref.shape[1:], 1)
    out = jnp.zeros(idx_ref.shape[1:], jnp.int32)
    for r, first in enumerate(picks):
        out = jnp.where(out_lane == r, first, out)
    idx_ref[0] = out


def _sample_topk(q_exp, km):
    n_seq, n_heads, width = q_exp.shape
    n_past = km.shape[1]
    return pl.pallas_call(
        functools.partial(_sample_topk_body, n_past=n_past),
        grid=(n_seq,),
        in_specs=[pl.BlockSpec((1, n_heads, width), lambda b: (b, 0, 0)),
                  pl.BlockSpec((1, n_past, width), lambda b: (b, 0, 0))],
        out_specs=pl.BlockSpec((1, n_heads, LANES), lambda b: (b, 0, 0)),
        out_shape=jax.ShapeDtypeStruct((n_seq, n_heads, LANES), jnp.int32),
        compiler_params=_cparams(1),
        name="sample_topk",
    )(q_exp, km)


_Q_ROWS = 8


def _moba_sample_body(pt_ref, idx_ref, q_ref, kn_ref, vn_ref, *refs, n_past):
    del pt_ref
    n_sel = MOBA_TOPK * PAGES_PER_BLOCK
    k_pages, v_pages, o_ref = refs[:n_sel], refs[n_sel:2 * n_sel], refs[-1]
    b = pl.program_id(0)
    h = pl.program_id(1)
    keys = MOBA_TOPK * MOBA_BLOCK
    q = jnp.broadcast_to(q_ref[0, 0], (_Q_ROWS, KV_WIDTH))
    k_sel = jnp.concatenate([r[0] for r in k_pages], axis=0)
    v_sel = jnp.concatenate([r[0] for r in v_pages], axis=0)
    col = lax.broadcasted_iota(jnp.int32, (_Q_ROWS, keys), 1)
    slot = col // MOBA_BLOCK
    base = (b * N_HEADS + h) * MOBA_TOPK
    idx = jnp.zeros((_Q_ROWS, keys), jnp.int32)
    for r in range(MOBA_TOPK):
        idx = jnp.where(slot == r, idx_ref[base + r], idx)
    pos = n_past * MOBA_BLOCK
    dist = (pos - (idx * MOBA_BLOCK + col % MOBA_BLOCK)).astype(F32)
    slope = jnp.exp2(-8.0 * jnp.full((_Q_ROWS, keys), (h + 1).astype(F32), F32) / N_HEADS)
    s = _dot_nt(q, k_sel, HIGHEST) - slope * dist
    s = jnp.where(idx < n_past, s, NEG_INF)
    s_own = jnp.sum(q * kn_ref[0], axis=-1, keepdims=True)
    m = jnp.maximum(jnp.max(s, axis=-1, keepdims=True), s_own)
    p = jnp.exp(s - m)
    p_own = jnp.exp(s_own - m)
    l = jnp.sum(p, axis=-1, keepdims=True) + p_own
    out = (_dot(p, v_sel, HIGHEST) + p_own * vn_ref[0]) / l
    lane = lax.broadcasted_iota(jnp.int32, out.shape, 1)
    out = jnp.where(lane // HEAD_DIM == h // Q_PER_KV, out, 0.0)
    folded = sum(out[:, j * HEAD_DIM:(j + 1) * HEAD_DIM] for j in range(KV_HEADS))
    o_ref[0, 0] = folded[0:1]


def _moba_sample(q_exp, k_new, v_new, cache_k2, cache_v2, page_table, idx_flat, n_past):
    n_seq = page_table.shape[0]

    def page_spec(j):
        r, pg = divmod(j, PAGES_PER_BLOCK)

        def index(b, h, pt, idx):
            blk = idx[(b * N_HEADS + h) * MOBA_TOPK + r]
            return (pt[b, blk * PAGES_PER_BLOCK + pg], 0, 0)

        return pl.BlockSpec((1, PAGE_SIZE, KV_WIDTH), index)

    n_sel = MOBA_TOPK * PAGES_PER_BLOCK
    new_spec = pl.BlockSpec((1, 1, KV_WIDTH), lambda b, h, pt, idx: (b, 0, 0))
    return pl.pallas_call(
        functools.partial(_moba_sample_body, n_past=n_past),
        grid_spec=pltpu.PrefetchScalarGridSpec(
            num_scalar_prefetch=2,
            grid=(n_seq, N_HEADS),
            in_specs=[pl.BlockSpec((1, 1, 1, KV_WIDTH), lambda b, h, pt, idx: (b, h, 0, 0)),
                      new_spec, new_spec]
                     + [page_spec(j) for j in range(n_sel)] * 2,
            out_specs=pl.BlockSpec((1, 1, 1, HEAD_DIM), lambda b, h, pt, idx: (b, h, 0, 0))),
        out_shape=jax.ShapeDtypeStruct((n_seq, N_HEADS, 1, HEAD_DIM), F32),
        compiler_params=_cparams(2),
        name="moba_sample",
    )(page_table, idx_flat, q_exp.reshape(n_seq, N_HEADS, 1, KV_WIDTH),
      k_new.reshape(n_seq, 1, KV_WIDTH), v_new.reshape(n_seq, 1, KV_WIDTH),
      *([cache_k2] * n_sel), *([cache_v2] * n_sel))


def _out_b_body(a_ref, *refs, prec, gated):
    if gated:
        x1_ref, w_ref, g_ref, y_ref = refs
        a = a_ref[...]
    else:
        gate_ref, x1_ref, w_ref, g_ref, y_ref = refs
        gate = gate_ref[...]
        a = (a_ref[...] * (gate * _sigmoid(gate))).astype(w_ref.dtype)
    o = _dot(a, w_ref[...], prec)
    y_ref[...] = x1_ref[...] + _rms_scale(o) * g_ref[...]


def _out_b(att, gate, x1, w, g_post, tm, prec):
    m, d = x1.shape
    tile = pl.BlockSpec((tm, d), lambda i: (i, 0))
    gated = gate is None
    acts = [att, x1] if gated else [att, gate, x1]
    return pl.pallas_call(
        functools.partial(_out_b_body, prec=prec, gated=gated),
        grid=(m // tm,),
        in_specs=[tile] * len(acts) + [_full((d, d)), _full((1, d))],
        out_specs=tile,
        out_shape=jax.ShapeDtypeStruct((m, d), F32),
        compiler_params=_cparams(1),
        name="out_b",
    )(*acts, w, g_post.reshape(1, d))


def _prompt_trunk(x, p):
    bsz, seqlen, d = x.shape
    m = bsz * seqlen
    nb = seqlen // MOBA_BLOCK
    nc = seqlen // S5_CHUNK
    g, pch = SSM_GROUPS, SSM_GROUP
    x2 = x.reshape(m, d)
    u, gate_a = _norm_matmul(x2, p['g_pre_a'], p['w_in_a'].astype(BF16), (d, d), (BF16, F32), 512, None)
    ar, ai, bb_re, bb_im = _s5_discretize(p['lambda_re'], p['lambda_im'], p['log_dt'], p['b_re'], p['b_im'])
    ops = _s5_prompt_operators(ar, ai, bb_re, bb_im, p['c_re'], p['c_im'], p['d_skip'])
    u2 = u.reshape(bsz, nc, S5_CHUNK, g, pch).transpose(3, 1, 0, 2, 4).reshape(g, nc * bsz, S5_CHUNK * pch)
    y2, h_re, h_im = _s5_prompt(u2, ops, bsz)
    y = y2.reshape(g, nc, bsz, S5_CHUNK, pch).transpose(2, 1, 3, 0, 4).reshape(m, d)
    ssm_re = h_re[:, :, :SSM_STATE].transpose(1, 0, 2)[None]
    ssm_im = h_im[:, :, :SSM_STATE].transpose(1, 0, 2)[None]
    x1 = _post_a(y, gate_a, x2, p['w_glu'].astype(BF16), p['b_glu'], p['w_out_a'].astype(BF16),
                 p['g_post_a'], 256, None)
    k, v, gate_b, qt, ka, vta, km = _pre_b(x1, p['g_kv'], p['w_kv'].astype(BF16), p['g_pre_b'],
                                           p['w_in_b'].astype(BF16), MOBA_BLOCK, None, (bsz, nb))
    km = km.reshape(bsz, nb, KV_HEADS, HEAD_DIM).transpose(0, 2, 1, 3)
    att = _moba_prompt(qt, gate_b.reshape(bsz, seqlen, d), ka, vta, km)
    y_out = _out_b(att.reshape(m, d), None, x1, p['w_out_b'].astype(BF16), p['g_post_b'], 512, None)
    return (y_out.reshape(bsz, seqlen, d), ssm_re, ssm_im,
            k.reshape(bsz, seqlen, KV_HEADS, HEAD_DIM), v.reshape(bsz, seqlen, KV_HEADS, HEAD_DIM))


def _sample_trunk(x, h0_re, h0_im, cache_k, cache_v, page_table, p):
    n_seq, seqlen, d = x.shape
    assert seqlen == 1
    g, pch = SSM_GROUPS, SSM_GROUP
    n_past = page_table.shape[1] // PAGES_PER_BLOCK
    x2 = x.reshape(n_seq, d)
    u, gate_a = _norm_matmul(x2, p['g_pre_a'], p['w_in_a'], (d, d), (F32, F32), n_seq, HIGHEST)
    ar, ai, bb_re, bb_im = _s5_discretize(p['lambda_re'], p['lambda_im'], p['log_dt'], p['b_re'], p['b_im'])
    y_g, hr, hi = _s5_step(u.reshape(n_seq, g, pch).transpose(1, 0, 2),
                           h0_re.transpose(1, 0, 2), h0_im.transpose(1, 0, 2),
                           ar, ai, bb_re, bb_im, p['c_re'], p['c_im'], p['d_skip'])
    y = y_g.transpose(1, 0, 2).reshape(n_seq, d)
    x1 = _post_a(y, gate_a, x2, p['w_glu'], p['b_glu'], p['w_out_a'], p['g_post_a'], n_seq, HIGHEST)
    k, v, gate_b, q = _pre_b(x1, p['g_kv'], p['w_kv'], p['g_pre_b'], p['w_in_b'], n_seq, HIGHEST)
    cache_k2 = cache_k.reshape(cache_k.shape[0], PAGE_SIZE, KV_WIDTH)
    cache_v2 = cache_v.reshape(cache_v.shape[0], PAGE_SIZE, KV_WIDTH)
    km = _block_means(cache_k2, page_table, n_past)
    eye = jnp.eye(KV_HEADS, dtype=F32)
    q_exp = (q.reshape(n_seq, KV_HEADS, Q_PER_KV, 1, HEAD_DIM) * eye[None, :, None, :, None])
    q_exp = q_exp.reshape(n_seq, N_HEADS, KV_WIDTH)
    idx = _sample_topk(q_exp, km)[:, :, :MOBA_TOPK].reshape(-1)
    att = _moba_sample(q_exp, k, v, cache_k2, cache_v2, page_table, idx, n_past)
    y_out = _out_b(att.reshape(n_seq, d), gate_b, x1, p['w_out_b'], p['g_post_b'], n_seq, HIGHEST)
    return (y_out.reshape(n_seq, 1, d), hr.transpose(1, 0, 2)[None], hi.transpose(1, 0, 2)[None],
            k.reshape(n_seq, 1, KV_HEADS, HEAD_DIM), v.reshape(n_seq, 1, KV_HEADS, HEAD_DIM))


def kernel(x_prompt, x_sample, state_ssm_re, state_ssm_im, cache_k, cache_v, page_table, g_pre_a, w_in_a, lambda_re, lambda_im, log_dt, b_re, b_im, c_re, c_im, d_skip, w_glu, b_glu, w_out_a, g_post_a, g_kv, w_kv, g_pre_b, w_in_b, w_out_b, g_post_b):
    p = {'g_pre_a': g_pre_a[0], 'w_in_a': w_in_a[0], 'lambda_re': lambda_re[0], 'lambda_im': lambda_im[0],
         'log_dt': log_dt[0], 'b_re': b_re[0], 'b_im': b_im[0], 'c_re': c_re[0], 'c_im': c_im[0],
         'd_skip': d_skip[0], 'w_glu': w_glu[0], 'b_glu': b_glu[0], 'w_out_a': w_out_a[0],
         'g_post_a': g_post_a[0], 'g_kv': g_kv, 'w_kv': w_kv, 'g_pre_b': g_pre_b[0],
         'w_in_b': w_in_b[0], 'w_out_b': w_out_b[0], 'g_post_b': g_post_b[0]}
    y_p, re_p, im_p, k_p, v_p = _prompt_trunk(x_prompt, p)
    y_s, re_s, im_s, k_s, v_s = _sample_trunk(x_sample, state_ssm_re[0], state_ssm_im[0],
                                              cache_k, cache_v, page_table, p)
    return (y_p, y_s, re_p, im_p, k_p, v_p, re_s, im_s, k_s, v_s)
```

```python
import functools
import math

import jax
import jax.numpy as jnp
from jax import lax
from jax.experimental import pallas as pl
from jax.experimental.pallas import tpu as pltpu

F32 = jnp.float32
BF16 = jnp.bfloat16
HIGHEST = lax.Precision.HIGHEST

D_MODEL = 1024
SSM_GROUP = 16
SSM_GROUPS = D_MODEL // SSM_GROUP
SSM_STATE = 64
HEAD_DIM = 64
N_HEADS = D_MODEL // HEAD_DIM
KV_HEADS = 4
Q_PER_KV = N_HEADS // KV_HEADS
KV_WIDTH = KV_HEADS * HEAD_DIM
MOBA_BLOCK = 256
MOBA_TOPK = 3
PAGE_SIZE = 128
PAGES_PER_BLOCK = MOBA_BLOCK // PAGE_SIZE
EPS = 1e-6
NEG_INF = -1e30
REMOVED = -3e38
SCALE = HEAD_DIM ** -0.5
S5_CHUNK = 16
LANES = 128
VMEM_LIMIT = 56 * 1024 * 1024


def _cparams(n_grid):
    return pltpu.CompilerParams(dimension_semantics=("arbitrary",) * n_grid,
                                vmem_limit_bytes=VMEM_LIMIT)


def _dot(a, b, prec=None):
    return jnp.dot(a, b, preferred_element_type=F32, precision=prec)


def _dot_nt(a, b, prec=None):
    return lax.dot_general(a, b, (((1,), (1,)), ((), ())),
                           preferred_element_type=F32, precision=prec)


def _rms_scale(x):
    return x * lax.rsqrt(jnp.mean(x * x, axis=-1, keepdims=True) + EPS)


def _sigmoid(x):
    return 1.0 / (1.0 + jnp.exp(-x))


def _full(shape):
    zeros = (0,) * len(shape)
    return pl.BlockSpec(shape, lambda *_: zeros)


def _tile(ref):
    x = ref[...]
    return x.reshape(x.shape[-2:])


def _norm_matmul_body(x_ref, g_ref, w_ref, *out_refs, prec):
    h = (_rms_scale(_tile(x_ref)) * g_ref[...]).astype(w_ref.dtype)
    off = 0
    for o_ref in out_refs:
        n = o_ref.shape[-1]
        o_ref[...] = _dot(h, w_ref[:, off:off + n], prec).astype(o_ref.dtype).reshape(o_ref.shape)
        off += n


def _norm_matmul(x, g, w, prec, grid, x_spec, out_specs, out_shape):
    d, n = w.shape
    return pl.pallas_call(
        functools.partial(_norm_matmul_body, prec=prec),
        grid=grid,
        in_specs=[x_spec, _full((1, d)), _full((d, n))],
        out_specs=out_specs,
        out_shape=out_shape,
        compiler_params=_cparams(len(grid)),
        name="norm_matmul",
    )(x, g.reshape(1, d), w)


S5_LANE_GROUPS = LANES // SSM_GROUP
S5_STATE_LANES = S5_LANE_GROUPS * SSM_STATE
S5_PAIR = 2 * LANES


def _s5_prompt_body(u_ref, m_ref, s_ref, r_ref, a_ref, d_ref, y_ref, hfin_ref, e_s, hin_s, h_s):
    t_chunk, bsz, rc, _ = u_ref.shape
    rows = bsz * rc
    sl = S5_STATE_LANES

    @pl.when(pl.program_id(1) == 0)
    def _():
        h_s[...] = jnp.zeros(h_s.shape, F32)

    xs = [u_ref[s].reshape(rows, LANES) for s in range(t_chunk)]
    xcat = jnp.concatenate(xs, axis=1)
    half = sl // LANES
    cols = lambda j: slice(j * LANES, (j + 1) * LANES)
    e = _dot(xcat, s_ref[0])
    for j in range(2 * half):
        e_s[j] = e[:, cols(j)]
    ar = [jnp.broadcast_to(a_ref[0, 0:1, cols(j)], (bsz, LANES)) for j in range(half)]
    ai = [jnp.broadcast_to(a_ref[0, 1:2, cols(j)], (bsz, LANES)) for j in range(half)]

    def step(c, carry):
        hr, hi = carry
        at_c = pl.ds(c, bsz, stride=rc)
        new_r, new_i = [], []
        for j in range(half):
            hin_s[j, at_c, :] = hr[j]
            hin_s[half + j, at_c, :] = hi[j]
            new_r.append(ar[j] * hr[j] - ai[j] * hi[j] + e_s[j, at_c, :])
            new_i.append(ar[j] * hi[j] + ai[j] * hr[j] + e_s[half + j, at_c, :])
        return tuple(new_r), tuple(new_i)

    carry0 = (tuple(h_s[0, j] for j in range(half)), tuple(h_s[1, j] for j in range(half)))
    hr, hi = lax.fori_loop(0, rc, step, carry0)
    for j in range(half):
        h_s[0, j] = hr[j]
        h_s[1, j] = hi[j]
        hfin_ref[0, :, cols(j)] = hr[j]
        hfin_ref[0, :, cols(half + j)] = hi[j]
    hin = jnp.concatenate([hin_s[j] for j in range(2 * half)], axis=1).astype(BF16)
    d_row = d_ref[0]
    for i in range(t_chunk // 2):
        acc = _dot(hin, r_ref[0, :, i * S5_PAIR:(i + 1) * S5_PAIR])
        for j in range(i + 1):
            acc += _dot(xcat[:, j * S5_PAIR:(j + 1) * S5_PAIR], m_ref[0, i - j])
        for tt in range(2):
            t = 2 * i + tt
            y_t = acc[:, tt * LANES:(tt + 1) * LANES] + d_row * xs[t].astype(F32)
            y_ref[t] = y_t.reshape(bsz, rc, LANES)


def _s5_prompt(u5, ops, rc):
    t_chunk, bsz, nc, d = u5.shape
    n_lb = d // LANES
    m, s, r, a16, dl = ops
    sl = S5_STATE_LANES
    act = pl.BlockSpec((t_chunk, bsz, rc, LANES), lambda lb, ct: (0, 0, ct, lb))
    per_lb = lambda shape: pl.BlockSpec((1,) + shape, lambda lb, ct: (lb,) + (0,) * len(shape))
    return pl.pallas_call(
        _s5_prompt_body,
        grid=(n_lb, nc // rc),
        in_specs=[act, per_lb(m.shape[1:]), per_lb(s.shape[1:]), per_lb(r.shape[1:]),
                  per_lb((2, sl)), per_lb((1, LANES))],
        out_specs=[act, per_lb((bsz, 2 * sl))],
        out_shape=[jax.ShapeDtypeStruct(u5.shape, F32),
                   jax.ShapeDtypeStruct((n_lb, bsz, 2 * sl), F32)],
        scratch_shapes=[pltpu.VMEM((2 * sl // LANES, bsz * rc, LANES), F32),
                        pltpu.VMEM((2 * sl // LANES, bsz * rc, LANES), F32),
                        pltpu.VMEM((2, sl // LANES, bsz, LANES), F32)],
        compiler_params=_cparams(2),
        name="s5_prompt",
    )(u5, m, s, r, a16, dl)


def _s5_discretize(lam_re, lam_im, log_dt, b_re, b_im):
    dt = jnp.exp(log_dt)[:, None]
    mag = jnp.exp(lam_re * dt)
    ar = mag * jnp.cos(lam_im * dt)
    ai = mag * jnp.sin(lam_im * dt)
    den = lam_re * lam_re + lam_im * lam_im
    nr = ar - 1.0
    coef_re = (nr * lam_re + ai * lam_im) / den
    coef_im = (ai * lam_re - nr * lam_im) / den
    bb_re = coef_re[..., None] * b_re - coef_im[..., None] * b_im
    bb_im = coef_re[..., None] * b_im + coef_im[..., None] * b_re
    return ar, ai, bb_re, bb_im


def _s5_prompt_operators(ar, ai, bb_re, bb_im, c_re, c_im, d_skip):
    t = S5_CHUNK
    g, n = ar.shape
    p = SSM_GROUP
    lg = S5_LANE_GROUPS
    n_lb = g // lg
    pr, pi = [jnp.ones_like(ar)], [jnp.zeros_like(ar)]
    for _ in range(t):
        pr, pi = pr + [pr[-1] * ar - pi[-1] * ai], pi + [pr[-1] * ai + pi[-1] * ar]
    pr = jnp.stack(pr)
    pi = jnp.stack(pi)
    cp_re = c_re[None] * pr[:, :, None, :] - c_im[None] * pi[:, :, None, :]
    cp_im = c_re[None] * pi[:, :, None, :] + c_im[None] * pr[:, :, None, :]
    k = (jnp.einsum('tgpn,gnq->gtpq', cp_re[:t], bb_re, precision=HIGHEST)
         - jnp.einsum('tgpn,gnq->gtpq', cp_im[:t], bb_im, precision=HIGHEST))
    eye = jnp.eye(lg, dtype=F32)
    k_lag = lambda lag: k[:, lag] if lag >= 0 else jnp.zeros_like(k[:, 0])
    kt = jnp.stack([jnp.stack([jnp.stack([k_lag(2 * dl + tt - ss) for tt in range(2)], axis=1)
                               for ss in range(2)], axis=1) for dl in range(t // 2)], axis=1)
    kt = kt.reshape(n_lb, lg, t // 2, 2, 2, p, p)
    m = kt.transpose(0, 2, 3, 1, 6, 4, 5)[..., None, :] * eye[:, None, None, :, None]
    m = m.reshape(n_lb, t // 2, 2 * lg * p, 2 * lg * p)
    pr_rev = jnp.stack([pr[t - 1 - s] for s in range(t)])
    pi_rev = jnp.stack([pi[t - 1 - s] for s in range(t)])
    s_re = pr_rev[:, :, :, None] * bb_re[None] - pi_rev[:, :, :, None] * bb_im[None]
    s_im = pr_rev[:, :, :, None] * bb_im[None] + pi_rev[:, :, :, None] * bb_re[None]
    s_ri = jnp.stack([s_re, s_im]).reshape(2, t, n_lb, lg, n, p)
    s = s_ri.transpose(2, 1, 3, 5, 0, 4)[..., None, :] * eye[:, None, None, :, None]
    s = s.reshape(n_lb, t * lg * p, 2 * lg * n)
    r_ri = jnp.stack([cp_re[1:], -cp_im[1:]]).reshape(2, t, n_lb, lg, p, n)
    r = r_ri.transpose(2, 0, 3, 5, 1, 4)[..., None, :] * eye[:, None, None, :, None]
    r = r.reshape(n_lb, 2 * lg * n, t * lg * p)
    a16 = jnp.stack([pr[t].reshape(n_lb, lg * n), pi[t].reshape(n_lb, lg * n)], axis=1)
    return (m.astype(BF16), s.astype(BF16), r.astype(BF16), a16, d_skip.reshape(n_lb, 1, lg * p))


def _s5_step_body(u_ref, h0r_ref, h0i_ref, ar_ref, ai_ref, bbr_ref, bbi_ref, cr_ref, ci_ref,
                  d_ref, y_ref, hr_ref, hi_ref):
    u = u_ref[0]
    ar, ai = ar_ref[0], ai_ref[0]
    h0r, h0i = h0r_ref[0], h0i_ref[0]
    hr = _dot_nt(u, bbr_ref[0], HIGHEST) + ar * h0r - ai * h0i
    hi = _dot_nt(u, bbi_ref[0], HIGHEST) + ar * h0i + ai * h0r
    hr_ref[0] = hr
    hi_ref[0] = hi
    y_ref[0] = (_dot_nt(hr, cr_ref[0], HIGHEST) - _dot_nt(hi, ci_ref[0], HIGHEST)
                + d_ref[0] * u)


def _s5_step(u_g, h0r, h0i, ar, ai, bb_re, bb_im, c_re, c_im, d_skip):
    g, b, p = u_g.shape
    n = ar.shape[-1]
    per_group = lambda shape: pl.BlockSpec((1,) + shape, lambda i: (i, 0, 0))
    return pl.pallas_call(
        _s5_step_body,
        grid=(g,),
        in_specs=[per_group((b, p)), per_group((b, n)), per_group((b, n)),
                  per_group((1, n)), per_group((1, n)),
                  per_group((n, p)), per_group((n, p)),
                  per_group((p, n)), per_group((p, n)), per_group((1, p))],
        out_specs=[per_group((b, p)), per_group((b, n)), per_group((b, n))],
        out_shape=[jax.ShapeDtypeStruct((g, b, p), F32),
                   jax.ShapeDtypeStruct((g, b, n), F32),
                   jax.ShapeDtypeStruct((g, b, n), F32)],
        compiler_params=_cparams(1),
        name="s5_step",
    )(u_g, h0r, h0i, ar.reshape(g, 1, n), ai.reshape(g, 1, n), bb_re, bb_im, c_re, c_im,
      d_skip.reshape(g, 1, p))


def _post_a_body(y_ref, ga_ref, x_ref, wglu_ref, bglu_ref, wout_ref, gpost_ref, x1_ref, *, prec):
    y = _tile(y_ref)
    wdt = wglu_ref.dtype
    gy = 0.5 * y * (1.0 + lax.erf(y * math.sqrt(0.5)))
    z = _dot(gy.astype(wdt), wglu_ref[...], prec) + bglu_ref[...]
    gate = _tile(ga_ref)
    t = (gy * _sigmoid(z)) * (gate * _sigmoid(gate))
    o = _dot(t.astype(wdt), wout_ref[...], prec)
    x1_ref[...] = (_tile(x_ref) + _rms_scale(o) * gpost_ref[...]).reshape(x1_ref.shape)


def _post_a(y, gate, x, w_glu, b_glu, w_out, g_post, prec, grid, y_spec, x_spec):
    d = w_glu.shape[0]
    return pl.pallas_call(
        functools.partial(_post_a_body, prec=prec),
        grid=grid,
        in_specs=[y_spec, x_spec, x_spec, _full((d, d)), _full((1, d)), _full((d, d)), _full((1, d))],
        out_specs=x_spec,
        out_shape=jax.ShapeDtypeStruct(x.shape, F32),
        compiler_params=_cparams(len(grid)),
        name="post_a",
    )(y, gate, x, w_glu, b_glu.reshape(1, d), w_out, g_post.reshape(1, d))


ALIBI_ROWS = 16
FEAT_OFF = HEAD_DIM
V_ROWS = HEAD_DIM + 16


def _aug_width(nb):
    nbp = -(-nb // 16) * 16
    return -(-(FEAT_OFF + ALIBI_ROWS + nbp) // LANES) * LANES, nbp


def _pre_b_body(x1_ref, gkv_ref, wkv_ref, gb_ref, winb_ref, k_ref, v_ref, gate_ref, *rest, prec, nb):
    xn = _rms_scale(x1_ref[...])
    wdt = wkv_ref.dtype
    hk = (xn * gkv_ref[...]).astype(wdt)
    k = _dot(hk, wkv_ref[:, :KV_WIDTH], prec)
    v = _dot(hk, wkv_ref[:, KV_WIDTH:], prec)
    k_ref[...] = k
    v_ref[...] = v
    hq = (xn * gb_ref[...]).astype(wdt)
    q = _dot(hq, winb_ref[:, :D_MODEL], prec) * SCALE
    gate_ref[...] = _dot(hq, winb_ref[:, D_MODEL:], prec)
    if nb is None:
        q_ref, = rest
        q_ref[...] = q
        return
    qt_ref, ka_ref, vta_ref, km_ref = rest
    tm = q.shape[0]
    qt_ref[0, 0] = q.T.astype(BF16)
    km_ref[0] = jnp.mean(k, axis=0, keepdims=True)
    n = pl.program_id(0) % nb
    width = ka_ref.shape[-1] - FEAT_OFF
    lane = lax.broadcasted_iota(jnp.int32, (tm, width), 1)
    rowi = lax.broadcasted_iota(jnp.int32, (tm, width), 0)
    feat = jnp.where(lane < 3, n, jnp.where(lane < 6, rowi, jnp.where(lane < 9, 1, 0)))
    feat = jnp.where(lane - ALIBI_ROWS == n, 1, feat).astype(F32).astype(BF16)
    kb = k.astype(BF16)
    vt = v.T
    ones_row = (lax.broadcasted_iota(jnp.int32, (V_ROWS - HEAD_DIM, tm), 0) == 0).astype(F32).astype(BF16)
    for h in range(KV_HEADS):
        ka_ref[0, h, :, :FEAT_OFF] = kb[:, h * HEAD_DIM:(h + 1) * HEAD_DIM]
        ka_ref[0, h, :, FEAT_OFF:] = feat
        vta_ref[0, 0, h * V_ROWS:h * V_ROWS + HEAD_DIM, :] = vt[h * HEAD_DIM:(h + 1) * HEAD_DIM].astype(BF16)
        vta_ref[0, 0, h * V_ROWS + HEAD_DIM:(h + 1) * V_ROWS, :] = ones_row


def _pre_b(x1, g_kv, w_kv, g_pre_b, w_in_b, tm, prec, prompt_blocks=None):
    m, d = x1.shape
    row = lambda wd: pl.BlockSpec((tm, wd), lambda i: (i, 0))
    out_specs = [row(KV_WIDTH), row(KV_WIDTH), row(d)]
    out_shape = [jax.ShapeDtypeStruct((m, KV_WIDTH), F32), jax.ShapeDtypeStruct((m, KV_WIDTH), F32),
                 jax.ShapeDtypeStruct((m, d), F32)]
    nb = None
    if prompt_blocks is None:
        out_specs += [row(d)]
        out_shape += [jax.ShapeDtypeStruct((m, d), F32)]
    else:
        bsz, nb = prompt_blocks
        assert tm == MOBA_BLOCK and m == bsz * nb * tm
        aug, _ = _aug_width(nb)
        out_specs += [pl.BlockSpec((1, 1, d, tm), lambda i: (i // nb, i % nb, 0, 0)),
                      pl.BlockSpec((1, KV_HEADS, tm, aug), lambda i: (i // nb, 0, i % nb, 0)),
                      pl.BlockSpec((1, 1, KV_HEADS * V_ROWS, tm), lambda i: (i // nb, i % nb, 0, 0)),
                      pl.BlockSpec((1, 1, KV_WIDTH), lambda i: (i, 0, 0))]
        out_shape += [jax.ShapeDtypeStruct((bsz, nb, d, tm), BF16),
                      jax.ShapeDtypeStruct((bsz, KV_HEADS, nb * tm, aug), BF16),
                      jax.ShapeDtypeStruct((bsz, nb, KV_HEADS * V_ROWS, tm), BF16),
                      jax.ShapeDtypeStruct((bsz * nb, 1, KV_WIDTH), F32)]
    return pl.pallas_call(
        functools.partial(_pre_b_body, prec=prec, nb=nb),
        grid=(m // tm,),
        in_specs=[row(d), _full((1, d)), _full((d, 2 * KV_WIDTH)), _full((1, d)), _full((d, 2 * d))],
        out_specs=out_specs,
        out_shape=out_shape,
        compiler_params=_cparams(1),
        name="pre_b",
    )(x1, g_kv.reshape(1, d), w_kv, g_pre_b.reshape(1, d), w_in_b)


def _top3_mask(gs, idx, valid, axis):
    n = gs.shape[axis]
    gs = jnp.where(valid, gs, NEG_INF)
    sel = jnp.zeros(gs.shape, F32)
    picks = []
    for _ in range(MOBA_TOPK):
        mx = jnp.max(gs, axis=axis, keepdims=True)
        first = jnp.min(jnp.where(gs == mx, idx, n), axis=axis, keepdims=True)
        pick = idx == first
        sel = jnp.where(pick & valid, 1.0, sel)
        gs = jnp.where(pick, REMOVED, gs)
        picks.append(first)
    return sel, picks


def _split3(x):
    hi = x.astype(BF16).astype(F32)
    mid = (x - hi).astype(BF16).astype(F32)
    lo = (x - hi - mid).astype(BF16).astype(F32)
    return [hi, mid, lo]


def _moba_prompt_body(qt_ref, g_ref, ka_ref, vta_ref, km_ref, o_ref, qa_s, qd_s):
    h = pl.program_id(1)
    c = pl.program_id(2)
    blk = MOBA_BLOCK
    nb = km_ref.shape[2]
    aug, nbp = _aug_width(nb)
    km = km_ref[0, 0]
    row = lax.broadcasted_iota(jnp.int32, (nb, blk), 0)
    frow = lax.broadcasted_iota(jnp.int32, (ALIBI_ROWS, blk), 0)
    pos_q = (c * blk + lax.broadcasted_iota(jnp.int32, (1, blk), 1)).astype(F32)
    for g in range(Q_PER_KV):
        lanes = slice(g * blk, (g + 1) * blk)
        qg = qt_ref[0, 0, g * HEAD_DIM:(g + 1) * HEAD_DIM, :]
        sel, _ = _top3_mask(_dot(km, qg.astype(F32), HIGHEST), row, row < c, 0)
        bias = jnp.where(sel > 0.0, 0.0, NEG_INF)
        if nbp > nb:
            bias = jnp.concatenate([bias, jnp.zeros((nbp - nb, blk), F32)], axis=0)
        head1 = (h * Q_PER_KV + g + 1).astype(F32)
        slope = jnp.exp2(-8.0 * jnp.full((1, blk), head1, F32) / N_HEADS)
        terms = _split3(slope * float(blk)) + _split3(slope) + _split3(-slope * pos_q)
        feat = jnp.zeros((ALIBI_ROWS, blk), F32)
        for r, term in enumerate(terms):
            feat = jnp.where(frow == r, term, feat)
        for ref, b_rows in ((qa_s, bias.astype(BF16)), (qd_s, jnp.zeros((nbp, blk), BF16))):
            ref[:FEAT_OFF, lanes] = qg
            ref[FEAT_OFF:FEAT_OFF + ALIBI_ROWS, lanes] = feat.astype(BF16)
            ref[FEAT_OFF + ALIBI_ROWS:FEAT_OFF + ALIBI_ROWS + nbp, lanes] = b_rows
            if FEAT_OFF + ALIBI_ROWS + nbp < aug:
                ref[FEAT_OFF + ALIBI_ROWS + nbp:, lanes] = jnp.zeros((aug - FEAT_OFF - ALIBI_ROWS - nbp, blk), BF16)

    def keys(n):
        return ka_ref[0, 0, pl.ds(pl.multiple_of(n * blk, blk), blk), :]

    s = _dot(keys(c), qd_s[...])
    key = lax.broadcasted_iota(jnp.int32, s.shape, 0)
    qry = lax.broadcasted_iota(jnp.int32, s.shape, 1) & (blk - 1)
    s = jnp.where(key <= qry, s, NEG_INF)
    m = jnp.max(s, axis=0, keepdims=True)
    acc = _dot(vta_ref[0, c], jnp.exp(s - m).astype(BF16))

    def block_pair(i, carry):
        m, acc = carry
        n0, n1 = 2 * i, 2 * i + 1
        s0 = _dot(keys(n0), qa_s[...])
        s1 = _dot(keys(n1), qa_s[...])
        m2 = jnp.maximum(m, jnp.maximum(jnp.max(s0, axis=0, keepdims=True), jnp.max(s1, axis=0, keepdims=True)))
        p0 = jnp.exp(s0 - m2).astype(BF16)
        p1 = jnp.exp(s1 - m2).astype(BF16)
        return m2, jnp.exp(m - m2) * acc + _dot(vta_ref[0, n0], p0) + _dot(vta_ref[0, n1], p1)

    m, acc = lax.fori_loop(0, (c + 1) // 2, block_pair, (m, acc))
    out_t = acc[:HEAD_DIM] / acc[HEAD_DIM:HEAD_DIM + 1]
    att = jnp.concatenate([out_t[:, g * blk:(g + 1) * blk] for g in range(Q_PER_KV)], axis=0).T
    gate = g_ref[0]
    o_ref[0] = (att * (gate * _sigmoid(gate))).astype(o_ref.dtype)


def _moba_prompt(qt, gate, ka, vta, km):
    bsz, seqlen, d = gate.shape
    nb = seqlen // MOBA_BLOCK
    width = Q_PER_KV * HEAD_DIM
    aug, _ = _aug_width(nb)
    tile = pl.BlockSpec((1, MOBA_BLOCK, width), lambda b, h, c: (b, c, h))
    return pl.pallas_call(
        _moba_prompt_body,
        grid=(bsz, KV_HEADS, nb),
        in_specs=[pl.BlockSpec((1, 1, width, MOBA_BLOCK), lambda b, h, c: (b, c, h, 0)),
                  tile,
                  pl.BlockSpec((1, 1, seqlen, aug), lambda b, h, c: (b, h, 0, 0)),
                  pl.BlockSpec((1, nb, V_ROWS, MOBA_BLOCK), lambda b, h, c: (b, 0, h, 0)),
                  pl.BlockSpec((1, 1, nb, HEAD_DIM), lambda b, h, c: (b, h, 0, 0))],
        out_specs=tile,
        out_shape=jax.ShapeDtypeStruct((bsz, seqlen, d), BF16),
        scratch_shapes=[pltpu.VMEM((aug, Q_PER_KV * MOBA_BLOCK), BF16) for _ in range(2)],
        compiler_params=_cparams(3),
        name="moba_prompt",
    )(qt, gate, ka, vta, km)


_KM_BLOCKS = 8


def _block_mean_body(pt_ref, *refs):
    del pt_ref
    pages, o_ref = refs[:-1], refs[-1]
    rows = []
    for j in range(_KM_BLOCKS):
        tot = sum(jnp.sum(pages[PAGES_PER_BLOCK * j + r][0], axis=0, keepdims=True)
                  for r in range(PAGES_PER_BLOCK))
        rows.append(tot * (1.0 / MOBA_BLOCK))
    o_ref[0] = jnp.concatenate(rows, axis=0)


def _block_means(cache2, page_table, n_blocks):
    n_seq = page_table.shape[0]
    per_step = _KM_BLOCKS * PAGES_PER_BLOCK
    page_spec = lambda j: pl.BlockSpec(
        (1, PAGE_SIZE, KV_WIDTH), lambda b, s, pt, j=j: (pt[b, s * per_step + j], 0, 0))
    return pl.pallas_call(
        _block_mean_body,
        grid_spec=pltpu.PrefetchScalarGridSpec(
            num_scalar_prefetch=1,
            grid=(n_seq, n_blocks // _KM_BLOCKS),
            in_specs=[page_spec(j) for j in range(per_step)],
            out_specs=pl.BlockSpec((1, _KM_BLOCKS, KV_WIDTH), lambda b, s, pt: (b, s, 0))),
        out_shape=jax.ShapeDtypeStruct((n_seq, n_blocks, KV_WIDTH), F32),
        compiler_params=_cparams(2),
        name="block_means",
    )(page_table, *([cache2] * per_step))


def _sample_topk_body(q_ref, km_ref, idx_ref, *, n_past):
    gs = _dot_nt(q_ref[0], km_ref[0], HIGHEST)
    lane = lax.broadcasted_iota(jnp.int32, gs.shape, 1)
    _, picks = _top3_mask(gs, lane, lane < n_past, 1)
    out_lane = lax.broadcasted_iota(jnp.int32, idx_ref.shape[1:], 1)
    out = jnp.zeros(idx_ref.shape[1:], jnp.int32)
    for r, first in enumerate(picks):
        out = jnp.where(out_lane == r, first, out)
    idx_ref[0] = out


def _sample_topk(q_exp, km):
    n_seq, n_heads, width = q_exp.shape
    n_past = km.shape[1]
    return pl.pallas_call(
        functools.partial(_sample_topk_body, n_past=n_past),
        grid=(n_seq,),
        in_specs=[pl.BlockSpec((1, n_heads, width), lambda b: (b, 0, 0)),
                  pl.BlockSpec((1, n_past, width), lambda b: (b, 0, 0))],
        out_specs=pl.BlockSpec((1, n_heads, LANES), lambda b: (b, 0, 0)),
        out_shape=jax.ShapeDtypeStruct((n_seq, n_heads, LANES), jnp.int32),
        compiler_params=_cparams(1),
        name="sample_topk",
    )(q_exp, km)


_Q_ROWS = 8


def _moba_sample_body(pt_ref, idx_ref, q_ref, kn_ref, vn_ref, *refs, n_past):
    del pt_ref
    n_sel = MOBA_TOPK * PAGES_PER_BLOCK
    k_pages, v_pages, o_ref = refs[:n_sel], refs[n_sel:2 * n_sel], refs[-1]
    b = pl.program_id(0)
    h = pl.program_id(1)
    keys = MOBA_TOPK * MOBA_BLOCK
    q = jnp.broadcast_to(q_ref[0, 0], (_Q_ROWS, KV_WIDTH))
    k_sel = jnp.concatenate([r[0] for r in k_pages], axis=0)
    v_sel = jnp.concatenate([r[0] for r in v_pages], axis=0)
    col = lax.broadcasted_iota(jnp.int32, (_Q_ROWS, keys), 1)
    slot = col // MOBA_BLOCK
    base = (b * N_HEADS + h) * MOBA_TOPK
    idx = jnp.zeros((_Q_ROWS, keys), jnp.int32)
    for r in range(MOBA_TOPK):
        idx = jnp.where(slot == r, idx_ref[base + r], idx)
    pos = n_past * MOBA_BLOCK
    dist = (pos - (idx * MOBA_BLOCK + col % MOBA_BLOCK)).astype(F32)
    slope = jnp.exp2(-8.0 * jnp.full((_Q_ROWS, keys), (h + 1).astype(F32), F32) / N_HEADS)
    s = _dot_nt(q.astype(BF16), k_sel.astype(BF16)) - slope * dist
    s = jnp.where(idx < n_past, s, NEG_INF)
    s_own = jnp.sum(q * kn_ref[0], axis=-1, keepdims=True)
    m = jnp.maximum(jnp.max(s, axis=-1, keepdims=True), s_own)
    p = jnp.exp(s - m)
    p_own = jnp.exp(s_own - m)
    l = jnp.sum(p, axis=-1, keepdims=True) + p_own
    out = (_dot(p.astype(BF16), v_sel.astype(BF16)) + p_own * vn_ref[0]) / l
    lane = lax.broadcasted_iota(jnp.int32, out.shape, 1)
    out = jnp.where(lane // HEAD_DIM == h // Q_PER_KV, out, 0.0)
    folded = sum(out[:, j * HEAD_DIM:(j + 1) * HEAD_DIM] for j in range(KV_HEADS))
    o_ref[0, 0] = folded[0:1]


def _moba_sample(q_exp, k_new, v_new, cache_k2, cache_v2, page_table, idx_flat, n_past):
    n_seq = page_table.shape[0]

    def page_spec(j):
        r, pg = divmod(j, PAGES_PER_BLOCK)

        def index(b, h, pt, idx):
            blk = idx[(b * N_HEADS + h) * MOBA_TOPK + r]
            return (pt[b, blk * PAGES_PER_BLOCK + pg], 0, 0)

        return pl.BlockSpec((1, PAGE_SIZE, KV_WIDTH), index)

    n_sel = MOBA_TOPK * PAGES_PER_BLOCK
    new_spec = pl.BlockSpec((1, 1, KV_WIDTH), lambda b, h, pt, idx: (b, 0, 0))
    return pl.pallas_call(
        functools.partial(_moba_sample_body, n_past=n_past),
        grid_spec=pltpu.PrefetchScalarGridSpec(
            num_scalar_prefetch=2,
            grid=(n_seq, N_HEADS),
            in_specs=[pl.BlockSpec((1, 1, 1, KV_WIDTH), lambda b, h, pt, idx: (b, h, 0, 0)),
                      new_spec, new_spec]
                     + [page_spec(j) for j in range(n_sel)] * 2,
            out_specs=pl.BlockSpec((1, 1, 1, HEAD_DIM), lambda b, h, pt, idx: (b, h, 0, 0))),
        out_shape=jax.ShapeDtypeStruct((n_seq, N_HEADS, 1, HEAD_DIM), F32),
        compiler_params=_cparams(2),
        name="moba_sample",
    )(page_table, idx_flat, q_exp.reshape(n_seq, N_HEADS, 1, KV_WIDTH),
      k_new.reshape(n_seq, 1, KV_WIDTH), v_new.reshape(n_seq, 1, KV_WIDTH),
      *([cache_k2] * n_sel), *([cache_v2] * n_sel))


def _out_b_body(a_ref, *refs, prec, gated):
    if gated:
        x1_ref, w_ref, g_ref, y_ref = refs
        a = a_ref[...]
    else:
        gate_ref, x1_ref, w_ref, g_ref, y_ref = refs
        gate = gate_ref[...]
        a = (a_ref[...] * (gate * _sigmoid(gate))).astype(w_ref.dtype)
    o = _dot(a, w_ref[...], prec)
    y_ref[...] = x1_ref[...] + _rms_scale(o) * g_ref[...]


def _out_b(att, gate, x1, w, g_post, tm, prec):
    m, d = x1.shape
    tile = pl.BlockSpec((tm, d), lambda i: (i, 0))
    gated = gate is None
    acts = [att, x1] if gated else [att, gate, x1]
    return pl.pallas_call(
        functools.partial(_out_b_body, prec=prec, gated=gated),
        grid=(m // tm,),
        in_specs=[tile] * len(acts) + [_full((d, d)), _full((1, d))],
        out_specs=tile,
        out_shape=jax.ShapeDtypeStruct((m, d), F32),
        compiler_params=_cparams(1),
        name="out_b",
    )(*acts, w, g_post.reshape(1, d))


def _prompt_trunk(x, p):
    bsz, seqlen, d = x.shape
    m = bsz * seqlen
    nb = seqlen // MOBA_BLOCK
    nc = seqlen // S5_CHUNK
    rt = min(nc, 256)
    x3 = x.reshape(bsz, nc, S5_CHUNK * d)
    grid = (bsz, nc // rt, S5_CHUNK)
    nat_spec = pl.BlockSpec((1, rt, d), lambda b, r, t: (b, r, t))
    tmaj_spec = pl.BlockSpec((1, 1, rt, d), lambda b, r, t: (t, b, r, 0))
    u5, gate_a = _norm_matmul(
        x3, p['g_pre_a'], p['w_in_a'].astype(BF16), None, grid, nat_spec, [tmaj_spec, nat_spec],
        [jax.ShapeDtypeStruct((S5_CHUNK, bsz, nc, d), BF16), jax.ShapeDtypeStruct(x3.shape, F32)])
    ar, ai, bb_re, bb_im = _s5_discretize(p['lambda_re'], p['lambda_im'], p['log_dt'], p['b_re'], p['b_im'])
    ops = _s5_prompt_operators(ar, ai, bb_re, bb_im, p['c_re'], p['c_im'], p['d_skip'])
    y5, h_fin = _s5_prompt(u5, ops, min(nc, 64))
    h_fin = h_fin.reshape(d // LANES, bsz, 2, S5_LANE_GROUPS, SSM_STATE).transpose(2, 1, 0, 3, 4)
    h_fin = h_fin.reshape(2, 1, bsz, SSM_GROUPS, SSM_STATE)
    x1 = _post_a(y5, gate_a, x3, p['w_glu'].astype(BF16), p['b_glu'], p['w_out_a'].astype(BF16),
                 p['g_post_a'], None, grid, tmaj_spec, nat_spec).reshape(m, d)
    k, v, gate_b, qt, ka, vta, km = _pre_b(x1, p['g_kv'], p['w_kv'].astype(BF16), p['g_pre_b'],
                                           p['w_in_b'].astype(BF16), MOBA_BLOCK, None, (bsz, nb))
    km = km.reshape(bsz, nb, KV_HEADS, HEAD_DIM).transpose(0, 2, 1, 3)
    att = _moba_prompt(qt, gate_b.reshape(bsz, seqlen, d), ka, vta, km)
    y_out = _out_b(att.reshape(m, d), None, x1, p['w_out_b'].astype(BF16), p['g_post_b'], 512, None)
    return (y_out.reshape(bsz, seqlen, d), h_fin[0], h_fin[1],
            k.reshape(bsz, seqlen, KV_HEADS, HEAD_DIM), v.reshape(bsz, seqlen, KV_HEADS, HEAD_DIM))


def _sample_trunk(x, h0_re, h0_im, cache_k, cache_v, page_table, p):
    n_seq, seqlen, d = x.shape
    assert seqlen == 1
    g, pch = SSM_GROUPS, SSM_GROUP
    n_past = page_table.shape[1] // PAGES_PER_BLOCK
    x2 = x.reshape(n_seq, d)
    whole = pl.BlockSpec((n_seq, d), lambda i: (0, 0))
    u, gate_a = _norm_matmul(x2, p['g_pre_a'], p['w_in_a'], HIGHEST, (1,), whole, [whole, whole],
                             [jax.ShapeDtypeStruct((n_seq, d), F32)] * 2)
    ar, ai, bb_re, bb_im = _s5_discretize(p['lambda_re'], p['lambda_im'], p['log_dt'], p['b_re'], p['b_im'])
    y_g, hr, hi = _s5_step(u.reshape(n_seq, g, pch).transpose(1, 0, 2),
                           h0_re.transpose(1, 0, 2), h0_im.transpose(1, 0, 2),
                           ar, ai, bb_re, bb_im, p['c_re'], p['c_im'], p['d_skip'])
    y = y_g.transpose(1, 0, 2).reshape(n_seq, d)
    x1 = _post_a(y, gate_a, x2, p['w_glu'], p['b_glu'], p['w_out_a'], p['g_post_a'], HIGHEST, (1,), whole, whole)
    k, v, gate_b, q = _pre_b(x1, p['g_kv'], p['w_kv'], p['g_pre_b'], p['w_in_b'], n_seq, HIGHEST)
    cache_k2 = cache_k.reshape(cache_k.shape[0], PAGE_SIZE, KV_WIDTH)
    cache_v2 = cache_v.reshape(cache_v.shape[0], PAGE_SIZE, KV_WIDTH)
    km = _block_means(cache_k2, page_table, n_past)
    eye = jnp.eye(KV_HEADS, dtype=F32)
    q_exp = (q.reshape(n_seq, KV_HEADS, Q_PER_KV, 1, HEAD_DIM) * eye[None, :, None, :, None])
    q_exp = q_exp.reshape(n_seq, N_HEADS, KV_WIDTH)
    idx = _sample_topk(q_exp, km)[:, :, :MOBA_TOPK].reshape(-1)
    att = _moba_sample(q_exp, k, v, cache_k2, cache_v2, page_table, idx, n_past)
    y_out = _out_b(att.reshape(n_seq, d), gate_b, x1, p['w_out_b'], p['g_post_b'], n_seq, HIGHEST)
    return (y_out.reshape(n_seq, 1, d), hr.transpose(1, 0, 2)[None], hi.transpose(1, 0, 2)[None],
            k.reshape(n_seq, 1, KV_HEADS, HEAD_DIM), v.reshape(n_seq, 1, KV_HEADS, HEAD_DIM))


def kernel(x_prompt, x_sample, state_ssm_re, state_ssm_im, cache_k, cache_v, page_table, g_pre_a, w_in_a, lambda_re, lambda_im, log_dt, b_re, b_im, c_re, c_im, d_skip, w_glu, b_glu, w_out_a, g_post_a, g_kv, w_kv, g_pre_b, w_in_b, w_out_b, g_post_b):
    p = {'g_pre_a': g_pre_a[0], 'w_in_a': w_in_a[0], 'lambda_re': lambda_re[0], 'lambda_im': lambda_im[0],
         'log_dt': log_dt[0], 'b_re': b_re[0], 'b_im': b_im[0], 'c_re': c_re[0], 'c_im': c_im[0],
         'd_skip': d_skip[0], 'w_glu': w_glu[0], 'b_glu': b_glu[0], 'w_out_a': w_out_a[0],
         'g_post_a': g_post_a[0], 'g_kv': g_kv, 'w_kv': w_kv, 'g_pre_b': g_pre_b[0],
         'w_in_b': w_in_b[0], 'w_out_b': w_out_b[0], 'g_post_b': g_post_b[0]}
    y_p, re_p, im_p, k_p, v_p = _prompt_trunk(x_prompt, p)
    y_s, re_s, im_s, k_s, v_s = _sample_trunk(x_sample, state_ssm_re[0], state_ssm_im[0],
                                              cache_k, cache_v, page_table, p)
    return (y_p, y_s, re_p, im_p, k_p, v_p, re_s, im_s, k_s, v_s)
```

```python
import functools
import math

import jax
import jax.numpy as jnp
from jax import lax
from jax.experimental import pallas as pl
from jax.experimental.pallas import tpu as pltpu

F32 = jnp.float32
BF16 = jnp.bfloat16
HIGHEST = lax.Precision.HIGHEST

D_MODEL = 1024
SSM_GROUP = 16
SSM_GROUPS = D_MODEL // SSM_GROUP
SSM_STATE = 64
HEAD_DIM = 64
N_HEADS = D_MODEL // HEAD_DIM
KV_HEADS = 4
Q_PER_KV = N_HEADS // KV_HEADS
KV_WIDTH = KV_HEADS * HEAD_DIM
MOBA_BLOCK = 256
MOBA_TOPK = 3
PAGE_SIZE = 128
PAGES_PER_BLOCK = MOBA_BLOCK // PAGE_SIZE
EPS = 1e-6
NEG_INF = -1e30
REMOVED = -3e38
SCALE = HEAD_DIM ** -0.5
S5_CHUNK = 16
LANES = 128
VMEM_LIMIT = 56 * 1024 * 1024


def _cparams(n_grid):
    return pltpu.CompilerParams(dimension_semantics=("arbitrary",) * n_grid,
                                vmem_limit_bytes=VMEM_LIMIT)


def _dot(a, b, prec=None):
    return jnp.dot(a, b, preferred_element_type=F32, precision=prec)


def _dot_nt(a, b, prec=None):
    return lax.dot_general(a, b, (((1,), (1,)), ((), ())),
                           preferred_element_type=F32, precision=prec)


def _rms_scale(x):
    return x * lax.rsqrt(jnp.mean(x * x, axis=-1, keepdims=True) + EPS)


def _sigmoid(x):
    return 1.0 / (1.0 + jnp.exp(-x))


def _full(shape):
    zeros = (0,) * len(shape)
    return pl.BlockSpec(shape, lambda *_: zeros)


def _tile(ref):
    x = ref[...]
    return x.reshape(x.shape[-2:])


def _norm_matmul_body(x_ref, g_ref, w_ref, *out_refs, prec):
    h = (_rms_scale(_tile(x_ref)) * g_ref[...]).astype(w_ref.dtype)
    off = 0
    for o_ref in out_refs:
        n = o_ref.shape[-1]
        o_ref[...] = _dot(h, w_ref[:, off:off + n], prec).astype(o_ref.dtype).reshape(o_ref.shape)
        off += n


def _norm_matmul(x, g, w, prec, grid, x_spec, out_specs, out_shape):
    d, n = w.shape
    return pl.pallas_call(
        functools.partial(_norm_matmul_body, prec=prec),
        grid=grid,
        in_specs=[x_spec, _full((1, d)), _full((d, n))],
        out_specs=out_specs,
        out_shape=out_shape,
        compiler_params=_cparams(len(grid)),
        name="norm_matmul",
    )(x, g.reshape(1, d), w)


S5_LANE_GROUPS = LANES // SSM_GROUP
S5_STATE_LANES = S5_LANE_GROUPS * SSM_STATE
S5_PAIR = 2 * LANES


def _s5_prompt_body(u_ref, m_ref, s_ref, r_ref, a_ref, d_ref, y_ref, hfin_ref, e_s, hin_s, h_s):
    t_chunk, bsz, rc, _ = u_ref.shape
    rows = bsz * rc
    sl = S5_STATE_LANES

    @pl.when(pl.program_id(1) == 0)
    def _():
        h_s[...] = jnp.zeros(h_s.shape, F32)

    xs = [u_ref[s].reshape(rows, LANES) for s in range(t_chunk)]
    xcat = jnp.concatenate(xs, axis=1)
    half = sl // LANES
    cols = lambda j: slice(j * LANES, (j + 1) * LANES)
    e = _dot(xcat, s_ref[0])
    for j in range(2 * half):
        e_s[j] = e[:, cols(j)]
    ar = [jnp.broadcast_to(a_ref[0, 0:1, cols(j)], (bsz, LANES)) for j in range(half)]
    ai = [jnp.broadcast_to(a_ref[0, 1:2, cols(j)], (bsz, LANES)) for j in range(half)]

    def step(c, carry):
        hr, hi = carry
        at_c = pl.ds(c, bsz, stride=rc)
        new_r, new_i = [], []
        for j in range(half):
            hin_s[j, at_c, :] = hr[j]
            hin_s[half + j, at_c, :] = hi[j]
            new_r.append(ar[j] * hr[j] - ai[j] * hi[j] + e_s[j, at_c, :])
            new_i.append(ar[j] * hi[j] + ai[j] * hr[j] + e_s[half + j, at_c, :])
        return tuple(new_r), tuple(new_i)

    carry0 = (tuple(h_s[0, j] for j in range(half)), tuple(h_s[1, j] for j in range(half)))
    hr, hi = lax.fori_loop(0, rc, step, carry0)
    for j in range(half):
        h_s[0, j] = hr[j]
        h_s[1, j] = hi[j]
        hfin_ref[0, :, cols(j)] = hr[j]
        hfin_ref[0, :, cols(half + j)] = hi[j]
    hin = jnp.concatenate([hin_s[j] for j in range(2 * half)], axis=1).astype(BF16)
    d_row = d_ref[0]
    for i in range(t_chunk // 2):
        acc = _dot(hin, r_ref[0, :, i * S5_PAIR:(i + 1) * S5_PAIR])
        for j in range(i + 1):
            acc += _dot(xcat[:, j * S5_PAIR:(j + 1) * S5_PAIR], m_ref[0, i - j])
        for tt in range(2):
            t = 2 * i + tt
            y_t = acc[:, tt * LANES:(tt + 1) * LANES] + d_row * xs[t].astype(F32)
            y_ref[t] = y_t.reshape(bsz, rc, LANES)


def _s5_prompt(u5, ops, rc):
    t_chunk, bsz, nc, d = u5.shape
    n_lb = d // LANES
    m, s, r, a16, dl = ops
    sl = S5_STATE_LANES
    act = pl.BlockSpec((t_chunk, bsz, rc, LANES), lambda lb, ct: (0, 0, ct, lb))
    per_lb = lambda shape: pl.BlockSpec((1,) + shape, lambda lb, ct: (lb,) + (0,) * len(shape))
    return pl.pallas_call(
        _s5_prompt_body,
        grid=(n_lb, nc // rc),
        in_specs=[act, per_lb(m.shape[1:]), per_lb(s.shape[1:]), per_lb(r.shape[1:]),
                  per_lb((2, sl)), per_lb((1, LANES))],
        out_specs=[act, per_lb((bsz, 2 * sl))],
        out_shape=[jax.ShapeDtypeStruct(u5.shape, F32),
                   jax.ShapeDtypeStruct((n_lb, bsz, 2 * sl), F32)],
        scratch_shapes=[pltpu.VMEM((2 * sl // LANES, bsz * rc, LANES), F32),
                        pltpu.VMEM((2 * sl // LANES, bsz * rc, LANES), F32),
                        pltpu.VMEM((2, sl // LANES, bsz, LANES), F32)],
        compiler_params=_cparams(2),
        name="s5_prompt",
    )(u5, m, s, r, a16, dl)


def _s5_discretize(lam_re, lam_im, log_dt, b_re, b_im):
    dt = jnp.exp(log_dt)[:, None]
    mag = jnp.exp(lam_re * dt)
    ar = mag * jnp.cos(lam_im * dt)
    ai = mag * jnp.sin(lam_im * dt)
    den = lam_re * lam_re + lam_im * lam_im
    nr = ar - 1.0
    coef_re = (nr * lam_re + ai * lam_im) / den
    coef_im = (ai * lam_re - nr * lam_im) / den
    bb_re = coef_re[..., None] * b_re - coef_im[..., None] * b_im
    bb_im = coef_re[..., None] * b_im + coef_im[..., None] * b_re
    return ar, ai, bb_re, bb_im


def _s5_prompt_operators(ar, ai, bb_re, bb_im, c_re, c_im, d_skip):
    t = S5_CHUNK
    g, n = ar.shape
    p = SSM_GROUP
    lg = S5_LANE_GROUPS
    n_lb = g // lg
    pr, pi = [jnp.ones_like(ar)], [jnp.zeros_like(ar)]
    for _ in range(t):
        pr, pi = pr + [pr[-1] * ar - pi[-1] * ai], pi + [pr[-1] * ai + pi[-1] * ar]
    pr = jnp.stack(pr)
    pi = jnp.stack(pi)
    cp_re = c_re[None] * pr[:, :, None, :] - c_im[None] * pi[:, :, None, :]
    cp_im = c_re[None] * pi[:, :, None, :] + c_im[None] * pr[:, :, None, :]
    k = (jnp.einsum('tgpn,gnq->gtpq', cp_re[:t], bb_re, precision=HIGHEST)
         - jnp.einsum('tgpn,gnq->gtpq', cp_im[:t], bb_im, precision=HIGHEST))
    eye = jnp.eye(lg, dtype=F32)
    k_lag = lambda lag: k[:, lag] if lag >= 0 else jnp.zeros_like(k[:, 0])
    kt = jnp.stack([jnp.stack([jnp.stack([k_lag(2 * dl + tt - ss) for tt in range(2)], axis=1)
                               for ss in range(2)], axis=1) for dl in range(t // 2)], axis=1)
    kt = kt.reshape(n_lb, lg, t // 2, 2, 2, p, p)
    m = kt.transpose(0, 2, 3, 1, 6, 4, 5)[..., None, :] * eye[:, None, None, :, None]
    m = m.reshape(n_lb, t // 2, 2 * lg * p, 2 * lg * p)
    pr_rev = jnp.stack([pr[t - 1 - s] for s in range(t)])
    pi_rev = jnp.stack([pi[t - 1 - s] for s in range(t)])
    s_re = pr_rev[:, :, :, None] * bb_re[None] - pi_rev[:, :, :, None] * bb_im[None]
    s_im = pr_rev[:, :, :, None] * bb_im[None] + pi_rev[:, :, :, None] * bb_re[None]
    s_ri = jnp.stack([s_re, s_im]).reshape(2, t, n_lb, lg, n, p)
    s = s_ri.transpose(2, 1, 3, 5, 0, 4)[..., None, :] * eye[:, None, None, :, None]
    s = s.reshape(n_lb, t * lg * p, 2 * lg * n)
    r_ri = jnp.stack([cp_re[1:], -cp_im[1:]]).reshape(2, t, n_lb, lg, p, n)
    r = r_ri.transpose(2, 0, 3, 5, 1, 4)[..., None, :] * eye[:, None, None, :, None]
    r = r.reshape(n_lb, 2 * lg * n, t * lg * p)
    a16 = jnp.stack([pr[t].reshape(n_lb, lg * n), pi[t].reshape(n_lb, lg * n)], axis=1)
    return (m.astype(BF16), s.astype(BF16), r.astype(BF16), a16, d_skip.reshape(n_lb, 1, lg * p))


def _s5_step_body(u_ref, h0r_ref, h0i_ref, ar_ref, ai_ref, bbr_ref, bbi_ref, cr_ref, ci_ref,
                  d_ref, y_ref, hr_ref, hi_ref):
    u = u_ref[0]
    ar, ai = ar_ref[0], ai_ref[0]
    h0r, h0i = h0r_ref[0], h0i_ref[0]
    hr = _dot_nt(u, bbr_ref[0], HIGHEST) + ar * h0r - ai * h0i
    hi = _dot_nt(u, bbi_ref[0], HIGHEST) + ar * h0i + ai * h0r
    hr_ref[0] = hr
    hi_ref[0] = hi
    y_ref[0] = (_dot_nt(hr, cr_ref[0], HIGHEST) - _dot_nt(hi, ci_ref[0], HIGHEST)
                + d_ref[0] * u)


def _s5_step(u_g, h0r, h0i, ar, ai, bb_re, bb_im, c_re, c_im, d_skip):
    g, b, p = u_g.shape
    n = ar.shape[-1]
    per_group = lambda shape: pl.BlockSpec((1,) + shape, lambda i: (i, 0, 0))
    return pl.pallas_call(
        _s5_step_body,
        grid=(g,),
        in_specs=[per_group((b, p)), per_group((b, n)), per_group((b, n)),
                  per_group((1, n)), per_group((1, n)),
                  per_group((n, p)), per_group((n, p)),
                  per_group((p, n)), per_group((p, n)), per_group((1, p))],
        out_specs=[per_group((b, p)), per_group((b, n)), per_group((b, n))],
        out_shape=[jax.ShapeDtypeStruct((g, b, p), F32),
                   jax.ShapeDtypeStruct((g, b, n), F32),
                   jax.ShapeDtypeStruct((g, b, n), F32)],
        compiler_params=_cparams(1),
        name="s5_step",
    )(u_g, h0r, h0i, ar.reshape(g, 1, n), ai.reshape(g, 1, n), bb_re, bb_im, c_re, c_im,
      d_skip.reshape(g, 1, p))


def _post_a_body(y_ref, ga_ref, x_ref, wglu_ref, bglu_ref, wout_ref, gpost_ref, x1_ref, *, prec):
    y = _tile(y_ref)
    wdt = wglu_ref.dtype
    gy = 0.5 * y * (1.0 + lax.erf(y * math.sqrt(0.5)))
    z = _dot(gy.astype(wdt), wglu_ref[...], prec) + bglu_ref[...]
    gate = _tile(ga_ref)
    t = (gy * _sigmoid(z)) * (gate * _sigmoid(gate))
    o = _dot(t.astype(wdt), wout_ref[...], prec)
    x1_ref[...] = (_tile(x_ref) + _rms_scale(o) * gpost_ref[...]).reshape(x1_ref.shape)


def _post_a(y, gate, x, w_glu, b_glu, w_out, g_post, prec, grid, y_spec, x_spec):
    d = w_glu.shape[0]
    return pl.pallas_call(
        functools.partial(_post_a_body, prec=prec),
        grid=grid,
        in_specs=[y_spec, x_spec, x_spec, _full((d, d)), _full((1, d)), _full((d, d)), _full((1, d))],
        out_specs=x_spec,
        out_shape=jax.ShapeDtypeStruct(x.shape, F32),
        compiler_params=_cparams(len(grid)),
        name="post_a",
    )(y, gate, x, w_glu, b_glu.reshape(1, d), w_out, g_post.reshape(1, d))


ALIBI_ROWS = 16
FEAT_OFF = HEAD_DIM
V_ROWS = HEAD_DIM + 16


def _aug_width(nb):
    nbp = -(-nb // 16) * 16
    return -(-(FEAT_OFF + ALIBI_ROWS + nbp) // LANES) * LANES, nbp


def _pre_b_body(x1_ref, gkv_ref, wkv_ref, gb_ref, winb_ref, k_ref, v_ref, gate_ref, *rest, prec, nb):
    xn = _rms_scale(x1_ref[...])
    wdt = wkv_ref.dtype
    hk = (xn * gkv_ref[...]).astype(wdt)
    k = _dot(hk, wkv_ref[:, :KV_WIDTH], prec)
    v = _dot(hk, wkv_ref[:, KV_WIDTH:], prec)
    for h in range(KV_HEADS):
        k_ref[:, h, :] = k[:, h * HEAD_DIM:(h + 1) * HEAD_DIM]
        v_ref[:, h, :] = v[:, h * HEAD_DIM:(h + 1) * HEAD_DIM]
    hq = (xn * gb_ref[...]).astype(wdt)
    q = _dot(hq, winb_ref[:, :D_MODEL], prec) * SCALE
    gate_ref[...] = _dot(hq, winb_ref[:, D_MODEL:], prec)
    if nb is None:
        q_ref, = rest
        q_ref[...] = q
        return
    qt_ref, ka_ref, vta_ref, km_ref = rest
    tm = q.shape[0]
    qt_ref[0, 0] = q.T.astype(BF16)
    km_ref[0] = jnp.mean(k, axis=0, keepdims=True)
    n = pl.program_id(0) % nb
    width = ka_ref.shape[-1] - FEAT_OFF
    lane = lax.broadcasted_iota(jnp.int32, (tm, width), 1)
    rowi = lax.broadcasted_iota(jnp.int32, (tm, width), 0)
    feat = jnp.where(lane < 3, n, jnp.where(lane < 6, rowi, jnp.where(lane < 9, 1, 0)))
    feat = jnp.where(lane - ALIBI_ROWS == n, 1, feat).astype(F32).astype(BF16)
    kb = k.astype(BF16)
    vt = v.T
    ones_row = (lax.broadcasted_iota(jnp.int32, (V_ROWS - HEAD_DIM, tm), 0) == 0).astype(F32).astype(BF16)
    for h in range(KV_HEADS):
        ka_ref[0, h, :, :FEAT_OFF] = kb[:, h * HEAD_DIM:(h + 1) * HEAD_DIM]
        ka_ref[0, h, :, FEAT_OFF:] = feat
        vta_ref[0, 0, h * V_ROWS:h * V_ROWS + HEAD_DIM, :] = vt[h * HEAD_DIM:(h + 1) * HEAD_DIM].astype(BF16)
        vta_ref[0, 0, h * V_ROWS + HEAD_DIM:(h + 1) * V_ROWS, :] = ones_row


def _pre_b(x1, g_kv, w_kv, g_pre_b, w_in_b, tm, prec, prompt_blocks=None):
    m, d = x1.shape
    row = lambda wd: pl.BlockSpec((tm, wd), lambda i: (i, 0))
    kv_spec = pl.BlockSpec((tm, KV_HEADS, HEAD_DIM), lambda i: (i, 0, 0))
    kv_shape = jax.ShapeDtypeStruct((m, KV_HEADS, HEAD_DIM), F32)
    out_specs = [kv_spec, kv_spec, row(d)]
    out_shape = [kv_shape, kv_shape, jax.ShapeDtypeStruct((m, d), F32)]
    nb = None
    if prompt_blocks is None:
        out_specs += [row(d)]
        out_shape += [jax.ShapeDtypeStruct((m, d), F32)]
    else:
        bsz, nb = prompt_blocks
        assert tm == MOBA_BLOCK and m == bsz * nb * tm
        aug, _ = _aug_width(nb)
        out_specs += [pl.BlockSpec((1, 1, d, tm), lambda i: (i // nb, i % nb, 0, 0)),
                      pl.BlockSpec((1, KV_HEADS, tm, aug), lambda i: (i // nb, 0, i % nb, 0)),
                      pl.BlockSpec((1, 1, KV_HEADS * V_ROWS, tm), lambda i: (i // nb, i % nb, 0, 0)),
                      pl.BlockSpec((1, 1, KV_WIDTH), lambda i: (i, 0, 0))]
        out_shape += [jax.ShapeDtypeStruct((bsz, nb, d, tm), BF16),
                      jax.ShapeDtypeStruct((bsz, KV_HEADS, nb * tm, aug), BF16),
                      jax.ShapeDtypeStruct((bsz, nb, KV_HEADS * V_ROWS, tm), BF16),
                      jax.ShapeDtypeStruct((bsz * nb, 1, KV_WIDTH), F32)]
    return pl.pallas_call(
        functools.partial(_pre_b_body, prec=prec, nb=nb),
        grid=(m // tm,),
        in_specs=[row(d), _full((1, d)), _full((d, 2 * KV_WIDTH)), _full((1, d)), _full((d, 2 * d))],
        out_specs=out_specs,
        out_shape=out_shape,
        compiler_params=_cparams(1),
        name="pre_b",
    )(x1, g_kv.reshape(1, d), w_kv, g_pre_b.reshape(1, d), w_in_b)


def _top3_mask(gs, idx, valid, axis):
    n = gs.shape[axis]
    gs = jnp.where(valid, gs, NEG_INF)
    sel = jnp.zeros(gs.shape, F32)
    picks = []
    for _ in range(MOBA_TOPK):
        mx = jnp.max(gs, axis=axis, keepdims=True)
        first = jnp.min(jnp.where(gs == mx, idx, n), axis=axis, keepdims=True)
        pick = idx == first
        sel = jnp.where(pick & valid, 1.0, sel)
        gs = jnp.where(pick, REMOVED, gs)
        picks.append(first)
    return sel, picks


def _split3(x):
    hi = x.astype(BF16).astype(F32)
    mid = (x - hi).astype(BF16).astype(F32)
    lo = (x - hi - mid).astype(BF16).astype(F32)
    return [hi, mid, lo]


def _moba_prompt_body(qt_ref, g_ref, ka_ref, vta_ref, km_ref, o_ref, qa_s, qd_s):
    h = pl.program_id(1)
    c = pl.program_id(2)
    blk = MOBA_BLOCK
    nb = km_ref.shape[2]
    aug, nbp = _aug_width(nb)
    km = km_ref[0, 0]
    row = lax.broadcasted_iota(jnp.int32, (nb, blk), 0)
    frow = lax.broadcasted_iota(jnp.int32, (ALIBI_ROWS, blk), 0)
    pos_q = (c * blk + lax.broadcasted_iota(jnp.int32, (1, blk), 1)).astype(F32)
    for g in range(Q_PER_KV):
        lanes = slice(g * blk, (g + 1) * blk)
        qg = qt_ref[0, 0, g * HEAD_DIM:(g + 1) * HEAD_DIM, :]
        sel, _ = _top3_mask(_dot(km, qg.astype(F32), HIGHEST), row, row < c, 0)
        bias = jnp.where(sel > 0.0, 0.0, NEG_INF)
        if nbp > nb:
            bias = jnp.concatenate([bias, jnp.zeros((nbp - nb, blk), F32)], axis=0)
        head1 = (h * Q_PER_KV + g + 1).astype(F32)
        slope = jnp.exp2(-8.0 * jnp.full((1, blk), head1, F32) / N_HEADS)
        terms = _split3(slope * float(blk)) + _split3(slope) + _split3(-slope * pos_q)
        feat = jnp.zeros((ALIBI_ROWS, blk), F32)
        for r, term in enumerate(terms):
            feat = jnp.where(frow == r, term, feat)
        for ref, b_rows in ((qa_s, bias.astype(BF16)), (qd_s, jnp.zeros((nbp, blk), BF16))):
            ref[:FEAT_OFF, lanes] = qg
            ref[FEAT_OFF:FEAT_OFF + ALIBI_ROWS, lanes] = feat.astype(BF16)
            ref[FEAT_OFF + ALIBI_ROWS:FEAT_OFF + ALIBI_ROWS + nbp, lanes] = b_rows
            if FEAT_OFF + ALIBI_ROWS + nbp < aug:
                ref[FEAT_OFF + ALIBI_ROWS + nbp:, lanes] = jnp.zeros((aug - FEAT_OFF - ALIBI_ROWS - nbp, blk), BF16)

    def keys(n):
        return ka_ref[0, 0, pl.ds(pl.multiple_of(n * blk, blk), blk), :]

    s = _dot(keys(c), qd_s[...])
    key = lax.broadcasted_iota(jnp.int32, s.shape, 0)
    qry = lax.broadcasted_iota(jnp.int32, s.shape, 1) & (blk - 1)
    s = jnp.where(key <= qry, s, NEG_INF)
    m = jnp.max(s, axis=0, keepdims=True)
    acc = _dot(vta_ref[0, c], jnp.exp(s - m).astype(BF16))

    def block_pair(i, carry):
        m, acc = carry
        n0, n1 = 2 * i, 2 * i + 1
        s0 = _dot(keys(n0), qa_s[...])
        s1 = _dot(keys(n1), qa_s[...])
        m2 = jnp.maximum(m, jnp.maximum(jnp.max(s0, axis=0, keepdims=True), jnp.max(s1, axis=0, keepdims=True)))
        p0 = jnp.exp(s0 - m2).astype(BF16)
        p1 = jnp.exp(s1 - m2).astype(BF16)
        return m2, jnp.exp(m - m2) * acc + _dot(vta_ref[0, n0], p0) + _dot(vta_ref[0, n1], p1)

    m, acc = lax.fori_loop(0, (c + 1) // 2, block_pair, (m, acc))
    out_t = acc[:HEAD_DIM] / acc[HEAD_DIM:HEAD_DIM + 1]
    att = jnp.concatenate([out_t[:, g * blk:(g + 1) * blk] for g in range(Q_PER_KV)], axis=0).T
    gate = g_ref[0]
    o_ref[0] = (att * (gate * _sigmoid(gate))).astype(o_ref.dtype)


def _moba_prompt(qt, gate, ka, vta, km):
    bsz, seqlen, d = gate.shape
    nb = seqlen // MOBA_BLOCK
    width = Q_PER_KV * HEAD_DIM
    aug, _ = _aug_width(nb)
    tile = pl.BlockSpec((1, MOBA_BLOCK, width), lambda b, h, c: (b, c, h))
    return pl.pallas_call(
        _moba_prompt_body,
        grid=(bsz, KV_HEADS, nb),
        in_specs=[pl.BlockSpec((1, 1, width, MOBA_BLOCK), lambda b, h, c: (b, c, h, 0)),
                  tile,
                  pl.BlockSpec((1, 1, seqlen, aug), lambda b, h, c: (b, h, 0, 0)),
                  pl.BlockSpec((1, nb, V_ROWS, MOBA_BLOCK), lambda b, h, c: (b, 0, h, 0)),
                  pl.BlockSpec((1, 1, nb, HEAD_DIM), lambda b, h, c: (b, h, 0, 0))],
        out_specs=tile,
        out_shape=jax.ShapeDtypeStruct((bsz, seqlen, d), BF16),
        scratch_shapes=[pltpu.VMEM((aug, Q_PER_KV * MOBA_BLOCK), BF16) for _ in range(2)],
        compiler_params=_cparams(3),
        name="moba_prompt",
    )(qt, gate, ka, vta, km)


_KM_BLOCKS = 8


def _block_mean_body(pt_ref, *refs):
    del pt_ref
    pages, o_ref = refs[:-1], refs[-1]
    for j in range(_KM_BLOCKS):
        tot = sum(jnp.sum(pages[PAGES_PER_BLOCK * j + r][0], axis=0) for r in range(PAGES_PER_BLOCK))
        o_ref[0, j] = tot * (1.0 / MOBA_BLOCK)


def _block_means(cache, page_table, n_blocks):
    n_seq = page_table.shape[0]
    per_step = _KM_BLOCKS * PAGES_PER_BLOCK
    page_spec = lambda j: pl.BlockSpec(
        (1, PAGE_SIZE, KV_HEADS, HEAD_DIM), lambda b, s, pt, j=j: (pt[b, s * per_step + j], 0, 0, 0))
    return pl.pallas_call(
        _block_mean_body,
        grid_spec=pltpu.PrefetchScalarGridSpec(
            num_scalar_prefetch=1,
            grid=(n_seq, n_blocks // _KM_BLOCKS),
            in_specs=[page_spec(j) for j in range(per_step)],
            out_specs=pl.BlockSpec((1, _KM_BLOCKS, KV_HEADS, HEAD_DIM), lambda b, s, pt: (b, s, 0, 0))),
        out_shape=jax.ShapeDtypeStruct((n_seq, n_blocks, KV_HEADS, HEAD_DIM), F32),
        compiler_params=_cparams(2),
        name="block_means",
    )(page_table, *([cache] * per_step))


def _sample_topk_body(q_ref, km_ref, idx_ref, *, n_past):
    gs = _dot_nt(q_ref[0], km_ref[0], HIGHEST)
    lane = lax.broadcasted_iota(jnp.int32, gs.shape, 1)
    _, picks = _top3_mask(gs, lane, lane < n_past, 1)
    out_lane = lax.broadcasted_iota(jnp.int32, idx_ref.shape[1:], 1)
    out = jnp.zeros(idx_ref.shape[1:], jnp.int32)
    for r, first in enumerate(picks):
        out = jnp.where(out_lane == r, first, out)
    idx_ref[0] = out


def _sample_topk(q_exp, km):
    n_seq, n_heads, width = q_exp.shape
    n_past = km.shape[1]
    return pl.pallas_call(
        functools.partial(_sample_topk_body, n_past=n_past),
        grid=(n_seq,),
        in_specs=[pl.BlockSpec((1, n_heads, width), lambda b: (b, 0, 0)),
                  pl.BlockSpec((1, n_past, width), lambda b: (b, 0, 0))],
        out_specs=pl.BlockSpec((1, n_heads, LANES), lambda b: (b, 0, 0)),
        out_shape=jax.ShapeDtypeStruct((n_seq, n_heads, LANES), jnp.int32),
        compiler_params=_cparams(1),
        name="sample_topk",
    )(q_exp, km)


_Q_ROWS = 8


def _moba_sample_body(pt_ref, idx_ref, q_ref, kn_ref, vn_ref, *refs, n_past):
    del pt_ref
    n_sel = MOBA_TOPK * PAGES_PER_BLOCK
    k_pages, v_pages, o_ref = refs[:n_sel], refs[n_sel:2 * n_sel], refs[-1]
    b = pl.program_id(0)
    h = pl.program_id(1)
    kvh = h // Q_PER_KV
    keys = MOBA_TOPK * MOBA_BLOCK
    q = jnp.broadcast_to(q_ref[0, 0], (_Q_ROWS, HEAD_DIM))
    head_rows = lambda r: r[0, :, pl.ds(kvh, 1), :].reshape(PAGE_SIZE, HEAD_DIM)
    k_sel = jnp.concatenate([head_rows(r) for r in k_pages], axis=0)
    v_sel = jnp.concatenate([head_rows(r) for r in v_pages], axis=0)
    col = lax.broadcasted_iota(jnp.int32, (_Q_ROWS, keys), 1)
    slot = col // MOBA_BLOCK
    base = (b * N_HEADS + h) * MOBA_TOPK
    idx = jnp.zeros((_Q_ROWS, keys), jnp.int32)
    for r in range(MOBA_TOPK):
        idx = jnp.where(slot == r, idx_ref[base + r], idx)
    pos = n_past * MOBA_BLOCK
    dist = (pos - (idx * MOBA_BLOCK + col % MOBA_BLOCK)).astype(F32)
    slope = jnp.exp2(-8.0 * jnp.full((_Q_ROWS, keys), (h + 1).astype(F32), F32) / N_HEADS)
    s = _dot_nt(q.astype(BF16), k_sel.astype(BF16)) - slope * dist
    s = jnp.where(idx < n_past, s, NEG_INF)
    s_own = jnp.sum(q * kn_ref[0, 0], axis=-1, keepdims=True)
    m = jnp.maximum(jnp.max(s, axis=-1, keepdims=True), s_own)
    p = jnp.exp(s - m)
    p_own = jnp.exp(s_own - m)
    l = jnp.sum(p, axis=-1, keepdims=True) + p_own
    out = (_dot(p.astype(BF16), v_sel.astype(BF16)) + p_own * vn_ref[0, 0]) / l
    o_ref[0, 0] = out[0:1]


def _moba_sample(q, k_new, v_new, cache_k, cache_v, page_table, idx_flat, n_past):
    n_seq = page_table.shape[0]

    def page_spec(j):
        r, pg = divmod(j, PAGES_PER_BLOCK)

        def index(b, h, pt, idx):
            blk = idx[(b * N_HEADS + h) * MOBA_TOPK + r]
            return (pt[b, blk * PAGES_PER_BLOCK + pg], 0, 0, 0)

        return pl.BlockSpec((1, PAGE_SIZE, KV_HEADS, HEAD_DIM), index)

    n_sel = MOBA_TOPK * PAGES_PER_BLOCK
    row = (1, 1, 1, HEAD_DIM)
    new_spec = pl.BlockSpec(row, lambda b, h, pt, idx: (b, h // Q_PER_KV, 0, 0))
    head_spec = pl.BlockSpec(row, lambda b, h, pt, idx: (b, h, 0, 0))
    return pl.pallas_call(
        functools.partial(_moba_sample_body, n_past=n_past),
        grid_spec=pltpu.PrefetchScalarGridSpec(
            num_scalar_prefetch=2,
            grid=(n_seq, N_HEADS),
            in_specs=[head_spec, new_spec, new_spec] + [page_spec(j) for j in range(n_sel)] * 2,
            out_specs=head_spec),
        out_shape=jax.ShapeDtypeStruct((n_seq, N_HEADS, 1, HEAD_DIM), F32),
        compiler_params=_cparams(2),
        name="moba_sample",
    )(page_table, idx_flat, q, k_new, v_new, *([cache_k] * n_sel), *([cache_v] * n_sel))


def _out_b_body(a_ref, *refs, prec, gated):
    if gated:
        x1_ref, w_ref, g_ref, y_ref = refs
        a = a_ref[...]
    else:
        gate_ref, x1_ref, w_ref, g_ref, y_ref = refs
        gate = gate_ref[...]
        a = (a_ref[...] * (gate * _sigmoid(gate))).astype(w_ref.dtype)
    o = _dot(a, w_ref[...], prec)
    y_ref[...] = x1_ref[...] + _rms_scale(o) * g_ref[...]


def _out_b(att, gate, x1, w, g_post, tm, prec):
    m, d = x1.shape
    tile = pl.BlockSpec((tm, d), lambda i: (i, 0))
    gated = gate is None
    acts = [att, x1] if gated else [att, gate, x1]
    return pl.pallas_call(
        functools.partial(_out_b_body, prec=prec, gated=gated),
        grid=(m // tm,),
        in_specs=[tile] * len(acts) + [_full((d, d)), _full((1, d))],
        out_specs=tile,
        out_shape=jax.ShapeDtypeStruct((m, d), F32),
        compiler_params=_cparams(1),
        name="out_b",
    )(*acts, w, g_post.reshape(1, d))


def _prompt_trunk(x, p):
    bsz, seqlen, d = x.shape
    m = bsz * seqlen
    nb = seqlen // MOBA_BLOCK
    nc = seqlen // S5_CHUNK
    rt = min(nc, 256)
    x3 = x.reshape(bsz, nc, S5_CHUNK * d)
    grid = (bsz, nc // rt, S5_CHUNK)
    nat_spec = pl.BlockSpec((1, rt, d), lambda b, r, t: (b, r, t))
    tmaj_spec = pl.BlockSpec((1, 1, rt, d), lambda b, r, t: (t, b, r, 0))
    u5, gate_a = _norm_matmul(
        x3, p['g_pre_a'], p['w_in_a'].astype(BF16), None, grid, nat_spec, [tmaj_spec, nat_spec],
        [jax.ShapeDtypeStruct((S5_CHUNK, bsz, nc, d), BF16), jax.ShapeDtypeStruct(x3.shape, F32)])
    ar, ai, bb_re, bb_im = _s5_discretize(p['lambda_re'], p['lambda_im'], p['log_dt'], p['b_re'], p['b_im'])
    ops = _s5_prompt_operators(ar, ai, bb_re, bb_im, p['c_re'], p['c_im'], p['d_skip'])
    y5, h_fin = _s5_prompt(u5, ops, min(nc, 64))
    h_fin = h_fin.reshape(d // LANES, bsz, 2, S5_LANE_GROUPS, SSM_STATE).transpose(2, 1, 0, 3, 4)
    h_fin = h_fin.reshape(2, 1, bsz, SSM_GROUPS, SSM_STATE)
    x1 = _post_a(y5, gate_a, x3, p['w_glu'].astype(BF16), p['b_glu'], p['w_out_a'].astype(BF16),
                 p['g_post_a'], None, grid, tmaj_spec, nat_spec).reshape(m, d)
    k, v, gate_b, qt, ka, vta, km = _pre_b(x1, p['g_kv'], p['w_kv'].astype(BF16), p['g_pre_b'],
                                           p['w_in_b'].astype(BF16), MOBA_BLOCK, None, (bsz, nb))
    km = km.reshape(bsz, nb, KV_HEADS, HEAD_DIM).transpose(0, 2, 1, 3)
    att = _moba_prompt(qt, gate_b.reshape(bsz, seqlen, d), ka, vta, km)
    y_out = _out_b(att.reshape(m, d), None, x1, p['w_out_b'].astype(BF16), p['g_post_b'], 512, None)
    return (y_out.reshape(bsz, seqlen, d), h_fin[0], h_fin[1],
            k.reshape(bsz, seqlen, KV_HEADS, HEAD_DIM), v.reshape(bsz, seqlen, KV_HEADS, HEAD_DIM))


def _sample_trunk(x, h0_re, h0_im, cache_k, cache_v, page_table, p):
    n_seq, seqlen, d = x.shape
    assert seqlen == 1
    g, pch = SSM_GROUPS, SSM_GROUP
    n_past = page_table.shape[1] // PAGES_PER_BLOCK
    x2 = x.reshape(n_seq, d)
    whole = pl.BlockSpec((n_seq, d), lambda i: (0, 0))
    u, gate_a = _norm_matmul(x2, p['g_pre_a'], p['w_in_a'], HIGHEST, (1,), whole, [whole, whole],
                             [jax.ShapeDtypeStruct((n_seq, d), F32)] * 2)
    ar, ai, bb_re, bb_im = _s5_discretize(p['lambda_re'], p['lambda_im'], p['log_dt'], p['b_re'], p['b_im'])
    y_g, hr, hi = _s5_step(u.reshape(n_seq, g, pch).transpose(1, 0, 2),
                           h0_re.transpose(1, 0, 2), h0_im.transpose(1, 0, 2),
                           ar, ai, bb_re, bb_im, p['c_re'], p['c_im'], p['d_skip'])
    y = y_g.transpose(1, 0, 2).reshape(n_seq, d)
    x1 = _post_a(y, gate_a, x2, p['w_glu'], p['b_glu'], p['w_out_a'], p['g_post_a'], HIGHEST, (1,), whole, whole)
    k, v, gate_b, q = _pre_b(x1, p['g_kv'], p['w_kv'], p['g_pre_b'], p['w_in_b'], n_seq, HIGHEST)
    km = _block_means(cache_k, page_table, n_past).reshape(n_seq, n_past, KV_WIDTH)
    eye = jnp.eye(KV_HEADS, dtype=F32)
    q_exp = (q.reshape(n_seq, KV_HEADS, Q_PER_KV, 1, HEAD_DIM) * eye[None, :, None, :, None])
    q_exp = q_exp.reshape(n_seq, N_HEADS, KV_WIDTH)
    idx = _sample_topk(q_exp, km)[:, :, :MOBA_TOPK].reshape(-1)
    att = _moba_sample(q.reshape(n_seq, N_HEADS, 1, HEAD_DIM), k.reshape(n_seq, KV_HEADS, 1, HEAD_DIM),
                       v.reshape(n_seq, KV_HEADS, 1, HEAD_DIM), cache_k, cache_v, page_table, idx, n_past)
    y_out = _out_b(att.reshape(n_seq, d), gate_b, x1, p['w_out_b'], p['g_post_b'], n_seq, HIGHEST)
    return (y_out.reshape(n_seq, 1, d), hr.transpose(1, 0, 2)[None], hi.transpose(1, 0, 2)[None],
            k.reshape(n_seq, 1, KV_HEADS, HEAD_DIM), v.reshape(n_seq, 1, KV_HEADS, HEAD_DIM))


def kernel(x_prompt, x_sample, state_ssm_re, state_ssm_im, cache_k, cache_v, page_table, g_pre_a, w_in_a, lambda_re, lambda_im, log_dt, b_re, b_im, c_re, c_im, d_skip, w_glu, b_glu, w_out_a, g_post_a, g_kv, w_kv, g_pre_b, w_in_b, w_out_b, g_post_b):
    p = {'g_pre_a': g_pre_a[0], 'w_in_a': w_in_a[0], 'lambda_re': lambda_re[0], 'lambda_im': lambda_im[0],
         'log_dt': log_dt[0], 'b_re': b_re[0], 'b_im': b_im[0], 'c_re': c_re[0], 'c_im': c_im[0],
         'd_skip': d_skip[0], 'w_glu': w_glu[0], 'b_glu': b_glu[0], 'w_out_a': w_out_a[0],
         'g_post_a': g_post_a[0], 'g_kv': g_kv, 'w_kv': w_kv, 'g_pre_b': g_pre_b[0],
         'w_in_b': w_in_b[0], 'w_out_b': w_out_b[0], 'g_post_b': g_post_b[0]}
    y_p, re_p, im_p, k_p, v_p = _prompt_trunk(x_prompt, p)
    y_s, re_s, im_s, k_s, v_s = _sample_trunk(x_sample, state_ssm_re[0], state_ssm_im[0],
                                              cache_k, cache_v, page_table, p)
    return (y_p, y_s, re_p, im_p, k_p, v_p, re_s, im_s, k_s, v_s)
```

```python
import functools
import math

import jax
import jax.numpy as jnp
from jax import lax
from jax.experimental import pallas as pl
from jax.experimental.pallas import tpu as pltpu

F32 = jnp.float32
BF16 = jnp.bfloat16
HIGHEST = lax.Precision.HIGHEST

D_MODEL = 1024
SSM_GROUP = 16
SSM_GROUPS = D_MODEL // SSM_GROUP
SSM_STATE = 64
HEAD_DIM = 64
N_HEADS = D_MODEL // HEAD_DIM
KV_HEADS = 4
Q_PER_KV = N_HEADS // KV_HEADS
KV_WIDTH = KV_HEADS * HEAD_DIM
MOBA_BLOCK = 256
MOBA_TOPK = 3
PAGE_SIZE = 128
PAGES_PER_BLOCK = MOBA_BLOCK // PAGE_SIZE
EPS = 1e-6
NEG_INF = -1e30
REMOVED = -3e38
SCALE = HEAD_DIM ** -0.5
S5_CHUNK = 16
LANES = 128
VMEM_LIMIT = 56 * 1024 * 1024


def _cparams(n_grid):
    return pltpu.CompilerParams(dimension_semantics=("arbitrary",) * n_grid,
                                vmem_limit_bytes=VMEM_LIMIT)


def _dot(a, b, prec=None):
    return jnp.dot(a, b, preferred_element_type=F32, precision=prec)


def _dot_nt(a, b, prec=None):
    return lax.dot_general(a, b, (((1,), (1,)), ((), ())),
                           preferred_element_type=F32, precision=prec)


def _rms_scale(x):
    return x * lax.rsqrt(jnp.mean(x * x, axis=-1, keepdims=True) + EPS)


def _sigmoid(x):
    return 1.0 / (1.0 + jnp.exp(-x))


def _full(shape):
    zeros = (0,) * len(shape)
    return pl.BlockSpec(shape, lambda *_: zeros)


def _tile(ref):
    x = ref[...]
    return x.reshape(x.shape[-2:])


def _norm_matmul_body(x_ref, g_ref, w_ref, *out_refs, prec):
    h = (_rms_scale(_tile(x_ref)) * g_ref[...]).astype(w_ref.dtype)
    off = 0
    for o_ref in out_refs:
        n = o_ref.shape[-1]
        o_ref[...] = _dot(h, w_ref[:, off:off + n], prec).astype(o_ref.dtype).reshape(o_ref.shape)
        off += n


def _norm_matmul(x, g, w, prec, grid, x_spec, out_specs, out_shape):
    d, n = w.shape
    return pl.pallas_call(
        functools.partial(_norm_matmul_body, prec=prec),
        grid=grid,
        in_specs=[x_spec, _full((1, d)), _full((d, n))],
        out_specs=out_specs,
        out_shape=out_shape,
        compiler_params=_cparams(len(grid)),
        name="norm_matmul",
    )(x, g.reshape(1, d), w)


S5_LANE_GROUPS = LANES // SSM_GROUP
S5_STATE_LANES = S5_LANE_GROUPS * SSM_STATE
S5_PAIR = 2 * LANES


def _s5_prompt_body(u_ref, m_ref, s_ref, r_ref, a_ref, d_ref, y_ref, hfin_ref, e_s, hin_s, h_s):
    t_chunk, bsz, rc, _ = u_ref.shape
    rows = bsz * rc
    sl = S5_STATE_LANES

    @pl.when(pl.program_id(1) == 0)
    def _():
        h_s[...] = jnp.zeros(h_s.shape, F32)

    xs = [u_ref[s].reshape(rows, LANES) for s in range(t_chunk)]
    xcat = jnp.concatenate(xs, axis=1)
    half = sl // LANES
    cols = lambda j: slice(j * LANES, (j + 1) * LANES)
    e = _dot(xcat, s_ref[0])
    for j in range(2 * half):
        e_s[j] = e[:, cols(j)]
    ar = [jnp.broadcast_to(a_ref[0, 0:1, cols(j)], (bsz, LANES)) for j in range(half)]
    ai = [jnp.broadcast_to(a_ref[0, 1:2, cols(j)], (bsz, LANES)) for j in range(half)]

    def step(c, carry):
        hr, hi = carry
        at_c = pl.ds(c, bsz, stride=rc)
        new_r, new_i = [], []
        for j in range(half):
            hin_s[j, at_c, :] = hr[j]
            hin_s[half + j, at_c, :] = hi[j]
            new_r.append(ar[j] * hr[j] - ai[j] * hi[j] + e_s[j, at_c, :])
            new_i.append(ar[j] * hi[j] + ai[j] * hr[j] + e_s[half + j, at_c, :])
        return tuple(new_r), tuple(new_i)

    carry0 = (tuple(h_s[0, j] for j in range(half)), tuple(h_s[1, j] for j in range(half)))
    hr, hi = lax.fori_loop(0, rc, step, carry0)
    for j in range(half):
        h_s[0, j] = hr[j]
        h_s[1, j] = hi[j]
        hfin_ref[0, :, cols(j)] = hr[j]
        hfin_ref[0, :, cols(half + j)] = hi[j]
    hin = jnp.concatenate([hin_s[j] for j in range(2 * half)], axis=1).astype(BF16)
    d_row = d_ref[0]
    for i in range(t_chunk // 2):
        acc = _dot(hin, r_ref[0, :, i * S5_PAIR:(i + 1) * S5_PAIR])
        for j in range(i + 1):
            acc += _dot(xcat[:, j * S5_PAIR:(j + 1) * S5_PAIR], m_ref[0, i - j])
        for tt in range(2):
            t = 2 * i + tt
            y_t = acc[:, tt * LANES:(tt + 1) * LANES] + d_row * xs[t].astype(F32)
            y_ref[t] = y_t.reshape(bsz, rc, LANES)


def _s5_prompt(u5, ops, rc):
    t_chunk, bsz, nc, d = u5.shape
    n_lb = d // LANES
    m, s, r, a16, dl = ops
    sl = S5_STATE_LANES
    act = pl.BlockSpec((t_chunk, bsz, rc, LANES), lambda lb, ct: (0, 0, ct, lb))
    per_lb = lambda shape: pl.BlockSpec((1,) + shape, lambda lb, ct: (lb,) + (0,) * len(shape))
    return pl.pallas_call(
        _s5_prompt_body,
        grid=(n_lb, nc // rc),
        in_specs=[act, per_lb(m.shape[1:]), per_lb(s.shape[1:]), per_lb(r.shape[1:]),
                  per_lb((2, sl)), per_lb((1, LANES))],
        out_specs=[act, per_lb((bsz, 2 * sl))],
        out_shape=[jax.ShapeDtypeStruct(u5.shape, F32),
                   jax.ShapeDtypeStruct((n_lb, bsz, 2 * sl), F32)],
        scratch_shapes=[pltpu.VMEM((2 * sl // LANES, bsz * rc, LANES), F32),
                        pltpu.VMEM((2 * sl // LANES, bsz * rc, LANES), F32),
                        pltpu.VMEM((2, sl // LANES, bsz, LANES), F32)],
        compiler_params=_cparams(2),
        name="s5_prompt",
    )(u5, m, s, r, a16, dl)


def _s5_discretize(lam_re, lam_im, log_dt, b_re, b_im):
    dt = jnp.exp(log_dt)[:, None]
    mag = jnp.exp(lam_re * dt)
    ar = mag * jnp.cos(lam_im * dt)
    ai = mag * jnp.sin(lam_im * dt)
    den = lam_re * lam_re + lam_im * lam_im
    nr = ar - 1.0
    coef_re = (nr * lam_re + ai * lam_im) / den
    coef_im = (ai * lam_re - nr * lam_im) / den
    bb_re = coef_re[..., None] * b_re - coef_im[..., None] * b_im
    bb_im = coef_re[..., None] * b_im + coef_im[..., None] * b_re
    return ar, ai, bb_re, bb_im


def _s5_prompt_operators(ar, ai, bb_re, bb_im, c_re, c_im, d_skip):
    t = S5_CHUNK
    g, n = ar.shape
    p = SSM_GROUP
    lg = S5_LANE_GROUPS
    n_lb = g // lg
    pr, pi = [jnp.ones_like(ar)], [jnp.zeros_like(ar)]
    for _ in range(t):
        pr, pi = pr + [pr[-1] * ar - pi[-1] * ai], pi + [pr[-1] * ai + pi[-1] * ar]
    pr = jnp.stack(pr)
    pi = jnp.stack(pi)
    cp_re = c_re[None] * pr[:, :, None, :] - c_im[None] * pi[:, :, None, :]
    cp_im = c_re[None] * pi[:, :, None, :] + c_im[None] * pr[:, :, None, :]
    k = (jnp.einsum('tgpn,gnq->gtpq', cp_re[:t], bb_re, precision=HIGHEST)
         - jnp.einsum('tgpn,gnq->gtpq', cp_im[:t], bb_im, precision=HIGHEST))
    eye = jnp.eye(lg, dtype=F32)
    k_lag = lambda lag: k[:, lag] if lag >= 0 else jnp.zeros_like(k[:, 0])
    kt = jnp.stack([jnp.stack([jnp.stack([k_lag(2 * dl + tt - ss) for tt in range(2)], axis=1)
                               for ss in range(2)], axis=1) for dl in range(t // 2)], axis=1)
    kt = kt.reshape(n_lb, lg, t // 2, 2, 2, p, p)
    m = kt.transpose(0, 2, 3, 1, 6, 4, 5)[..., None, :] * eye[:, None, None, :, None]
    m = m.reshape(n_lb, t // 2, 2 * lg * p, 2 * lg * p)
    pr_rev = jnp.stack([pr[t - 1 - s] for s in range(t)])
    pi_rev = jnp.stack([pi[t - 1 - s] for s in range(t)])
    s_re = pr_rev[:, :, :, None] * bb_re[None] - pi_rev[:, :, :, None] * bb_im[None]
    s_im = pr_rev[:, :, :, None] * bb_im[None] + pi_rev[:, :, :, None] * bb_re[None]
    s_ri = jnp.stack([s_re, s_im]).reshape(2, t, n_lb, lg, n, p)
    s = s_ri.transpose(2, 1, 3, 5, 0, 4)[..., None, :] * eye[:, None, None, :, None]
    s = s.reshape(n_lb, t * lg * p, 2 * lg * n)
    r_ri = jnp.stack([cp_re[1:], -cp_im[1:]]).reshape(2, t, n_lb, lg, p, n)
    r = r_ri.transpose(2, 0, 3, 5, 1, 4)[..., None, :] * eye[:, None, None, :, None]
    r = r.reshape(n_lb, 2 * lg * n, t * lg * p)
    a16 = jnp.stack([pr[t].reshape(n_lb, lg * n), pi[t].reshape(n_lb, lg * n)], axis=1)
    return (m.astype(BF16), s.astype(BF16), r.astype(BF16), a16, d_skip.reshape(n_lb, 1, lg * p))


def _s5_step_body(u_ref, h0r_ref, h0i_ref, ar_ref, ai_ref, bbr_ref, bbi_ref, cr_ref, ci_ref,
                  d_ref, y_ref, hr_ref, hi_ref):
    u = u_ref[0]
    ar, ai = ar_ref[0], ai_ref[0]
    h0r, h0i = h0r_ref[0], h0i_ref[0]
    hr = _dot_nt(u, bbr_ref[0], HIGHEST) + ar * h0r - ai * h0i
    hi = _dot_nt(u, bbi_ref[0], HIGHEST) + ar * h0i + ai * h0r
    hr_ref[0] = hr
    hi_ref[0] = hi
    y_ref[0] = (_dot_nt(hr, cr_ref[0], HIGHEST) - _dot_nt(hi, ci_ref[0], HIGHEST)
                + d_ref[0] * u)


def _s5_step(u_g, h0r, h0i, ar, ai, bb_re, bb_im, c_re, c_im, d_skip):
    g, b, p = u_g.shape
    n = ar.shape[-1]
    per_group = lambda shape: pl.BlockSpec((1,) + shape, lambda i: (i, 0, 0))
    return pl.pallas_call(
        _s5_step_body,
        grid=(g,),
        in_specs=[per_group((b, p)), per_group((b, n)), per_group((b, n)),
                  per_group((1, n)), per_group((1, n)),
                  per_group((n, p)), per_group((n, p)),
                  per_group((p, n)), per_group((p, n)), per_group((1, p))],
        out_specs=[per_group((b, p)), per_group((b, n)), per_group((b, n))],
        out_shape=[jax.ShapeDtypeStruct((g, b, p), F32),
                   jax.ShapeDtypeStruct((g, b, n), F32),
                   jax.ShapeDtypeStruct((g, b, n), F32)],
        compiler_params=_cparams(1),
        name="s5_step",
    )(u_g, h0r, h0i, ar.reshape(g, 1, n), ai.reshape(g, 1, n), bb_re, bb_im, c_re, c_im,
      d_skip.reshape(g, 1, p))


def _post_a_body(y_ref, ga_ref, x_ref, wglu_ref, bglu_ref, wout_ref, gpost_ref, x1_ref, *, prec):
    y = _tile(y_ref)
    wdt = wglu_ref.dtype
    gy = 0.5 * y * (1.0 + lax.erf(y * math.sqrt(0.5)))
    z = _dot(gy.astype(wdt), wglu_ref[...], prec) + bglu_ref[...]
    gate = _tile(ga_ref)
    t = (gy * _sigmoid(z)) * (gate * _sigmoid(gate))
    o = _dot(t.astype(wdt), wout_ref[...], prec)
    x1_ref[...] = (_tile(x_ref) + _rms_scale(o) * gpost_ref[...]).reshape(x1_ref.shape)


def _post_a(y, gate, x, w_glu, b_glu, w_out, g_post, prec, grid, y_spec, x_spec):
    d = w_glu.shape[0]
    return pl.pallas_call(
        functools.partial(_post_a_body, prec=prec),
        grid=grid,
        in_specs=[y_spec, x_spec, x_spec, _full((d, d)), _full((1, d)), _full((d, d)), _full((1, d))],
        out_specs=x_spec,
        out_shape=jax.ShapeDtypeStruct(x.shape, F32),
        compiler_params=_cparams(len(grid)),
        name="post_a",
    )(y, gate, x, w_glu, b_glu.reshape(1, d), w_out, g_post.reshape(1, d))


ALIBI_ROWS = 16
FEAT_OFF = HEAD_DIM
V_ROWS = HEAD_DIM + 16


def _aug_width(nb):
    nbp = -(-nb // 16) * 16
    return -(-(FEAT_OFF + ALIBI_ROWS + nbp) // LANES) * LANES, nbp


def _pre_b_body(x1_ref, gkv_ref, wkv_ref, gb_ref, winb_ref, k_ref, v_ref, gate_ref, *rest, prec, nb):
    xn = _rms_scale(x1_ref[...])
    wdt = wkv_ref.dtype
    hk = (xn * gkv_ref[...]).astype(wdt)
    k = _dot(hk, wkv_ref[:, :KV_WIDTH], prec)
    v = _dot(hk, wkv_ref[:, KV_WIDTH:], prec)
    hq = (xn * gb_ref[...]).astype(wdt)
    q = _dot(hq, winb_ref[:, :D_MODEL], prec) * SCALE
    gate_ref[...] = _dot(hq, winb_ref[:, D_MODEL:], prec)
    if nb is None:
        q_ref, = rest
        q_ref[...] = q
        k_ref[...] = k
        v_ref[...] = v
        return
    qt_ref, ka_ref, vta_ref, km_ref = rest
    tm = q.shape[0]
    qt_ref[0, 0] = q.T.astype(BF16)
    km_ref[0] = jnp.mean(k, axis=0, keepdims=True)
    n = pl.program_id(0) % nb
    width = ka_ref.shape[-1] - FEAT_OFF
    lane = lax.broadcasted_iota(jnp.int32, (tm, width), 1)
    rowi = lax.broadcasted_iota(jnp.int32, (tm, width), 0)
    feat = jnp.where(lane < 3, n, jnp.where(lane < 6, rowi, jnp.where(lane < 9, 1, 0)))
    feat = jnp.where(lane - ALIBI_ROWS == n, 1, feat).astype(F32).astype(BF16)
    kb = k.astype(BF16)
    vt = v.T
    k_ref[0] = k.T
    v_ref[0] = vt
    ones_row = (lax.broadcasted_iota(jnp.int32, (V_ROWS - HEAD_DIM, tm), 0) == 0).astype(F32).astype(BF16)
    for h in range(KV_HEADS):
        ka_ref[0, h, :, :FEAT_OFF] = kb[:, h * HEAD_DIM:(h + 1) * HEAD_DIM]
        ka_ref[0, h, :, FEAT_OFF:] = feat
        vta_ref[0, 0, h * V_ROWS:h * V_ROWS + HEAD_DIM, :] = vt[h * HEAD_DIM:(h + 1) * HEAD_DIM].astype(BF16)
        vta_ref[0, 0, h * V_ROWS + HEAD_DIM:(h + 1) * V_ROWS, :] = ones_row


def _pre_b(x1, g_kv, w_kv, g_pre_b, w_in_b, tm, prec, prompt_blocks=None):
    m, d = x1.shape
    row = lambda wd: pl.BlockSpec((tm, wd), lambda i: (i, 0))
    nb = None
    if prompt_blocks is None:
        kv_spec = row(KV_WIDTH)
        kv_shape = jax.ShapeDtypeStruct((m, KV_WIDTH), F32)
        out_specs = [kv_spec, kv_spec, row(d), row(d)]
        out_shape = [kv_shape, kv_shape, jax.ShapeDtypeStruct((m, d), F32), jax.ShapeDtypeStruct((m, d), F32)]
    else:
        bsz, nb = prompt_blocks
        assert tm == MOBA_BLOCK and m == bsz * nb * tm
        aug, _ = _aug_width(nb)
        kv_spec = pl.BlockSpec((1, KV_WIDTH, tm), lambda i: (i // nb, 0, i % nb))
        kv_shape = jax.ShapeDtypeStruct((bsz, KV_WIDTH, nb * tm), F32)
        out_specs = [kv_spec, kv_spec, row(d)]
        out_shape = [kv_shape, kv_shape, jax.ShapeDtypeStruct((m, d), F32)]
        out_specs += [pl.BlockSpec((1, 1, d, tm), lambda i: (i // nb, i % nb, 0, 0)),
                      pl.BlockSpec((1, KV_HEADS, tm, aug), lambda i: (i // nb, 0, i % nb, 0)),
                      pl.BlockSpec((1, 1, KV_HEADS * V_ROWS, tm), lambda i: (i // nb, i % nb, 0, 0)),
                      pl.BlockSpec((1, 1, KV_WIDTH), lambda i: (i, 0, 0))]
        out_shape += [jax.ShapeDtypeStruct((bsz, nb, d, tm), BF16),
                      jax.ShapeDtypeStruct((bsz, KV_HEADS, nb * tm, aug), BF16),
                      jax.ShapeDtypeStruct((bsz, nb, KV_HEADS * V_ROWS, tm), BF16),
                      jax.ShapeDtypeStruct((bsz * nb, 1, KV_WIDTH), F32)]
    return pl.pallas_call(
        functools.partial(_pre_b_body, prec=prec, nb=nb),
        grid=(m // tm,),
        in_specs=[row(d), _full((1, d)), _full((d, 2 * KV_WIDTH)), _full((1, d)), _full((d, 2 * d))],
        out_specs=out_specs,
        out_shape=out_shape,
        compiler_params=_cparams(1),
        name="pre_b",
    )(x1, g_kv.reshape(1, d), w_kv, g_pre_b.reshape(1, d), w_in_b)


def _top3_mask(gs, idx, valid, axis):
    n = gs.shape[axis]
    gs = jnp.where(valid, gs, NEG_INF)
    sel = jnp.zeros(gs.shape, F32)
    picks = []
    for _ in range(MOBA_TOPK):
        mx = jnp.max(gs, axis=axis, keepdims=True)
        first = jnp.min(jnp.where(gs == mx, idx, n), axis=axis, keepdims=True)
        pick = idx == first
        sel = jnp.where(pick & valid, 1.0, sel)
        gs = jnp.where(pick, REMOVED, gs)
        picks.append(first)
    return sel, picks


def _split3(x):
    hi = x.astype(BF16).astype(F32)
    mid = (x - hi).astype(BF16).astype(F32)
    lo = (x - hi - mid).astype(BF16).astype(F32)
    return [hi, mid, lo]


def _moba_prompt_body(qt_ref, g_ref, ka_ref, vta_ref, km_ref, o_ref, qa_s, qd_s):
    h = pl.program_id(1)
    c = pl.program_id(2)
    blk = MOBA_BLOCK
    nb = km_ref.shape[2]
    aug, nbp = _aug_width(nb)
    km = km_ref[0, 0]
    row = lax.broadcasted_iota(jnp.int32, (nb, blk), 0)
    frow = lax.broadcasted_iota(jnp.int32, (ALIBI_ROWS, blk), 0)
    pos_q = (c * blk + lax.broadcasted_iota(jnp.int32, (1, blk), 1)).astype(F32)
    for g in range(Q_PER_KV):
        lanes = slice(g * blk, (g + 1) * blk)
        qg = qt_ref[0, 0, g * HEAD_DIM:(g + 1) * HEAD_DIM, :]
        sel, _ = _top3_mask(_dot(km, qg.astype(F32), HIGHEST), row, row < c, 0)
        bias = jnp.where(sel > 0.0, 0.0, NEG_INF)
        if nbp > nb:
            bias = jnp.concatenate([bias, jnp.zeros((nbp - nb, blk), F32)], axis=0)
        head1 = (h * Q_PER_KV + g + 1).astype(F32)
        slope = jnp.exp2(-8.0 * jnp.full((1, blk), head1, F32) / N_HEADS)
        terms = _split3(slope * float(blk)) + _split3(slope) + _split3(-slope * pos_q)
        feat = jnp.zeros((ALIBI_ROWS, blk), F32)
        for r, term in enumerate(terms):
            feat = jnp.where(frow == r, term, feat)
        for ref, b_rows in ((qa_s, bias.astype(BF16)), (qd_s, jnp.zeros((nbp, blk), BF16))):
            ref[:FEAT_OFF, lanes] = qg
            ref[FEAT_OFF:FEAT_OFF + ALIBI_ROWS, lanes] = feat.astype(BF16)
            ref[FEAT_OFF + ALIBI_ROWS:FEAT_OFF + ALIBI_ROWS + nbp, lanes] = b_rows
            if FEAT_OFF + ALIBI_ROWS + nbp < aug:
                ref[FEAT_OFF + ALIBI_ROWS + nbp:, lanes] = jnp.zeros((aug - FEAT_OFF - ALIBI_ROWS - nbp, blk), BF16)

    def keys(n):
        return ka_ref[0, 0, pl.ds(pl.multiple_of(n * blk, blk), blk), :]

    s = _dot(keys(c), qd_s[...])
    key = lax.broadcasted_iota(jnp.int32, s.shape, 0)
    qry = lax.broadcasted_iota(jnp.int32, s.shape, 1) & (blk - 1)
    s = jnp.where(key <= qry, s, NEG_INF)
    m = jnp.max(s, axis=0, keepdims=True)
    acc = _dot(vta_ref[0, c], jnp.exp(s - m).astype(BF16))

    def block_pair(i, carry):
        m, acc = carry
        n0, n1 = 2 * i, 2 * i + 1
        s0 = _dot(keys(n0), qa_s[...])
        s1 = _dot(keys(n1), qa_s[...])
        m2 = jnp.maximum(m, jnp.maximum(jnp.max(s0, axis=0, keepdims=True), jnp.max(s1, axis=0, keepdims=True)))
        p0 = jnp.exp(s0 - m2).astype(BF16)
        p1 = jnp.exp(s1 - m2).astype(BF16)
        return m2, jnp.exp(m - m2) * acc + _dot(vta_ref[0, n0], p0) + _dot(vta_ref[0, n1], p1)

    m, acc = lax.fori_loop(0, (c + 1) // 2, block_pair, (m, acc))
    out_t = acc[:HEAD_DIM] / acc[HEAD_DIM:HEAD_DIM + 1]
    att = jnp.concatenate([out_t[:, g * blk:(g + 1) * blk] for g in range(Q_PER_KV)], axis=0).T
    gate = g_ref[0]
    o_ref[0] = (att * (gate * _sigmoid(gate))).astype(o_ref.dtype)


def _moba_prompt(qt, gate, ka, vta, km):
    bsz, seqlen, d = gate.shape
    nb = seqlen // MOBA_BLOCK
    width = Q_PER_KV * HEAD_DIM
    aug, _ = _aug_width(nb)
    tile = pl.BlockSpec((1, MOBA_BLOCK, width), lambda b, h, c: (b, c, h))
    return pl.pallas_call(
        _moba_prompt_body,
        grid=(bsz, KV_HEADS, nb),
        in_specs=[pl.BlockSpec((1, 1, width, MOBA_BLOCK), lambda b, h, c: (b, c, h, 0)),
                  tile,
                  pl.BlockSpec((1, 1, seqlen, aug), lambda b, h, c: (b, h, 0, 0)),
                  pl.BlockSpec((1, nb, V_ROWS, MOBA_BLOCK), lambda b, h, c: (b, 0, h, 0)),
                  pl.BlockSpec((1, 1, nb, HEAD_DIM), lambda b, h, c: (b, h, 0, 0))],
        out_specs=tile,
        out_shape=jax.ShapeDtypeStruct((bsz, seqlen, d), BF16),
        scratch_shapes=[pltpu.VMEM((aug, Q_PER_KV * MOBA_BLOCK), BF16) for _ in range(2)],
        compiler_params=_cparams(3),
        name="moba_prompt",
    )(qt, gate, ka, vta, km)


_KM_BLOCKS = 8


def _block_mean_body(pt_ref, *refs):
    del pt_ref
    pages, o_ref = refs[:-1], refs[-1]
    s = pl.program_id(1)

    @pl.when(s == 0)
    def _():
        o_ref[...] = jnp.zeros(o_ref.shape, F32)

    lane = lax.broadcasted_iota(jnp.int32, o_ref.shape[1:], 2)
    acc = o_ref[0]
    for j in range(_KM_BLOCKS):
        tot = sum(pages[PAGES_PER_BLOCK * j + r][0] for r in range(PAGES_PER_BLOCK))
        mean = jnp.sum(tot, axis=-1, keepdims=True) * (1.0 / MOBA_BLOCK)
        acc = jnp.where(lane == s * _KM_BLOCKS + j, mean, acc)
    o_ref[0] = acc


def _block_means(cache_t, page_table, n_blocks):
    n_seq = page_table.shape[0]
    per_step = _KM_BLOCKS * PAGES_PER_BLOCK
    page_spec = lambda j: pl.BlockSpec(
        (1, KV_HEADS, HEAD_DIM, PAGE_SIZE), lambda b, s, pt, j=j: (pt[b, s * per_step + j], 0, 0, 0))
    return pl.pallas_call(
        _block_mean_body,
        grid_spec=pltpu.PrefetchScalarGridSpec(
            num_scalar_prefetch=1,
            grid=(n_seq, n_blocks // _KM_BLOCKS),
            in_specs=[page_spec(j) for j in range(per_step)],
            out_specs=pl.BlockSpec((1, KV_HEADS, HEAD_DIM, n_blocks), lambda b, s, pt: (b, 0, 0, 0))),
        out_shape=jax.ShapeDtypeStruct((n_seq, KV_HEADS, HEAD_DIM, n_blocks), F32),
        compiler_params=_cparams(2),
        name="block_means",
    )(page_table, *([cache_t] * per_step))


def _sample_topk_body(q_ref, km_ref, idx_ref, *, n_past):
    gs = jnp.concatenate([_dot(q_ref[0, k], km_ref[0, k], HIGHEST) for k in range(KV_HEADS)], axis=0)
    lane = lax.broadcasted_iota(jnp.int32, gs.shape, 1)
    _, picks = _top3_mask(gs, lane, lane < n_past, 1)
    out_lane = lax.broadcasted_iota(jnp.int32, idx_ref.shape[1:], 1)
    out = jnp.zeros(idx_ref.shape[1:], jnp.int32)
    for r, first in enumerate(picks):
        out = jnp.where(out_lane == r, first, out)
    idx_ref[0] = out


def _sample_topk(q, km_t):
    n_seq, _, _, n_past = km_t.shape
    return pl.pallas_call(
        functools.partial(_sample_topk_body, n_past=n_past),
        grid=(n_seq,),
        in_specs=[pl.BlockSpec((1,) + q.shape[1:], lambda b: (b, 0, 0, 0)),
                  pl.BlockSpec((1,) + km_t.shape[1:], lambda b: (b, 0, 0, 0))],
        out_specs=pl.BlockSpec((1, N_HEADS, LANES), lambda b: (b, 0, 0)),
        out_shape=jax.ShapeDtypeStruct((n_seq, N_HEADS, LANES), jnp.int32),
        compiler_params=_cparams(1),
        name="sample_topk",
    )(q, km_t)


_Q_ROWS = 8


def _moba_sample_body(pt_ref, idx_ref, q_ref, kn_ref, vn_ref, *refs, n_past):
    del pt_ref
    n_sel = MOBA_TOPK * PAGES_PER_BLOCK
    k_pages, v_pages, o_ref = refs[:n_sel], refs[n_sel:2 * n_sel], refs[-1]
    b = pl.program_id(0)
    h = pl.program_id(1)
    keys = MOBA_TOPK * MOBA_BLOCK
    q = jnp.broadcast_to(q_ref[0, 0], (_Q_ROWS, HEAD_DIM))
    kt_sel = jnp.concatenate([r[0, 0] for r in k_pages], axis=1)
    vt_sel = jnp.concatenate([r[0, 0] for r in v_pages], axis=1)
    col = lax.broadcasted_iota(jnp.int32, (_Q_ROWS, keys), 1)
    slot = col // MOBA_BLOCK
    base = (b * N_HEADS + h) * MOBA_TOPK
    idx = jnp.zeros((_Q_ROWS, keys), jnp.int32)
    for r in range(MOBA_TOPK):
        idx = jnp.where(slot == r, idx_ref[base + r], idx)
    pos = n_past * MOBA_BLOCK
    dist = (pos - (idx * MOBA_BLOCK + col % MOBA_BLOCK)).astype(F32)
    slope = jnp.exp2(-8.0 * jnp.full((_Q_ROWS, keys), (h + 1).astype(F32), F32) / N_HEADS)
    s = _dot(q, kt_sel, HIGHEST) - slope * dist
    s = jnp.where(idx < n_past, s, NEG_INF)
    s_own = jnp.sum(q * kn_ref[0, 0], axis=-1, keepdims=True)
    m = jnp.maximum(jnp.max(s, axis=-1, keepdims=True), s_own)
    p = jnp.exp(s - m)
    p_own = jnp.exp(s_own - m)
    l = jnp.sum(p, axis=-1, keepdims=True) + p_own
    out = (_dot_nt(p, vt_sel, HIGHEST) + p_own * vn_ref[0, 0]) / l
    o_ref[0, 0] = out[0:1]


def _moba_sample(q, k_new, v_new, cache_kt, cache_vt, page_table, idx_flat, n_past):
    n_seq = page_table.shape[0]

    def page_spec(j):
        r, pg = divmod(j, PAGES_PER_BLOCK)

        def index(b, h, pt, idx):
            blk = idx[(b * N_HEADS + h) * MOBA_TOPK + r]
            return (pt[b, blk * PAGES_PER_BLOCK + pg], h // Q_PER_KV, 0, 0)

        return pl.BlockSpec((1, 1, HEAD_DIM, PAGE_SIZE), index)

    n_sel = MOBA_TOPK * PAGES_PER_BLOCK
    row = (1, 1, 1, HEAD_DIM)
    new_spec = pl.BlockSpec(row, lambda b, h, pt, idx: (b, h // Q_PER_KV, 0, 0))
    head_spec = pl.BlockSpec(row, lambda b, h, pt, idx: (b, h, 0, 0))
    return pl.pallas_call(
        functools.partial(_moba_sample_body, n_past=n_past),
        grid_spec=pltpu.PrefetchScalarGridSpec(
            num_scalar_prefetch=2,
            grid=(n_seq, N_HEADS),
            in_specs=[head_spec, new_spec, new_spec] + [page_spec(j) for j in range(n_sel)] * 2,
            out_specs=head_spec),
        out_shape=jax.ShapeDtypeStruct((n_seq, N_HEADS, 1, HEAD_DIM), F32),
        compiler_params=_cparams(2),
        name="moba_sample",
    )(page_table, idx_flat, q, k_new, v_new, *([cache_kt] * n_sel), *([cache_vt] * n_sel))


def _out_b_body(a_ref, *refs, prec, gated):
    if gated:
        x1_ref, w_ref, g_ref, y_ref = refs
        a = a_ref[...]
    else:
        gate_ref, x1_ref, w_ref, g_ref, y_ref = refs
        gate = gate_ref[...]
        a = (a_ref[...] * (gate * _sigmoid(gate))).astype(w_ref.dtype)
    o = _dot(a, w_ref[...], prec)
    y_ref[...] = x1_ref[...] + _rms_scale(o) * g_ref[...]


def _out_b(att, gate, x1, w, g_post, tm, prec):
    m, d = x1.shape
    tile = pl.BlockSpec((tm, d), lambda i: (i, 0))
    gated = gate is None
    acts = [att, x1] if gated else [att, gate, x1]
    return pl.pallas_call(
        functools.partial(_out_b_body, prec=prec, gated=gated),
        grid=(m // tm,),
        in_specs=[tile] * len(acts) + [_full((d, d)), _full((1, d))],
        out_specs=tile,
        out_shape=jax.ShapeDtypeStruct((m, d), F32),
        compiler_params=_cparams(1),
        name="out_b",
    )(*acts, w, g_post.reshape(1, d))


def _prompt_trunk(x, p):
    bsz, seqlen, d = x.shape
    m = bsz * seqlen
    nb = seqlen // MOBA_BLOCK
    nc = seqlen // S5_CHUNK
    rt = min(nc, 256)
    x3 = x.reshape(bsz, nc, S5_CHUNK * d)
    grid = (bsz, nc // rt, S5_CHUNK)
    nat_spec = pl.BlockSpec((1, rt, d), lambda b, r, t: (b, r, t))
    tmaj_spec = pl.BlockSpec((1, 1, rt, d), lambda b, r, t: (t, b, r, 0))
    u5, gate_a = _norm_matmul(
        x3, p['g_pre_a'], p['w_in_a'].astype(BF16), None, grid, nat_spec, [tmaj_spec, nat_spec],
        [jax.ShapeDtypeStruct((S5_CHUNK, bsz, nc, d), BF16), jax.ShapeDtypeStruct(x3.shape, F32)])
    ar, ai, bb_re, bb_im = _s5_discretize(p['lambda_re'], p['lambda_im'], p['log_dt'], p['b_re'], p['b_im'])
    ops = _s5_prompt_operators(ar, ai, bb_re, bb_im, p['c_re'], p['c_im'], p['d_skip'])
    y5, h_fin = _s5_prompt(u5, ops, min(nc, 64))
    h_fin = h_fin.reshape(d // LANES, bsz, 2, S5_LANE_GROUPS, SSM_STATE).transpose(2, 1, 0, 3, 4)
    h_fin = h_fin.reshape(2, 1, bsz, SSM_GROUPS, SSM_STATE)
    x1 = _post_a(y5, gate_a, x3, p['w_glu'].astype(BF16), p['b_glu'], p['w_out_a'].astype(BF16),
                 p['g_post_a'], None, grid, tmaj_spec, nat_spec).reshape(m, d)
    k, v, gate_b, qt, ka, vta, km = _pre_b(x1, p['g_kv'], p['w_kv'].astype(BF16), p['g_pre_b'],
                                           p['w_in_b'].astype(BF16), MOBA_BLOCK, None, (bsz, nb))
    km = km.reshape(bsz, nb, KV_HEADS, HEAD_DIM).transpose(0, 2, 1, 3)
    att = _moba_prompt(qt, gate_b.reshape(bsz, seqlen, d), ka, vta, km)
    y_out = _out_b(att.reshape(m, d), None, x1, p['w_out_b'].astype(BF16), p['g_post_b'], 512, None)
    to_out = lambda t: t.reshape(bsz, KV_HEADS, HEAD_DIM, seqlen).transpose(0, 3, 1, 2)
    return (y_out.reshape(bsz, seqlen, d), h_fin[0], h_fin[1], to_out(k), to_out(v))


def _sample_trunk(x, h0_re, h0_im, cache_k, cache_v, page_table, p):
    n_seq, seqlen, d = x.shape
    assert seqlen == 1
    g, pch = SSM_GROUPS, SSM_GROUP
    n_past = page_table.shape[1] // PAGES_PER_BLOCK
    x2 = x.reshape(n_seq, d)
    whole = pl.BlockSpec((n_seq, d), lambda i: (0, 0))
    u, gate_a = _norm_matmul(x2, p['g_pre_a'], p['w_in_a'], HIGHEST, (1,), whole, [whole, whole],
                             [jax.ShapeDtypeStruct((n_seq, d), F32)] * 2)
    ar, ai, bb_re, bb_im = _s5_discretize(p['lambda_re'], p['lambda_im'], p['log_dt'], p['b_re'], p['b_im'])
    y_g, hr, hi = _s5_step(u.reshape(n_seq, g, pch).transpose(1, 0, 2),
                           h0_re.transpose(1, 0, 2), h0_im.transpose(1, 0, 2),
                           ar, ai, bb_re, bb_im, p['c_re'], p['c_im'], p['d_skip'])
    y = y_g.transpose(1, 0, 2).reshape(n_seq, d)
    x1 = _post_a(y, gate_a, x2, p['w_glu'], p['b_glu'], p['w_out_a'], p['g_post_a'], HIGHEST, (1,), whole, whole)
    k, v, gate_b, q = _pre_b(x1, p['g_kv'], p['w_kv'], p['g_pre_b'], p['w_in_b'], n_seq, HIGHEST)
    cache_kt = cache_k.transpose(0, 2, 3, 1)
    cache_vt = cache_v.transpose(0, 2, 3, 1)
    km_t = _block_means(cache_kt, page_table, n_past)
    idx = _sample_topk(q.reshape(n_seq, KV_HEADS, Q_PER_KV, HEAD_DIM), km_t)[:, :, :MOBA_TOPK].reshape(-1)
    att = _moba_sample(q.reshape(n_seq, N_HEADS, 1, HEAD_DIM), k.reshape(n_seq, KV_HEADS, 1, HEAD_DIM),
                       v.reshape(n_seq, KV_HEADS, 1, HEAD_DIM), cache_kt, cache_vt, page_table, idx, n_past)
    y_out = _out_b(att.reshape(n_seq, d), gate_b, x1, p['w_out_b'], p['g_post_b'], n_seq, HIGHEST)
    return (y_out.reshape(n_seq, 1, d), hr.transpose(1, 0, 2)[None], hi.transpose(1, 0, 2)[None],
            k.reshape(n_seq, 1, KV_HEADS, HEAD_DIM), v.reshape(n_seq, 1, KV_HEADS, HEAD_DIM))


def kernel(x_prompt, x_sample, state_ssm_re, state_ssm_im, cache_k, cache_v, page_table, g_pre_a, w_in_a, lambda_re, lambda_im, log_dt, b_re, b_im, c_re, c_im, d_skip, w_glu, b_glu, w_out_a, g_post_a, g_kv, w_kv, g_pre_b, w_in_b, w_out_b, g_post_b):
    p = {'g_pre_a': g_pre_a[0], 'w_in_a': w_in_a[0], 'lambda_re': lambda_re[0], 'lambda_im': lambda_im[0],
         'log_dt': log_dt[0], 'b_re': b_re[0], 'b_im': b_im[0], 'c_re': c_re[0], 'c_im': c_im[0],
         'd_skip': d_skip[0], 'w_glu': w_glu[0], 'b_glu': b_glu[0], 'w_out_a': w_out_a[0],
         'g_post_a': g_post_a[0], 'g_kv': g_kv, 'w_kv': w_kv, 'g_pre_b': g_pre_b[0],
         'w_in_b': w_in_b[0], 'w_out_b': w_out_b[0], 'g_post_b': g_post_b[0]}
    y_p, re_p, im_p, k_p, v_p = _prompt_trunk(x_prompt, p)
    y_s, re_s, im_s, k_s, v_s = _sample_trunk(x_sample, state_ssm_re[0], state_ssm_im[0],
                                              cache_k, cache_v, page_table, p)
    return (y_p, y_s, re_p, im_p, k_p, v_p, re_s, im_s, k_s, v_s)
```

```python
import functools
import math

import jax
import jax.numpy as jnp
from jax import lax
from jax.experimental import pallas as pl
from jax.experimental.pallas import tpu as pltpu

F32 = jnp.float32
BF16 = jnp.bfloat16
HIGHEST = lax.Precision.HIGHEST

D_MODEL = 1024
SSM_GROUP = 16
SSM_GROUPS = D_MODEL // SSM_GROUP
SSM_STATE = 64
HEAD_DIM = 64
N_HEADS = D_MODEL // HEAD_DIM
KV_HEADS = 4
Q_PER_KV = N_HEADS // KV_HEADS
KV_WIDTH = KV_HEADS * HEAD_DIM
MOBA_BLOCK = 256
MOBA_TOPK = 3
PAGE_SIZE = 128
PAGES_PER_BLOCK = MOBA_BLOCK // PAGE_SIZE
EPS = 1e-6
NEG_INF = -1e30
REMOVED = -3e38
SCALE = HEAD_DIM ** -0.5
S5_CHUNK = 16
LANES = 128
VMEM_LIMIT = 56 * 1024 * 1024


def _cparams(n_grid):
    return pltpu.CompilerParams(dimension_semantics=("arbitrary",) * n_grid,
                                vmem_limit_bytes=VMEM_LIMIT)


def _dot(a, b, prec=None):
    return jnp.dot(a, b, preferred_element_type=F32, precision=prec)


def _dot_nt(a, b, prec=None):
    return lax.dot_general(a, b, (((1,), (1,)), ((), ())),
                           preferred_element_type=F32, precision=prec)


def _rms_scale(x):
    return x * lax.rsqrt(jnp.mean(x * x, axis=-1, keepdims=True) + EPS)


def _sigmoid(x):
    return 1.0 / (1.0 + jnp.exp(-x))


def _full(shape):
    zeros = (0,) * len(shape)
    return pl.BlockSpec(shape, lambda *_: zeros)


def _tile(ref):
    x = ref[...]
    return x.reshape(x.shape[-2:])


def _norm_matmul_body(x_ref, g_ref, w_ref, *out_refs, prec):
    h = (_rms_scale(_tile(x_ref)) * g_ref[...]).astype(w_ref.dtype)
    off = 0
    for o_ref in out_refs:
        n = o_ref.shape[-1]
        o_ref[...] = _dot(h, w_ref[:, off:off + n], prec).astype(o_ref.dtype).reshape(o_ref.shape)
        off += n


def _norm_matmul(x, g, w, prec, grid, x_spec, out_specs, out_shape):
    d, n = w.shape
    return pl.pallas_call(
        functools.partial(_norm_matmul_body, prec=prec),
        grid=grid,
        in_specs=[x_spec, _full((1, d)), _full((d, n))],
        out_specs=out_specs,
        out_shape=out_shape,
        compiler_params=_cparams(len(grid)),
        name="norm_matmul",
    )(x, g.reshape(1, d), w)


S5_LANE_GROUPS = LANES // SSM_GROUP
S5_STATE_LANES = S5_LANE_GROUPS * SSM_STATE
S5_PAIR = 2 * LANES
S5_MC_ROWS = LANES


def _iota2(shape, dim):
    return lax.broadcasted_iota(jnp.int32, shape, dim)


def _expand_block_diagonal(mc_ref, sc_ref, rc_ref, m_s, s_s, r_s):
    p, n, lg = SSM_GROUP, SSM_STATE, S5_LANE_GROUPS
    sl = S5_STATE_LANES
    grp = lambda idx, per: (idx // per) % lg
    rows, cols = s_s.shape
    i, j = _iota2((2 * n, cols), 0), _iota2((2 * n, cols), 1)
    spread = (i == (j // sl) * n + j % n).astype(F32).astype(BF16)
    blk = 256
    for r0 in range(0, rows, blk):
        full = _dot(sc_ref[0, r0:r0 + blk, :], spread)
        keep = grp(_iota2((blk, cols), 0) + r0, p) == grp(_iota2((blk, cols), 1), n)
        s_s[r0:r0 + blk, :] = jnp.where(keep, full, 0.0).astype(BF16)
    rows, cols = r_s.shape
    i, j = _iota2((rows, 2 * n), 0), _iota2((rows, 2 * n), 1)
    spread = (j == (i // sl) * n + i % n).astype(F32).astype(BF16)
    for c0 in range(0, cols, blk):
        full = _dot(spread, rc_ref[0, :, c0:c0 + blk])
        keep = grp(_iota2((rows, blk), 0), n) == grp(_iota2((rows, blk), 1) + c0, p)
        r_s[:, c0:c0 + blk] = jnp.where(keep, full, 0.0).astype(BF16)
    n_delta, rows, cols = m_s.shape
    i, j = _iota2((rows, S5_MC_ROWS), 0), _iota2((rows, S5_MC_ROWS), 1)
    spread = (j == (i // LANES) * p + i % p).astype(F32).astype(BF16)
    keep = grp(_iota2((rows, cols), 0), p) == grp(_iota2((rows, cols), 1), p)
    for dl in range(n_delta):
        m_s[dl] = jnp.where(keep, _dot(spread, mc_ref[0, dl]), 0.0).astype(BF16)


def _s5_prompt_body(u_ref, mc_ref, sc_ref, rc_ref, a_ref, d_ref, y_ref, hfin_ref,
                    m_s, s_s, r_s, e_s, hin_s, h_s):
    bsz, tokens, _ = u_ref.shape
    t_chunk = S5_CHUNK
    rc = tokens // t_chunk
    sl = S5_STATE_LANES

    @pl.when(pl.program_id(1) == 0)
    def _():
        h_s[...] = jnp.zeros(h_s.shape, F32)
        _expand_block_diagonal(mc_ref, sc_ref, rc_ref, m_s, s_s, r_s)

    token = lambda ref, b, s: ref.at[b, pl.ds(s, rc, stride=t_chunk), :]
    xs = [jnp.concatenate([token(u_ref, b, s)[...] for b in range(bsz)], axis=0) for s in range(t_chunk)]
    xcat = jnp.concatenate([x.astype(BF16) for x in xs], axis=1)
    half = sl // LANES
    cols = lambda j: slice(j * LANES, (j + 1) * LANES)
    e = _dot(xcat, s_s[...])
    for j in range(2 * half):
        e_s[j] = e[:, cols(j)]
    ar = [jnp.broadcast_to(a_ref[0, 0:1, cols(j)], (bsz, LANES)) for j in range(half)]
    ai = [jnp.broadcast_to(a_ref[0, 1:2, cols(j)], (bsz, LANES)) for j in range(half)]

    def step(c, carry):
        hr, hi = carry
        at_c = pl.ds(c, bsz, stride=rc)
        new_r, new_i = [], []
        for j in range(half):
            hin_s[j, at_c, :] = hr[j]
            hin_s[half + j, at_c, :] = hi[j]
            new_r.append(ar[j] * hr[j] - ai[j] * hi[j] + e_s[j, at_c, :])
            new_i.append(ar[j] * hi[j] + ai[j] * hr[j] + e_s[half + j, at_c, :])
        return tuple(new_r), tuple(new_i)

    carry0 = (tuple(h_s[0, j] for j in range(half)), tuple(h_s[1, j] for j in range(half)))
    hr, hi = lax.fori_loop(0, rc, step, carry0)
    for j in range(half):
        h_s[0, j] = hr[j]
        h_s[1, j] = hi[j]
        hfin_ref[0, :, cols(j)] = hr[j]
        hfin_ref[0, :, cols(half + j)] = hi[j]
    hin = jnp.concatenate([hin_s[j] for j in range(2 * half)], axis=1).astype(BF16)
    d_row = d_ref[0]
    for i in range(t_chunk // 2):
        acc = _dot(hin, r_s[:, i * S5_PAIR:(i + 1) * S5_PAIR])
        for j in range(i + 1):
            acc += _dot(xcat[:, j * S5_PAIR:(j + 1) * S5_PAIR], m_s[i - j])
        for tt in range(2):
            t = 2 * i + tt
            y_t = acc[:, tt * LANES:(tt + 1) * LANES] + d_row * xs[t]
            for b in range(bsz):
                token(y_ref, b, t)[...] = y_t[b * rc:(b + 1) * rc]


def _s5_prompt(u, ops, rc):
    bsz, seqlen, d = u.shape
    n_lb = d // LANES
    mc, sc, rc_op, a16, dl = ops
    sl = S5_STATE_LANES
    t_chunk = S5_CHUNK
    tokens = rc * t_chunk
    act = pl.BlockSpec((bsz, tokens, LANES), lambda lb, ct: (0, ct, lb))
    per_lb = lambda shape: pl.BlockSpec((1,) + shape, lambda lb, ct: (lb,) + (0,) * len(shape))
    return pl.pallas_call(
        _s5_prompt_body,
        grid=(n_lb, seqlen // tokens),
        in_specs=[act, per_lb(mc.shape[1:]), per_lb(sc.shape[1:]), per_lb(rc_op.shape[1:]),
                  per_lb((2, sl)), per_lb((1, LANES))],
        out_specs=[act, per_lb((bsz, 2 * sl))],
        out_shape=[jax.ShapeDtypeStruct(u.shape, F32),
                   jax.ShapeDtypeStruct((n_lb, bsz, 2 * sl), F32)],
        scratch_shapes=[pltpu.VMEM((t_chunk // 2, S5_PAIR, S5_PAIR), BF16),
                        pltpu.VMEM((t_chunk * LANES, 2 * sl), BF16),
                        pltpu.VMEM((2 * sl, t_chunk * LANES), BF16),
                        pltpu.VMEM((2 * sl // LANES, bsz * rc, LANES), F32),
                        pltpu.VMEM((2 * sl // LANES, bsz * rc, LANES), F32),
                        pltpu.VMEM((2, sl // LANES, bsz, LANES), F32)],
        compiler_params=_cparams(2),
        name="s5_prompt",
    )(u, mc, sc, rc_op, a16, dl)


def _s5_discretize(lam_re, lam_im, log_dt, b_re, b_im):
    dt = jnp.exp(log_dt)[:, None]
    mag = jnp.exp(lam_re * dt)
    ar = mag * jnp.cos(lam_im * dt)
    ai = mag * jnp.sin(lam_im * dt)
    den = lam_re * lam_re + lam_im * lam_im
    nr = ar - 1.0
    coef_re = (nr * lam_re + ai * lam_im) / den
    coef_im = (ai * lam_re - nr * lam_im) / den
    bb_re = coef_re[..., None] * b_re - coef_im[..., None] * b_im
    bb_im = coef_re[..., None] * b_im + coef_im[..., None] * b_re
    return ar, ai, bb_re, bb_im


def _s5_prompt_operators(ar, ai, bb_re, bb_im, c_re, c_im, d_skip):
    t = S5_CHUNK
    g, n = ar.shape
    p = SSM_GROUP
    lg = S5_LANE_GROUPS
    n_lb = g // lg
    pr, pi = [jnp.ones_like(ar)], [jnp.zeros_like(ar)]
    for _ in range(t):
        pr, pi = pr + [pr[-1] * ar - pi[-1] * ai], pi + [pr[-1] * ai + pi[-1] * ar]
    pr = jnp.stack(pr)
    pi = jnp.stack(pi)
    cp_re = c_re[None] * pr[:, :, None, :] - c_im[None] * pi[:, :, None, :]
    cp_im = c_re[None] * pi[:, :, None, :] + c_im[None] * pr[:, :, None, :]
    k = (jnp.einsum('tgpn,gnq->gtpq', cp_re[:t], bb_re, precision=HIGHEST)
         - jnp.einsum('tgpn,gnq->gtpq', cp_im[:t], bb_im, precision=HIGHEST))
    k_lag = lambda lag: k[:, lag] if lag >= 0 else jnp.zeros_like(k[:, 0])
    kt = jnp.stack([jnp.stack([jnp.stack([k_lag(2 * dl + tt - ss) for tt in range(2)], axis=1)
                               for ss in range(2)], axis=1) for dl in range(t // 2)], axis=1)
    kt = kt.reshape(n_lb, lg, t // 2, 2, 2, p, p)
    m = kt.transpose(0, 2, 3, 6, 4, 1, 5).reshape(n_lb, t // 2, 2 * p, 2 * lg * p)
    m = jnp.pad(m, ((0, 0), (0, 0), (0, S5_MC_ROWS - 2 * p), (0, 0)))
    pr_rev = jnp.stack([pr[t - 1 - s] for s in range(t)])
    pi_rev = jnp.stack([pi[t - 1 - s] for s in range(t)])
    s_re = pr_rev[:, :, :, None] * bb_re[None] - pi_rev[:, :, :, None] * bb_im[None]
    s_im = pr_rev[:, :, :, None] * bb_im[None] + pi_rev[:, :, :, None] * bb_re[None]
    s_ri = jnp.stack([s_re, s_im]).reshape(2, t, n_lb, lg, n, p)
    s = s_ri.transpose(2, 1, 3, 5, 0, 4).reshape(n_lb, t * lg * p, 2 * n)
    r_ri = jnp.stack([cp_re[1:], -cp_im[1:]]).reshape(2, t, n_lb, lg, p, n)
    r = r_ri.transpose(2, 0, 5, 1, 3, 4).reshape(n_lb, 2 * n, t * lg * p)
    a16 = jnp.stack([pr[t].reshape(n_lb, lg * n), pi[t].reshape(n_lb, lg * n)], axis=1)
    return (m.astype(BF16), s.astype(BF16), r.astype(BF16), a16, d_skip.reshape(n_lb, 1, lg * p))


def _s5_step_body(u_ref, h0r_ref, h0i_ref, ar_ref, ai_ref, bbr_ref, bbi_ref, cr_ref, ci_ref,
                  d_ref, y_ref, hr_ref, hi_ref):
    u = u_ref[0]
    ar, ai = ar_ref[0], ai_ref[0]
    h0r, h0i = h0r_ref[0], h0i_ref[0]
    hr = _dot_nt(u, bbr_ref[0], HIGHEST) + ar * h0r - ai * h0i
    hi = _dot_nt(u, bbi_ref[0], HIGHEST) + ar * h0i + ai * h0r
    hr_ref[0] = hr
    hi_ref[0] = hi
    y_ref[0] = (_dot_nt(hr, cr_ref[0], HIGHEST) - _dot_nt(hi, ci_ref[0], HIGHEST)
                + d_ref[0] * u)


def _s5_step(u_g, h0r, h0i, ar, ai, bb_re, bb_im, c_re, c_im, d_skip):
    g, b, p = u_g.shape
    n = ar.shape[-1]
    per_group = lambda shape: pl.BlockSpec((1,) + shape, lambda i: (i, 0, 0))
    return pl.pallas_call(
        _s5_step_body,
        grid=(g,),
        in_specs=[per_group((b, p)), per_group((b, n)), per_group((b, n)),
                  per_group((1, n)), per_group((1, n)),
                  per_group((n, p)), per_group((n, p)),
                  per_group((p, n)), per_group((p, n)), per_group((1, p))],
        out_specs=[per_group((b, p)), per_group((b, n)), per_group((b, n))],
        out_shape=[jax.ShapeDtypeStruct((g, b, p), F32),
                   jax.ShapeDtypeStruct((g, b, n), F32),
                   jax.ShapeDtypeStruct((g, b, n), F32)],
        compiler_params=_cparams(1),
        name="s5_step",
    )(u_g, h0r, h0i, ar.reshape(g, 1, n), ai.reshape(g, 1, n), bb_re, bb_im, c_re, c_im,
      d_skip.reshape(g, 1, p))


def _post_a_body(y_ref, ga_ref, x_ref, wglu_ref, bglu_ref, wout_ref, gpost_ref, x1_ref, *, prec):
    y = _tile(y_ref)
    wdt = wglu_ref.dtype
    gy = 0.5 * y * (1.0 + lax.erf(y * math.sqrt(0.5)))
    z = _dot(gy.astype(wdt), wglu_ref[...], prec) + bglu_ref[...]
    gate = _tile(ga_ref)
    t = (gy * _sigmoid(z)) * (gate * _sigmoid(gate))
    o = _dot(t.astype(wdt), wout_ref[...], prec)
    x1_ref[...] = (_tile(x_ref) + _rms_scale(o) * gpost_ref[...]).reshape(x1_ref.shape)


def _post_a(y, gate, x, w_glu, b_glu, w_out, g_post, prec, grid, y_spec, x_spec):
    d = w_glu.shape[0]
    return pl.pallas_call(
        functools.partial(_post_a_body, prec=prec),
        grid=grid,
        in_specs=[y_spec, x_spec, x_spec, _full((d, d)), _full((1, d)), _full((d, d)), _full((1, d))],
        out_specs=x_spec,
        out_shape=jax.ShapeDtypeStruct(x.shape, F32),
        compiler_params=_cparams(len(grid)),
        name="post_a",
    )(y, gate, x, w_glu, b_glu.reshape(1, d), w_out, g_post.reshape(1, d))


ALIBI_ROWS = 16
FEAT_OFF = HEAD_DIM
V_ROWS = HEAD_DIM + 16


def _aug_width(nb):
    nbp = -(-nb // 16) * 16
    return -(-(FEAT_OFF + ALIBI_ROWS + nbp) // LANES) * LANES, nbp


def _pre_b_body(x1_ref, gkv_ref, wkv_ref, gb_ref, winb_ref, k_ref, v_ref, gate_ref, *rest, prec, nb):
    xn = _rms_scale(x1_ref[...])
    wdt = wkv_ref.dtype
    hk = (xn * gkv_ref[...]).astype(wdt)
    k = _dot(hk, wkv_ref[:, :KV_WIDTH], prec)
    v = _dot(hk, wkv_ref[:, KV_WIDTH:], prec)
    hq = (xn * gb_ref[...]).astype(wdt)
    q = _dot(hq, winb_ref[:, :D_MODEL], prec) * SCALE
    gate_ref[...] = _dot(hq, winb_ref[:, D_MODEL:], prec)
    if nb is None:
        q_ref, = rest
        q_ref[...] = q
        k_ref[...] = k
        v_ref[...] = v
        return
    qt_ref, ka_ref, vta_ref, km_ref = rest
    tm = q.shape[0]
    qt_ref[0, 0] = q.T.astype(BF16)
    km_ref[0] = jnp.mean(k, axis=0, keepdims=True)
    n = pl.program_id(0) % nb
    width = ka_ref.shape[-1] - FEAT_OFF
    lane = lax.broadcasted_iota(jnp.int32, (tm, width), 1)
    rowi = lax.broadcasted_iota(jnp.int32, (tm, width), 0)
    feat = jnp.where(lane < 3, n, jnp.where(lane < 6, rowi, jnp.where(lane < 9, 1, 0)))
    feat = jnp.where(lane - ALIBI_ROWS == n, 1, feat).astype(F32).astype(BF16)
    kb = k.astype(BF16)
    vt = v.T
    k_ref[0] = k.T
    v_ref[0] = vt
    ones_row = (lax.broadcasted_iota(jnp.int32, (V_ROWS - HEAD_DIM, tm), 0) == 0).astype(F32).astype(BF16)
    for h in range(KV_HEADS):
        ka_ref[0, h, :, :FEAT_OFF] = kb[:, h * HEAD_DIM:(h + 1) * HEAD_DIM]
        ka_ref[0, h, :, FEAT_OFF:] = feat
        vta_ref[0, 0, h * V_ROWS:h * V_ROWS + HEAD_DIM, :] = vt[h * HEAD_DIM:(h + 1) * HEAD_DIM].astype(BF16)
        vta_ref[0, 0, h * V_ROWS + HEAD_DIM:(h + 1) * V_ROWS, :] = ones_row


def _pre_b(x1, g_kv, w_kv, g_pre_b, w_in_b, tm, prec, prompt_blocks=None):
    m, d = x1.shape
    row = lambda wd: pl.BlockSpec((tm, wd), lambda i: (i, 0))
    nb = None
    if prompt_blocks is None:
        kv_spec = row(KV_WIDTH)
        kv_shape = jax.ShapeDtypeStruct((m, KV_WIDTH), F32)
        out_specs = [kv_spec, kv_spec, row(d), row(d)]
        out_shape = [kv_shape, kv_shape, jax.ShapeDtypeStruct((m, d), F32), jax.ShapeDtypeStruct((m, d), F32)]
    else:
        bsz, nb = prompt_blocks
        assert tm == MOBA_BLOCK and m == bsz * nb * tm
        aug, _ = _aug_width(nb)
        kv_spec = pl.BlockSpec((1, KV_WIDTH, tm), lambda i: (i // nb, 0, i % nb))
        kv_shape = jax.ShapeDtypeStruct((bsz, KV_WIDTH, nb * tm), F32)
        out_specs = [kv_spec, kv_spec, row(d)]
        out_shape = [kv_shape, kv_shape, jax.ShapeDtypeStruct((m, d), F32)]
        out_specs += [pl.BlockSpec((1, 1, d, tm), lambda i: (i // nb, i % nb, 0, 0)),
                      pl.BlockSpec((1, KV_HEADS, tm, aug), lambda i: (i // nb, 0, i % nb, 0)),
                      pl.BlockSpec((1, 1, KV_HEADS * V_ROWS, tm), lambda i: (i // nb, i % nb, 0, 0)),
                      pl.BlockSpec((1, 1, KV_WIDTH), lambda i: (i, 0, 0))]
        out_shape += [jax.ShapeDtypeStruct((bsz, nb, d, tm), BF16),
                      jax.ShapeDtypeStruct((bsz, KV_HEADS, nb * tm, aug), BF16),
                      jax.ShapeDtypeStruct((bsz, nb, KV_HEADS * V_ROWS, tm), BF16),
                      jax.ShapeDtypeStruct((bsz * nb, 1, KV_WIDTH), F32)]
    return pl.pallas_call(
        functools.partial(_pre_b_body, prec=prec, nb=nb),
        grid=(m // tm,),
        in_specs=[row(d), _full((1, d)), _full((d, 2 * KV_WIDTH)), _full((1, d)), _full((d, 2 * d))],
        out_specs=out_specs,
        out_shape=out_shape,
        compiler_params=_cparams(1),
        name="pre_b",
    )(x1, g_kv.reshape(1, d), w_kv, g_pre_b.reshape(1, d), w_in_b)


def _top3_mask(gs, idx, valid, axis):
    n = gs.shape[axis]
    gs = jnp.where(valid, gs, NEG_INF)
    sel = jnp.zeros(gs.shape, F32)
    picks = []
    for _ in range(MOBA_TOPK):
        mx = jnp.max(gs, axis=axis, keepdims=True)
        first = jnp.min(jnp.where(gs == mx, idx, n), axis=axis, keepdims=True)
        pick = idx == first
        sel = jnp.where(pick & valid, 1.0, sel)
        gs = jnp.where(pick, REMOVED, gs)
        picks.append(first)
    return sel, picks


def _split3(x):
    hi = x.astype(BF16).astype(F32)
    mid = (x - hi).astype(BF16).astype(F32)
    lo = (x - hi - mid).astype(BF16).astype(F32)
    return [hi, mid, lo]


def _moba_prompt_body(qt_ref, g_ref, ka_ref, vta_ref, km_ref, o_ref, qa_s, qd_s):
    h = pl.program_id(1)
    c = pl.program_id(2)
    blk = MOBA_BLOCK
    nb = km_ref.shape[2]
    aug, nbp = _aug_width(nb)
    km = km_ref[0, 0]
    row = lax.broadcasted_iota(jnp.int32, (nb, blk), 0)
    frow = lax.broadcasted_iota(jnp.int32, (ALIBI_ROWS, blk), 0)
    pos_q = (c * blk + lax.broadcasted_iota(jnp.int32, (1, blk), 1)).astype(F32)
    for g in range(Q_PER_KV):
        lanes = slice(g * blk, (g + 1) * blk)
        qg = qt_ref[0, 0, g * HEAD_DIM:(g + 1) * HEAD_DIM, :]
        sel, _ = _top3_mask(_dot(km, qg.astype(F32), HIGHEST), row, row < c, 0)
        bias = jnp.where(sel > 0.0, 0.0, NEG_INF)
        if nbp > nb:
            bias = jnp.concatenate([bias, jnp.zeros((nbp - nb, blk), F32)], axis=0)
        head1 = (h * Q_PER_KV + g + 1).astype(F32)
        slope = jnp.exp2(-8.0 * jnp.full((1, blk), head1, F32) / N_HEADS)
        terms = _split3(slope * float(blk)) + _split3(slope) + _split3(-slope * pos_q)
        feat = jnp.zeros((ALIBI_ROWS, blk), F32)
        for r, term in enumerate(terms):
            feat = jnp.where(frow == r, term, feat)
        for ref, b_rows in ((qa_s, bias.astype(BF16)), (qd_s, jnp.zeros((nbp, blk), BF16))):
            ref[:FEAT_OFF, lanes] = qg
            ref[FEAT_OFF:FEAT_OFF + ALIBI_ROWS, lanes] = feat.astype(BF16)
            ref[FEAT_OFF + ALIBI_ROWS:FEAT_OFF + ALIBI_ROWS + nbp, lanes] = b_rows
            if FEAT_OFF + ALIBI_ROWS + nbp < aug:
                ref[FEAT_OFF + ALIBI_ROWS + nbp:, lanes] = jnp.zeros((aug - FEAT_OFF - ALIBI_ROWS - nbp, blk), BF16)

    def keys(n):
        return ka_ref[0, 0, pl.ds(pl.multiple_of(n * blk, blk), blk), :]

    s = _dot(keys(c), qd_s[...])
    key = lax.broadcasted_iota(jnp.int32, s.shape, 0)
    qry = lax.broadcasted_iota(jnp.int32, s.shape, 1) & (blk - 1)
    s = jnp.where(key <= qry, s, NEG_INF)
    m = jnp.max(s, axis=0, keepdims=True)
    acc = _dot(vta_ref[0, c], jnp.exp(s - m).astype(BF16))

    def block_pair(i, carry):
        m, acc = carry
        n0, n1 = 2 * i, 2 * i + 1
        s0 = _dot(keys(n0), qa_s[...])
        s1 = _dot(keys(n1), qa_s[...])
        m2 = jnp.maximum(m, jnp.maximum(jnp.max(s0, axis=0, keepdims=True), jnp.max(s1, axis=0, keepdims=True)))
        p0 = jnp.exp(s0 - m2).astype(BF16)
        p1 = jnp.exp(s1 - m2).astype(BF16)
        return m2, jnp.exp(m - m2) * acc + _dot(vta_ref[0, n0], p0) + _dot(vta_ref[0, n1], p1)

    m, acc = lax.fori_loop(0, (c + 1) // 2, block_pair, (m, acc))
    out_t = acc[:HEAD_DIM] / acc[HEAD_DIM:HEAD_DIM + 1]
    att = jnp.concatenate([out_t[:, g * blk:(g + 1) * blk] for g in range(Q_PER_KV)], axis=0).T
    gate = g_ref[0]
    o_ref[0] = (att * (gate * _sigmoid(gate))).astype(o_ref.dtype)


def _moba_prompt(qt, gate, ka, vta, km):
    bsz, seqlen, d = gate.shape
    nb = seqlen // MOBA_BLOCK
    width = Q_PER_KV * HEAD_DIM
    aug, _ = _aug_width(nb)
    tile = pl.BlockSpec((1, MOBA_BLOCK, width), lambda b, h, c: (b, c, h))
    return pl.pallas_call(
        _moba_prompt_body,
        grid=(bsz, KV_HEADS, nb),
        in_specs=[pl.BlockSpec((1, 1, width, MOBA_BLOCK), lambda b, h, c: (b, c, h, 0)),
                  tile,
                  pl.BlockSpec((1, 1, seqlen, aug), lambda b, h, c: (b, h, 0, 0)),
                  pl.BlockSpec((1, nb, V_ROWS, MOBA_BLOCK), lambda b, h, c: (b, 0, h, 0)),
                  pl.BlockSpec((1, 1, nb, HEAD_DIM), lambda b, h, c: (b, h, 0, 0))],
        out_specs=tile,
        out_shape=jax.ShapeDtypeStruct((bsz, seqlen, d), BF16),
        scratch_shapes=[pltpu.VMEM((aug, Q_PER_KV * MOBA_BLOCK), BF16) for _ in range(2)],
        compiler_params=_cparams(3),
        name="moba_prompt",
    )(qt, gate, ka, vta, km)


_KM_BLOCKS = 8


def _block_mean_body(pt_ref, *refs):
    del pt_ref
    pages, o_ref = refs[:-1], refs[-1]
    s = pl.program_id(1)

    @pl.when(s == 0)
    def _():
        o_ref[...] = jnp.zeros(o_ref.shape, F32)

    lane = lax.broadcasted_iota(jnp.int32, o_ref.shape[1:], 2)
    acc = o_ref[0]
    for j in range(_KM_BLOCKS):
        tot = sum(pages[PAGES_PER_BLOCK * j + r][0] for r in range(PAGES_PER_BLOCK))
        mean = jnp.sum(tot, axis=-1, keepdims=True) * (1.0 / MOBA_BLOCK)
        acc = jnp.where(lane == s * _KM_BLOCKS + j, mean, acc)
    o_ref[0] = acc


def _block_means(cache_t, page_table, n_blocks):
    n_seq = page_table.shape[0]
    per_step = _KM_BLOCKS * PAGES_PER_BLOCK
    page_spec = lambda j: pl.BlockSpec(
        (1, KV_HEADS, HEAD_DIM, PAGE_SIZE), lambda b, s, pt, j=j: (pt[b, s * per_step + j], 0, 0, 0))
    return pl.pallas_call(
        _block_mean_body,
        grid_spec=pltpu.PrefetchScalarGridSpec(
            num_scalar_prefetch=1,
            grid=(n_seq, n_blocks // _KM_BLOCKS),
            in_specs=[page_spec(j) for j in range(per_step)],
            out_specs=pl.BlockSpec((1, KV_HEADS, HEAD_DIM, n_blocks), lambda b, s, pt: (b, 0, 0, 0))),
        out_shape=jax.ShapeDtypeStruct((n_seq, KV_HEADS, HEAD_DIM, n_blocks), F32),
        compiler_params=_cparams(2),
        name="block_means",
    )(page_table, *([cache_t] * per_step))


def _sample_topk_body(q_ref, km_ref, idx_ref, *, n_past):
    gs = jnp.concatenate([_dot(q_ref[0, k], km_ref[0, k], HIGHEST) for k in range(KV_HEADS)], axis=0)
    lane = lax.broadcasted_iota(jnp.int32, gs.shape, 1)
    _, picks = _top3_mask(gs, lane, lane < n_past, 1)
    out_lane = lax.broadcasted_iota(jnp.int32, idx_ref.shape[1:], 1)
    out = jnp.zeros(idx_ref.shape[1:], jnp.int32)
    for r, first in enumerate(picks):
        out = jnp.where(out_lane == r, first, out)
    idx_ref[0] = out


def _sample_topk(q, km_t):
    n_seq, _, _, n_past = km_t.shape
    return pl.pallas_call(
        functools.partial(_sample_topk_body, n_past=n_past),
        grid=(n_seq,),
        in_specs=[pl.BlockSpec((1,) + q.shape[1:], lambda b: (b, 0, 0, 0)),
                  pl.BlockSpec((1,) + km_t.shape[1:], lambda b: (b, 0, 0, 0))],
        out_specs=pl.BlockSpec((1, N_HEADS, LANES), lambda b: (b, 0, 0)),
        out_shape=jax.ShapeDtypeStruct((n_seq, N_HEADS, LANES), jnp.int32),
        compiler_params=_cparams(1),
        name="sample_topk",
    )(q, km_t)


_Q_ROWS = 8


def _moba_sample_body(pt_ref, idx_ref, q_ref, kn_ref, vn_ref, *refs, n_past):
    del pt_ref
    n_sel = MOBA_TOPK * PAGES_PER_BLOCK
    k_pages, v_pages, o_ref = refs[:n_sel], refs[n_sel:2 * n_sel], refs[-1]
    b = pl.program_id(0)
    h = pl.program_id(1)
    keys = MOBA_TOPK * MOBA_BLOCK
    q = jnp.broadcast_to(q_ref[0, 0], (_Q_ROWS, HEAD_DIM))
    kt_sel = jnp.concatenate([r[0, 0] for r in k_pages], axis=1)
    vt_sel = jnp.concatenate([r[0, 0] for r in v_pages], axis=1)
    col = lax.broadcasted_iota(jnp.int32, (_Q_ROWS, keys), 1)
    slot = col // MOBA_BLOCK
    base = (b * N_HEADS + h) * MOBA_TOPK
    idx = jnp.zeros((_Q_ROWS, keys), jnp.int32)
    for r in range(MOBA_TOPK):
        idx = jnp.where(slot == r, idx_ref[base + r], idx)
    pos = n_past * MOBA_BLOCK
    dist = (pos - (idx * MOBA_BLOCK + col % MOBA_BLOCK)).astype(F32)
    slope = jnp.exp2(-8.0 * jnp.full((_Q_ROWS, keys), (h + 1).astype(F32), F32) / N_HEADS)
    s = _dot(q, kt_sel, HIGHEST) - slope * dist
    s = jnp.where(idx < n_past, s, NEG_INF)
    s_own = jnp.sum(q * kn_ref[0, 0], axis=-1, keepdims=True)
    m = jnp.maximum(jnp.max(s, axis=-1, keepdims=True), s_own)
    p = jnp.exp(s - m)
    p_own = jnp.exp(s_own - m)
    l = jnp.sum(p, axis=-1, keepdims=True) + p_own
    out = (_dot_nt(p, vt_sel, HIGHEST) + p_own * vn_ref[0, 0]) / l
    o_ref[0, 0] = out[0:1]


def _moba_sample(q, k_new, v_new, cache_kt, cache_vt, page_table, idx_flat, n_past):
    n_seq = page_table.shape[0]

    def page_spec(j):
        r, pg = divmod(j, PAGES_PER_BLOCK)

        def index(b, h, pt, idx):
            blk = idx[(b * N_HEADS + h) * MOBA_TOPK + r]
            return (pt[b, blk * PAGES_PER_BLOCK + pg], h // Q_PER_KV, 0, 0)

        return pl.BlockSpec((1, 1, HEAD_DIM, PAGE_SIZE), index)

    n_sel = MOBA_TOPK * PAGES_PER_BLOCK
    row = (1, 1, 1, HEAD_DIM)
    new_spec = pl.BlockSpec(row, lambda b, h, pt, idx: (b, h // Q_PER_KV, 0, 0))
    head_spec = pl.BlockSpec(row, lambda b, h, pt, idx: (b, h, 0, 0))
    return pl.pallas_call(
        functools.partial(_moba_sample_body, n_past=n_past),
        grid_spec=pltpu.PrefetchScalarGridSpec(
            num_scalar_prefetch=2,
            grid=(n_seq, N_HEADS),
            in_specs=[head_spec, new_spec, new_spec] + [page_spec(j) for j in range(n_sel)] * 2,
            out_specs=head_spec),
        out_shape=jax.ShapeDtypeStruct((n_seq, N_HEADS, 1, HEAD_DIM), F32),
        compiler_params=_cparams(2),
        name="moba_sample",
    )(page_table, idx_flat, q, k_new, v_new, *([cache_kt] * n_sel), *([cache_vt] * n_sel))


def _out_b_body(a_ref, *refs, prec, gated):
    if gated:
        x1_ref, w_ref, g_ref, y_ref = refs
        a = a_ref[...]
    else:
        gate_ref, x1_ref, w_ref, g_ref, y_ref = refs
        gate = gate_ref[...]
        a = (a_ref[...] * (gate * _sigmoid(gate))).astype(w_ref.dtype)
    o = _dot(a, w_ref[...], prec)
    y_ref[...] = x1_ref[...] + _rms_scale(o) * g_ref[...]


def _out_b(att, gate, x1, w, g_post, tm, prec):
    m, d = x1.shape
    tile = pl.BlockSpec((tm, d), lambda i: (i, 0))
    gated = gate is None
    acts = [att, x1] if gated else [att, gate, x1]
    return pl.pallas_call(
        functools.partial(_out_b_body, prec=prec, gated=gated),
        grid=(m // tm,),
        in_specs=[tile] * len(acts) + [_full((d, d)), _full((1, d))],
        out_specs=tile,
        out_shape=jax.ShapeDtypeStruct((m, d), F32),
        compiler_params=_cparams(1),
        name="out_b",
    )(*acts, w, g_post.reshape(1, d))


def _prompt_trunk(x, p):
    bsz, seqlen, d = x.shape
    m = bsz * seqlen
    nb = seqlen // MOBA_BLOCK
    nc = seqlen // S5_CHUNK
    x2 = x.reshape(m, d)
    tm = 512
    tile = pl.BlockSpec((tm, d), lambda i: (i, 0))
    u, gate_a = _norm_matmul(x2, p['g_pre_a'], p['w_in_a'].astype(BF16), None, (m // tm,), tile, [tile, tile],
                             [jax.ShapeDtypeStruct((m, d), F32)] * 2)
    ar, ai, bb_re, bb_im = _s5_discretize(p['lambda_re'], p['lambda_im'], p['log_dt'], p['b_re'], p['b_im'])
    ops = _s5_prompt_operators(ar, ai, bb_re, bb_im, p['c_re'], p['c_im'], p['d_skip'])
    y, h_fin = _s5_prompt(u.reshape(bsz, seqlen, d), ops, min(nc, 64))
    h_fin = h_fin.reshape(d // LANES, bsz, 2, S5_LANE_GROUPS, SSM_STATE).transpose(2, 1, 0, 3, 4)
    h_fin = h_fin.reshape(2, 1, bsz, SSM_GROUPS, SSM_STATE)
    tm = 256
    tile = pl.BlockSpec((tm, d), lambda i: (i, 0))
    x1 = _post_a(y.reshape(m, d), gate_a, x2, p['w_glu'].astype(BF16), p['b_glu'], p['w_out_a'].astype(BF16),
                 p['g_post_a'], None, (m // tm,), tile, tile)
    k, v, gate_b, qt, ka, vta, km = _pre_b(x1, p['g_kv'], p['w_kv'].astype(BF16), p['g_pre_b'],
                                           p['w_in_b'].astype(BF16), MOBA_BLOCK, None, (bsz, nb))
    km = km.reshape(bsz, nb, KV_HEADS, HEAD_DIM).transpose(0, 2, 1, 3)
    att = _moba_prompt(qt, gate_b.reshape(bsz, seqlen, d), ka, vta, km)
    y_out = _out_b(att.reshape(m, d), None, x1, p['w_out_b'].astype(BF16), p['g_post_b'], 512, None)
    to_out = lambda t: t.reshape(bsz, KV_HEADS, HEAD_DIM, seqlen).transpose(0, 3, 1, 2)
    return (y_out.reshape(bsz, seqlen, d), h_fin[0], h_fin[1], to_out(k), to_out(v))


def _sample_trunk(x, h0_re, h0_im, cache_k, cache_v, page_table, p):
    n_seq, seqlen, d = x.shape
    assert seqlen == 1
    g, pch = SSM_GROUPS, SSM_GROUP
    n_past = page_table.shape[1] // PAGES_PER_BLOCK
    x2 = x.reshape(n_seq, d)
    whole = pl.BlockSpec((n_seq, d), lambda i: (0, 0))
    u, gate_a = _norm_matmul(x2, p['g_pre_a'], p['w_in_a'], HIGHEST, (1,), whole, [whole, whole],
                             [jax.ShapeDtypeStruct((n_seq, d), F32)] * 2)
    ar, ai, bb_re, bb_im = _s5_discretize(p['lambda_re'], p['lambda_im'], p['log_dt'], p['b_re'], p['b_im'])
    y_g, hr, hi = _s5_step(u.reshape(n_seq, g, pch).transpose(1, 0, 2),
                           h0_re.transpose(1, 0, 2), h0_im.transpose(1, 0, 2),
                           ar, ai, bb_re, bb_im, p['c_re'], p['c_im'], p['d_skip'])
    y = y_g.transpose(1, 0, 2).reshape(n_seq, d)
    x1 = _post_a(y, gate_a, x2, p['w_glu'], p['b_glu'], p['w_out_a'], p['g_post_a'], HIGHEST, (1,), whole, whole)
    k, v, gate_b, q = _pre_b(x1, p['g_kv'], p['w_kv'], p['g_pre_b'], p['w_in_b'], n_seq, HIGHEST)
    cache_kt = cache_k.transpose(0, 2, 3, 1)
    cache_vt = cache_v.transpose(0, 2, 3, 1)
    km_t = _block_means(cache_kt, page_table, n_past)
    idx = _sample_topk(q.reshape(n_seq, KV_HEADS, Q_PER_KV, HEAD_DIM), km_t)[:, :, :MOBA_TOPK].reshape(-1)
    att = _moba_sample(q.reshape(n_seq, N_HEADS, 1, HEAD_DIM), k.reshape(n_seq, KV_HEADS, 1, HEAD_DIM),
                       v.reshape(n_seq, KV_HEADS, 1, HEAD_DIM), cache_kt, cache_vt, page_table, idx, n_past)
    y_out = _out_b(att.reshape(n_seq, d), gate_b, x1, p['w_out_b'], p['g_post_b'], n_seq, HIGHEST)
    return (y_out.reshape(n_seq, 1, d), hr.transpose(1, 0, 2)[None], hi.transpose(1, 0, 2)[None],
            k.reshape(n_seq, 1, KV_HEADS, HEAD_DIM), v.reshape(n_seq, 1, KV_HEADS, HEAD_DIM))


def kernel(x_prompt, x_sample, state_ssm_re, state_ssm_im, cache_k, cache_v, page_table, g_pre_a, w_in_a, lambda_re, lambda_im, log_dt, b_re, b_im, c_re, c_im, d_skip, w_glu, b_glu, w_out_a, g_post_a, g_kv, w_kv, g_pre_b, w_in_b, w_out_b, g_post_b):
    p = {'g_pre_a': g_pre_a[0], 'w_in_a': w_in_a[0], 'lambda_re': lambda_re[0], 'lambda_im': lambda_im[0],
         'log_dt': log_dt[0], 'b_re': b_re[0], 'b_im': b_im[0], 'c_re': c_re[0], 'c_im': c_im[0],
         'd_skip': d_skip[0], 'w_glu': w_glu[0], 'b_glu': b_glu[0], 'w_out_a': w_out_a[0],
         'g_post_a': g_post_a[0], 'g_kv': g_kv, 'w_kv': w_kv, 'g_pre_b': g_pre_b[0],
         'w_in_b': w_in_b[0], 'w_out_b': w_out_b[0], 'g_post_b': g_post_b[0]}
    y_p, re_p, im_p, k_p, v_p = _prompt_trunk(x_prompt, p)
    y_s, re_s, im_s, k_s, v_s = _sample_trunk(x_sample, state_ssm_re[0], state_ssm_im[0],
                                              cache_k, cache_v, page_table, p)
    return (y_p, y_s, re_p, im_p, k_p, v_p, re_s, im_s, k_s, v_s)
```

```python
import functools
import math

import jax
import jax.numpy as jnp
from jax import lax
from jax.experimental import pallas as pl
from jax.experimental.pallas import tpu as pltpu

F32 = jnp.float32
BF16 = jnp.bfloat16
HIGHEST = lax.Precision.HIGHEST

D_MODEL = 1024
SSM_GROUP = 16
SSM_GROUPS = D_MODEL // SSM_GROUP
SSM_STATE = 64
HEAD_DIM = 64
N_HEADS = D_MODEL // HEAD_DIM
KV_HEADS = 4
Q_PER_KV = N_HEADS // KV_HEADS
KV_WIDTH = KV_HEADS * HEAD_DIM
MOBA_BLOCK = 256
MOBA_TOPK = 3
PAGE_SIZE = 128
PAGES_PER_BLOCK = MOBA_BLOCK // PAGE_SIZE
EPS = 1e-6
NEG_INF = -1e30
REMOVED = -3e38
SCALE = HEAD_DIM ** -0.5
S5_CHUNK = 16
LANES = 128
VMEM_LIMIT = 56 * 1024 * 1024


def _cparams(n_grid):
    return pltpu.CompilerParams(dimension_semantics=("arbitrary",) * n_grid,
                                vmem_limit_bytes=VMEM_LIMIT)


def _dot(a, b, prec=None):
    return jnp.dot(a, b, preferred_element_type=F32, precision=prec)


def _dot_nt(a, b, prec=None):
    return lax.dot_general(a, b, (((1,), (1,)), ((), ())),
                           preferred_element_type=F32, precision=prec)


def _rms_scale(x):
    return x * lax.rsqrt(jnp.mean(x * x, axis=-1, keepdims=True) + EPS)


def _sigmoid(x):
    return 1.0 / (1.0 + jnp.exp(-x))


def _full(shape):
    zeros = (0,) * len(shape)
    return pl.BlockSpec(shape, lambda *_: zeros)


def _tile(ref):
    x = ref[...]
    return x.reshape(x.shape[-2:])


def _norm_matmul_body(x_ref, g_ref, w_ref, *out_refs, prec):
    h = (_rms_scale(_tile(x_ref)) * g_ref[...]).astype(w_ref.dtype)
    off = 0
    for o_ref in out_refs:
        n = o_ref.shape[-1]
        o_ref[...] = _dot(h, w_ref[:, off:off + n], prec).astype(o_ref.dtype).reshape(o_ref.shape)
        off += n


def _norm_matmul(x, g, w, prec, grid, x_spec, out_specs, out_shape):
    d, n = w.shape
    return pl.pallas_call(
        functools.partial(_norm_matmul_body, prec=prec),
        grid=grid,
        in_specs=[x_spec, _full((1, d)), _full((d, n))],
        out_specs=out_specs,
        out_shape=out_shape,
        compiler_params=_cparams(len(grid)),
        name="norm_matmul",
    )(x, g.reshape(1, d), w)


S5_LANE_GROUPS = LANES // SSM_GROUP
S5_STATE_LANES = S5_LANE_GROUPS * SSM_STATE
S5_PAIR = 2 * LANES
S5_MC_ROWS = LANES


def _iota2(shape, dim):
    return lax.broadcasted_iota(jnp.int32, shape, dim)


def _expand_block_diagonal(mc_ref, sc_ref, rc_ref, m_s, s_s, r_s):
    p, n, lg = SSM_GROUP, SSM_STATE, S5_LANE_GROUPS
    sl = S5_STATE_LANES
    grp = lambda idx, per: (idx // per) % lg
    rows, cols = s_s.shape
    i, j = _iota2((2 * n, cols), 0), _iota2((2 * n, cols), 1)
    spread = (i == (j // sl) * n + j % n).astype(F32).astype(BF16)
    blk = 256
    for r0 in range(0, rows, blk):
        full = _dot(sc_ref[0, r0:r0 + blk, :], spread)
        keep = grp(_iota2((blk, cols), 0) + r0, p) == grp(_iota2((blk, cols), 1), n)
        s_s[r0:r0 + blk, :] = jnp.where(keep, full, 0.0).astype(BF16)
    rows, cols = r_s.shape
    i, j = _iota2((rows, 2 * n), 0), _iota2((rows, 2 * n), 1)
    spread = (j == (i // sl) * n + i % n).astype(F32).astype(BF16)
    for c0 in range(0, cols, blk):
        full = _dot(spread, rc_ref[0, :, c0:c0 + blk])
        keep = grp(_iota2((rows, blk), 0), n) == grp(_iota2((rows, blk), 1) + c0, p)
        r_s[:, c0:c0 + blk] = jnp.where(keep, full, 0.0).astype(BF16)
    n_delta, rows, cols = m_s.shape
    i, j = _iota2((rows, S5_MC_ROWS), 0), _iota2((rows, S5_MC_ROWS), 1)
    spread = (j == (i // LANES) * p + i % p).astype(F32).astype(BF16)
    keep = grp(_iota2((rows, cols), 0), p) == grp(_iota2((rows, cols), 1), p)
    for dl in range(n_delta):
        m_s[dl] = jnp.where(keep, _dot(spread, mc_ref[0, dl]), 0.0).astype(BF16)


def _s5_prompt_body(u_ref, mc_ref, sc_ref, rc_ref, a_ref, d_ref, y_ref, hfin_ref,
                    m_s, s_s, r_s, e_s, hin_s, h_s):
    bsz, tokens, _ = u_ref.shape
    t_chunk = S5_CHUNK
    rc = tokens // t_chunk
    sl = S5_STATE_LANES

    @pl.when(pl.program_id(1) == 0)
    def _():
        h_s[...] = jnp.zeros(h_s.shape, F32)
        _expand_block_diagonal(mc_ref, sc_ref, rc_ref, m_s, s_s, r_s)

    token = lambda ref, b, s: ref.at[b, pl.ds(s, rc, stride=t_chunk), :]
    xs = [jnp.concatenate([token(u_ref, b, s)[...] for b in range(bsz)], axis=0) for s in range(t_chunk)]
    xcat = jnp.concatenate([x.astype(BF16) for x in xs], axis=1)
    half = sl // LANES
    cols = lambda j: slice(j * LANES, (j + 1) * LANES)
    e = _dot(xcat, s_s[...])
    for j in range(2 * half):
        e_s[j] = e[:, cols(j)]
    ar = [jnp.broadcast_to(a_ref[0, 0:1, cols(j)], (bsz, LANES)) for j in range(half)]
    ai = [jnp.broadcast_to(a_ref[0, 1:2, cols(j)], (bsz, LANES)) for j in range(half)]

    def step(c, carry):
        hr, hi = carry
        at_c = pl.ds(c, bsz, stride=rc)
        new_r, new_i = [], []
        for j in range(half):
            hin_s[j, at_c, :] = hr[j]
            hin_s[half + j, at_c, :] = hi[j]
            new_r.append(ar[j] * hr[j] - ai[j] * hi[j] + e_s[j, at_c, :])
            new_i.append(ar[j] * hi[j] + ai[j] * hr[j] + e_s[half + j, at_c, :])
        return tuple(new_r), tuple(new_i)

    carry0 = (tuple(h_s[0, j] for j in range(half)), tuple(h_s[1, j] for j in range(half)))
    hr, hi = lax.fori_loop(0, rc, step, carry0)
    for j in range(half):
        h_s[0, j] = hr[j]
        h_s[1, j] = hi[j]
        hfin_ref[0, :, cols(j)] = hr[j]
        hfin_ref[0, :, cols(half + j)] = hi[j]
    hin = jnp.concatenate([hin_s[j] for j in range(2 * half)], axis=1).astype(BF16)
    d_row = d_ref[0]
    for i in range(t_chunk // 2):
        acc = _dot(hin, r_s[:, i * S5_PAIR:(i + 1) * S5_PAIR])
        for j in range(i + 1):
            acc += _dot(xcat[:, j * S5_PAIR:(j + 1) * S5_PAIR], m_s[i - j])
        for tt in range(2):
            t = 2 * i + tt
            y_t = acc[:, tt * LANES:(tt + 1) * LANES] + d_row * xs[t]
            for b in range(bsz):
                token(y_ref, b, t)[...] = y_t[b * rc:(b + 1) * rc]


def _s5_prompt(u, ops, rc):
    bsz, seqlen, d = u.shape
    n_lb = d // LANES
    mc, sc, rc_op, a16, dl = ops
    sl = S5_STATE_LANES
    t_chunk = S5_CHUNK
    tokens = rc * t_chunk
    act = pl.BlockSpec((bsz, tokens, LANES), lambda lb, ct: (0, ct, lb))
    per_lb = lambda shape: pl.BlockSpec((1,) + shape, lambda lb, ct: (lb,) + (0,) * len(shape))
    return pl.pallas_call(
        _s5_prompt_body,
        grid=(n_lb, seqlen // tokens),
        in_specs=[act, per_lb(mc.shape[1:]), per_lb(sc.shape[1:]), per_lb(rc_op.shape[1:]),
                  per_lb((2, sl)), per_lb((1, LANES))],
        out_specs=[act, per_lb((bsz, 2 * sl))],
        out_shape=[jax.ShapeDtypeStruct(u.shape, F32),
                   jax.ShapeDtypeStruct((n_lb, bsz, 2 * sl), F32)],
        scratch_shapes=[pltpu.VMEM((t_chunk // 2, S5_PAIR, S5_PAIR), BF16),
                        pltpu.VMEM((t_chunk * LANES, 2 * sl), BF16),
                        pltpu.VMEM((2 * sl, t_chunk * LANES), BF16),
                        pltpu.VMEM((2 * sl // LANES, bsz * rc, LANES), F32),
                        pltpu.VMEM((2 * sl // LANES, bsz * rc, LANES), F32),
                        pltpu.VMEM((2, sl // LANES, bsz, LANES), F32)],
        compiler_params=_cparams(2),
        name="s5_prompt",
    )(u, mc, sc, rc_op, a16, dl)


def _s5_discretize(lam_re, lam_im, log_dt, b_re, b_im):
    dt = jnp.exp(log_dt)[:, None]
    mag = jnp.exp(lam_re * dt)
    ar = mag * jnp.cos(lam_im * dt)
    ai = mag * jnp.sin(lam_im * dt)
    den = lam_re * lam_re + lam_im * lam_im
    nr = ar - 1.0
    coef_re = (nr * lam_re + ai * lam_im) / den
    coef_im = (ai * lam_re - nr * lam_im) / den
    bb_re = coef_re[..., None] * b_re - coef_im[..., None] * b_im
    bb_im = coef_re[..., None] * b_im + coef_im[..., None] * b_re
    return ar, ai, bb_re, bb_im


def _s5_prompt_operators(ar, ai, bb_re, bb_im, c_re, c_im, d_skip):
    t = S5_CHUNK
    g, n = ar.shape
    p = SSM_GROUP
    lg = S5_LANE_GROUPS
    n_lb = g // lg
    pr, pi = [jnp.ones_like(ar)], [jnp.zeros_like(ar)]
    for _ in range(t):
        pr, pi = pr + [pr[-1] * ar - pi[-1] * ai], pi + [pr[-1] * ai + pi[-1] * ar]
    pr = jnp.stack(pr)
    pi = jnp.stack(pi)
    cp_re = c_re[None] * pr[:, :, None, :] - c_im[None] * pi[:, :, None, :]
    cp_im = c_re[None] * pi[:, :, None, :] + c_im[None] * pr[:, :, None, :]
    k = (jnp.einsum('tgpn,gnq->gtpq', cp_re[:t], bb_re, precision=HIGHEST)
         - jnp.einsum('tgpn,gnq->gtpq', cp_im[:t], bb_im, precision=HIGHEST))
    k_lag = lambda lag: k[:, lag] if lag >= 0 else jnp.zeros_like(k[:, 0])
    kt = jnp.stack([jnp.stack([jnp.stack([k_lag(2 * dl + tt - ss) for tt in range(2)], axis=1)
                               for ss in range(2)], axis=1) for dl in range(t // 2)], axis=1)
    kt = kt.reshape(n_lb, lg, t // 2, 2, 2, p, p)
    m = kt.transpose(0, 2, 3, 6, 4, 1, 5).reshape(n_lb, t // 2, 2 * p, 2 * lg * p)
    m = jnp.pad(m, ((0, 0), (0, 0), (0, S5_MC_ROWS - 2 * p), (0, 0)))
    pr_rev = jnp.stack([pr[t - 1 - s] for s in range(t)])
    pi_rev = jnp.stack([pi[t - 1 - s] for s in range(t)])
    s_re = pr_rev[:, :, :, None] * bb_re[None] - pi_rev[:, :, :, None] * bb_im[None]
    s_im = pr_rev[:, :, :, None] * bb_im[None] + pi_rev[:, :, :, None] * bb_re[None]
    s_ri = jnp.stack([s_re, s_im]).reshape(2, t, n_lb, lg, n, p)
    s = s_ri.transpose(2, 1, 3, 5, 0, 4).reshape(n_lb, t * lg * p, 2 * n)
    r_ri = jnp.stack([cp_re[1:], -cp_im[1:]]).reshape(2, t, n_lb, lg, p, n)
    r = r_ri.transpose(2, 0, 5, 1, 3, 4).reshape(n_lb, 2 * n, t * lg * p)
    a16 = jnp.stack([pr[t].reshape(n_lb, lg * n), pi[t].reshape(n_lb, lg * n)], axis=1)
    return (m.astype(BF16), s.astype(BF16), r.astype(BF16), a16, d_skip.reshape(n_lb, 1, lg * p))


def _s5_step_body(u_ref, h0r_ref, h0i_ref, ar_ref, ai_ref, bbr_ref, bbi_ref, cr_ref, ci_ref,
                  d_ref, y_ref, hr_ref, hi_ref):
    u = u_ref[0]
    ar, ai = ar_ref[0], ai_ref[0]
    h0r, h0i = h0r_ref[0], h0i_ref[0]
    hr = _dot_nt(u, bbr_ref[0], HIGHEST) + ar * h0r - ai * h0i
    hi = _dot_nt(u, bbi_ref[0], HIGHEST) + ar * h0i + ai * h0r
    hr_ref[0] = hr
    hi_ref[0] = hi
    y_ref[0] = (_dot_nt(hr, cr_ref[0], HIGHEST) - _dot_nt(hi, ci_ref[0], HIGHEST)
                + d_ref[0] * u)


def _s5_step(u_g, h0r, h0i, ar, ai, bb_re, bb_im, c_re, c_im, d_skip):
    g, b, p = u_g.shape
    n = ar.shape[-1]
    per_group = lambda shape: pl.BlockSpec((1,) + shape, lambda i: (i, 0, 0))
    return pl.pallas_call(
        _s5_step_body,
        grid=(g,),
        in_specs=[per_group((b, p)), per_group((b, n)), per_group((b, n)),
                  per_group((1, n)), per_group((1, n)),
                  per_group((n, p)), per_group((n, p)),
                  per_group((p, n)), per_group((p, n)), per_group((1, p))],
        out_specs=[per_group((b, p)), per_group((b, n)), per_group((b, n))],
        out_shape=[jax.ShapeDtypeStruct((g, b, p), F32),
                   jax.ShapeDtypeStruct((g, b, n), F32),
                   jax.ShapeDtypeStruct((g, b, n), F32)],
        compiler_params=_cparams(1),
        name="s5_step",
    )(u_g, h0r, h0i, ar.reshape(g, 1, n), ai.reshape(g, 1, n), bb_re, bb_im, c_re, c_im,
      d_skip.reshape(g, 1, p))


def _post_a_body(y_ref, ga_ref, x_ref, wglu_ref, bglu_ref, wout_ref, gpost_ref, x1_ref, *, prec):
    y = _tile(y_ref)
    wdt = wglu_ref.dtype
    gy = 0.5 * y * (1.0 + lax.erf(y * math.sqrt(0.5)))
    z = _dot(gy.astype(wdt), wglu_ref[...], prec) + bglu_ref[...]
    gate = _tile(ga_ref)
    t = (gy * _sigmoid(z)) * (gate * _sigmoid(gate))
    o = _dot(t.astype(wdt), wout_ref[...], prec)
    x1_ref[...] = (_tile(x_ref) + _rms_scale(o) * gpost_ref[...]).reshape(x1_ref.shape)


def _post_a(y, gate, x, w_glu, b_glu, w_out, g_post, prec, grid, y_spec, x_spec):
    d = w_glu.shape[0]
    return pl.pallas_call(
        functools.partial(_post_a_body, prec=prec),
        grid=grid,
        in_specs=[y_spec, x_spec, x_spec, _full((d, d)), _full((1, d)), _full((d, d)), _full((1, d))],
        out_specs=x_spec,
        out_shape=jax.ShapeDtypeStruct(x.shape, F32),
        compiler_params=_cparams(len(grid)),
        name="post_a",
    )(y, gate, x, w_glu, b_glu.reshape(1, d), w_out, g_post.reshape(1, d))


ALIBI_ROWS = 16
FEAT_OFF = HEAD_DIM
V_ROWS = HEAD_DIM + 16


def _aug_width(nb):
    nbp = -(-nb // 16) * 16
    return -(-(FEAT_OFF + ALIBI_ROWS + nbp) // LANES) * LANES, nbp


def _pre_b_body(x1_ref, gkv_ref, wkv_ref, gb_ref, winb_ref, k_ref, v_ref, gate_ref, *rest, prec, nb):
    xn = _rms_scale(x1_ref[...])
    wdt = wkv_ref.dtype
    hk = (xn * gkv_ref[...]).astype(wdt)
    k = _dot(hk, wkv_ref[:, :KV_WIDTH], prec)
    v = _dot(hk, wkv_ref[:, KV_WIDTH:], prec)
    hq = (xn * gb_ref[...]).astype(wdt)
    q = _dot(hq, winb_ref[:, :D_MODEL], prec) * SCALE
    gate_ref[...] = _dot(hq, winb_ref[:, D_MODEL:], prec)
    if nb is None:
        q_ref, = rest
        q_ref[...] = q
        k_ref[...] = k
        v_ref[...] = v
        return
    qt_ref, ka_ref, vta_ref, km_ref, st_ref = rest
    tm = q.shape[0]
    qt_ref[0, 0] = q.T.astype(BF16)
    km_ref[0] = jnp.mean(k, axis=0, keepdims=True)
    pick = lambda width: (_iota2((width, LANES), 0) // HEAD_DIM == _iota2((width, LANES), 1)).astype(F32).astype(BF16)
    sq_norms = lambda x: jnp.max(_dot((x * x).astype(BF16), pick(x.shape[1])), axis=0, keepdims=True)
    st_ref[0, 0:1, :] = sq_norms(q)
    st_ref[0, 1:2, :] = sq_norms(k)
    n = pl.program_id(0) % nb
    width = ka_ref.shape[-1] - FEAT_OFF
    lane = lax.broadcasted_iota(jnp.int32, (tm, width), 1)
    rowi = lax.broadcasted_iota(jnp.int32, (tm, width), 0)
    feat = jnp.where(lane < 3, n, jnp.where(lane < 6, rowi, jnp.where(lane < 9, 1, 0)))
    feat = jnp.where(lane - ALIBI_ROWS == n, 1, feat).astype(F32).astype(BF16)
    kb = k.astype(BF16)
    vt = v.T
    k_ref[0] = k.T
    v_ref[0] = vt
    ones_row = (lax.broadcasted_iota(jnp.int32, (V_ROWS - HEAD_DIM, tm), 0) == 0).astype(F32).astype(BF16)
    for h in range(KV_HEADS):
        ka_ref[0, h, :, :FEAT_OFF] = kb[:, h * HEAD_DIM:(h + 1) * HEAD_DIM]
        ka_ref[0, h, :, FEAT_OFF:] = feat
        vta_ref[0, 0, h * V_ROWS:h * V_ROWS + HEAD_DIM, :] = vt[h * HEAD_DIM:(h + 1) * HEAD_DIM].astype(BF16)
        vta_ref[0, 0, h * V_ROWS + HEAD_DIM:(h + 1) * V_ROWS, :] = ones_row


def _pre_b(x1, g_kv, w_kv, g_pre_b, w_in_b, tm, prec, prompt_blocks=None):
    m, d = x1.shape
    row = lambda wd: pl.BlockSpec((tm, wd), lambda i: (i, 0))
    nb = None
    if prompt_blocks is None:
        kv_spec = row(KV_WIDTH)
        kv_shape = jax.ShapeDtypeStruct((m, KV_WIDTH), F32)
        out_specs = [kv_spec, kv_spec, row(d), row(d)]
        out_shape = [kv_shape, kv_shape, jax.ShapeDtypeStruct((m, d), F32), jax.ShapeDtypeStruct((m, d), F32)]
    else:
        bsz, nb = prompt_blocks
        assert tm == MOBA_BLOCK and m == bsz * nb * tm
        aug, _ = _aug_width(nb)
        kv_spec = pl.BlockSpec((1, KV_WIDTH, tm), lambda i: (i // nb, 0, i % nb))
        kv_shape = jax.ShapeDtypeStruct((bsz, KV_WIDTH, nb * tm), F32)
        out_specs = [kv_spec, kv_spec, row(d)]
        out_shape = [kv_shape, kv_shape, jax.ShapeDtypeStruct((m, d), F32)]
        out_specs += [pl.BlockSpec((1, 1, d, tm), lambda i: (i // nb, i % nb, 0, 0)),
                      pl.BlockSpec((1, KV_HEADS, tm, aug), lambda i: (i // nb, 0, i % nb, 0)),
                      pl.BlockSpec((1, 1, KV_HEADS * V_ROWS, tm), lambda i: (i // nb, i % nb, 0, 0)),
                      pl.BlockSpec((1, 1, KV_WIDTH), lambda i: (i, 0, 0)),
                      pl.BlockSpec((1, 2, LANES), lambda i: (i, 0, 0))]
        out_shape += [jax.ShapeDtypeStruct((bsz, nb, d, tm), BF16),
                      jax.ShapeDtypeStruct((bsz, KV_HEADS, nb * tm, aug), BF16),
                      jax.ShapeDtypeStruct((bsz, nb, KV_HEADS * V_ROWS, tm), BF16),
                      jax.ShapeDtypeStruct((bsz * nb, 1, KV_WIDTH), F32),
                      jax.ShapeDtypeStruct((bsz * nb, 2, LANES), F32)]
    return pl.pallas_call(
        functools.partial(_pre_b_body, prec=prec, nb=nb),
        grid=(m // tm,),
        in_specs=[row(d), _full((1, d)), _full((d, 2 * KV_WIDTH)), _full((1, d)), _full((d, 2 * d))],
        out_specs=out_specs,
        out_shape=out_shape,
        compiler_params=_cparams(1),
        name="pre_b",
    )(x1, g_kv.reshape(1, d), w_kv, g_pre_b.reshape(1, d), w_in_b)


def _top3_mask(gs, idx, valid, axis):
    n = gs.shape[axis]
    gs = jnp.where(valid, gs, NEG_INF)
    sel = jnp.zeros(gs.shape, F32)
    picks = []
    for _ in range(MOBA_TOPK):
        mx = jnp.max(gs, axis=axis, keepdims=True)
        first = jnp.min(jnp.where(gs == mx, idx, n), axis=axis, keepdims=True)
        pick = idx == first
        sel = jnp.where(pick & valid, 1.0, sel)
        gs = jnp.where(pick, REMOVED, gs)
        picks.append(first)
    return sel, picks


def _split3(x):
    hi = x.astype(BF16).astype(F32)
    mid = (x - hi).astype(BF16).astype(F32)
    lo = (x - hi - mid).astype(BF16).astype(F32)
    return [hi, mid, lo]


def _moba_prompt_body(first_ref, qt_ref, g_ref, ka_ref, vta_ref, km_ref, o_ref, qa_s, qd_s):
    h = pl.program_id(1)
    c = pl.program_id(2)
    blk = MOBA_BLOCK
    nb = km_ref.shape[2]
    aug, nbp = _aug_width(nb)
    km = km_ref[0, 0]
    row = lax.broadcasted_iota(jnp.int32, (nb, blk), 0)
    frow = lax.broadcasted_iota(jnp.int32, (ALIBI_ROWS, blk), 0)
    pos_q = (c * blk + lax.broadcasted_iota(jnp.int32, (1, blk), 1)).astype(F32)
    for g in range(Q_PER_KV):
        lanes = slice(g * blk, (g + 1) * blk)
        qg = qt_ref[0, 0, g * HEAD_DIM:(g + 1) * HEAD_DIM, :]
        sel, _ = _top3_mask(_dot(km, qg.astype(F32), HIGHEST), row, row < c, 0)
        bias = jnp.where(sel > 0.0, 0.0, NEG_INF)
        if nbp > nb:
            bias = jnp.concatenate([bias, jnp.zeros((nbp - nb, blk), F32)], axis=0)
        head1 = (h * Q_PER_KV + g + 1).astype(F32)
        slope = jnp.exp2(-8.0 * jnp.full((1, blk), head1, F32) / N_HEADS)
        terms = _split3(slope * float(blk)) + _split3(slope) + _split3(-slope * pos_q)
        feat = jnp.zeros((ALIBI_ROWS, blk), F32)
        for r, term in enumerate(terms):
            feat = jnp.where(frow == r, term, feat)
        for ref, b_rows in ((qa_s, bias.astype(BF16)), (qd_s, jnp.zeros((nbp, blk), BF16))):
            ref[:FEAT_OFF, lanes] = qg
            ref[FEAT_OFF:FEAT_OFF + ALIBI_ROWS, lanes] = feat.astype(BF16)
            ref[FEAT_OFF + ALIBI_ROWS:FEAT_OFF + ALIBI_ROWS + nbp, lanes] = b_rows
            if FEAT_OFF + ALIBI_ROWS + nbp < aug:
                ref[FEAT_OFF + ALIBI_ROWS + nbp:, lanes] = jnp.zeros((aug - FEAT_OFF - ALIBI_ROWS - nbp, blk), BF16)

    def keys(n):
        return ka_ref[0, 0, pl.ds(pl.multiple_of(n * blk, blk), blk), :]

    s = _dot(keys(c), qd_s[...])
    key = lax.broadcasted_iota(jnp.int32, s.shape, 0)
    qry = lax.broadcasted_iota(jnp.int32, s.shape, 1) & (blk - 1)
    s = jnp.where(key <= qry, s, NEG_INF)
    m = jnp.max(s, axis=0, keepdims=True)
    acc = _dot(vta_ref[0, c], jnp.exp(s - m).astype(BF16))

    def block_pair(i, carry):
        m, acc = carry
        n0, n1 = 2 * i, 2 * i + 1
        s0 = _dot(keys(n0), qa_s[...])
        s1 = _dot(keys(n1), qa_s[...])
        m2 = jnp.maximum(m, jnp.maximum(jnp.max(s0, axis=0, keepdims=True), jnp.max(s1, axis=0, keepdims=True)))
        p0 = jnp.exp(s0 - m2).astype(BF16)
        p1 = jnp.exp(s1 - m2).astype(BF16)
        return m2, jnp.exp(m - m2) * acc + _dot(vta_ref[0, n0], p0) + _dot(vta_ref[0, n1], p1)

    first = first_ref[(pl.program_id(0) * KV_HEADS + h) * nb + c]
    m, acc = lax.fori_loop(first, (c + 1) // 2, block_pair, (m, acc))
    out_t = acc[:HEAD_DIM] / acc[HEAD_DIM:HEAD_DIM + 1]
    att = jnp.concatenate([out_t[:, g * blk:(g + 1) * blk] for g in range(Q_PER_KV)], axis=0).T
    gate = g_ref[0]
    o_ref[0] = (att * (gate * _sigmoid(gate))).astype(o_ref.dtype)


_SKIP_LOGIT = -120.0
_NORM_SLACK = 1.02


def _first_needed_pair(stats, bsz, nb):
    q_max = jnp.sqrt(stats[:, 0, :N_HEADS]).reshape(bsz, nb, KV_HEADS, Q_PER_KV) * _NORM_SLACK
    k_max = jnp.sqrt(stats[:, 1, :KV_HEADS]).reshape(bsz, nb, KV_HEADS) * _NORM_SLACK
    slopes = jnp.exp2(-8.0 * jnp.arange(1, N_HEADS + 1, dtype=F32) / N_HEADS).reshape(KV_HEADS, Q_PER_KV)
    c = jnp.arange(nb)[:, None]
    n = jnp.arange(nb)[None, :]
    min_dist = ((c - n) * MOBA_BLOCK - (MOBA_BLOCK - 1)).astype(F32)
    k_pair = k_max[:, None, :, :] + k_max[:, :, None, :]
    bound = (q_max[:, :, None, :, :] * k_pair[..., None]
             - slopes[None, None, None] * min_dist[None, :, :, None, None])
    dead = jnp.all(bound < _SKIP_LOGIT, axis=-1) & (n < c)[None, :, :, None]
    lead = jnp.sum(jnp.cumprod(dead.astype(jnp.int32), axis=2), axis=2)
    return (lead // 2).transpose(0, 2, 1).reshape(-1).astype(jnp.int32)


def _moba_prompt(qt, gate, ka, vta, km, first_pair):
    bsz, seqlen, d = gate.shape
    nb = seqlen // MOBA_BLOCK
    width = Q_PER_KV * HEAD_DIM
    aug, _ = _aug_width(nb)
    tile = pl.BlockSpec((1, MOBA_BLOCK, width), lambda b, h, c, fp: (b, c, h))
    return pl.pallas_call(
        _moba_prompt_body,
        grid_spec=pltpu.PrefetchScalarGridSpec(
            num_scalar_prefetch=1,
            grid=(bsz, KV_HEADS, nb),
            in_specs=[pl.BlockSpec((1, 1, width, MOBA_BLOCK), lambda b, h, c, fp: (b, c, h, 0)),
                      tile,
                      pl.BlockSpec((1, 1, seqlen, aug), lambda b, h, c, fp: (b, h, 0, 0)),
                      pl.BlockSpec((1, nb, V_ROWS, MOBA_BLOCK), lambda b, h, c, fp: (b, 0, h, 0)),
                      pl.BlockSpec((1, 1, nb, HEAD_DIM), lambda b, h, c, fp: (b, h, 0, 0))],
            out_specs=tile,
            scratch_shapes=[pltpu.VMEM((aug, Q_PER_KV * MOBA_BLOCK), BF16) for _ in range(2)]),
        out_shape=jax.ShapeDtypeStruct((bsz, seqlen, d), BF16),
        compiler_params=_cparams(3),
        name="moba_prompt",
    )(first_pair, qt, gate, ka, vta, km)


_KM_BLOCKS = 16


def _block_mean_body(pt_ref, *refs):
    del pt_ref
    pages, o_ref = refs[:-1], refs[-1]
    s = pl.program_id(1)

    @pl.when(s == 0)
    def _():
        o_ref[...] = jnp.zeros(o_ref.shape, F32)

    lane = lax.broadcasted_iota(jnp.int32, o_ref.shape[1:], 2)
    acc = o_ref[0]
    for j in range(_KM_BLOCKS):
        tot = sum(pages[PAGES_PER_BLOCK * j + r][0] for r in range(PAGES_PER_BLOCK))
        mean = jnp.sum(tot, axis=-1, keepdims=True) * (1.0 / MOBA_BLOCK)
        acc = jnp.where(lane == s * _KM_BLOCKS + j, mean, acc)
    o_ref[0] = acc


def _block_means(cache_t, page_table, n_blocks):
    n_seq = page_table.shape[0]
    per_step = _KM_BLOCKS * PAGES_PER_BLOCK
    page_spec = lambda j: pl.BlockSpec(
        (1, KV_HEADS, HEAD_DIM, PAGE_SIZE), lambda b, s, pt, j=j: (pt[b, s * per_step + j], 0, 0, 0))
    return pl.pallas_call(
        _block_mean_body,
        grid_spec=pltpu.PrefetchScalarGridSpec(
            num_scalar_prefetch=1,
            grid=(n_seq, n_blocks // _KM_BLOCKS),
            in_specs=[page_spec(j) for j in range(per_step)],
            out_specs=pl.BlockSpec((1, KV_HEADS, HEAD_DIM, n_blocks), lambda b, s, pt: (b, 0, 0, 0))),
        out_shape=jax.ShapeDtypeStruct((n_seq, KV_HEADS, HEAD_DIM, n_blocks), F32),
        compiler_params=_cparams(2),
        name="block_means",
    )(page_table, *([cache_t] * per_step))


def _sample_topk_body(q_ref, km_ref, idx_ref, *, n_past):
    gs = jnp.concatenate([_dot(q_ref[0, k], km_ref[0, k], HIGHEST) for k in range(KV_HEADS)], axis=0)
    lane = lax.broadcasted_iota(jnp.int32, gs.shape, 1)
    _, picks = _top3_mask(gs, lane, lane < n_past, 1)
    out_lane = lax.broadcasted_iota(jnp.int32, idx_ref.shape[1:], 1)
    out = jnp.zeros(idx_ref.shape[1:], jnp.int32)
    for r, first in enumerate(picks):
        out = jnp.where(out_lane == r, first, out)
    idx_ref[0] = out


def _sample_topk(q, km_t):
    n_seq, _, _, n_past = km_t.shape
    return pl.pallas_call(
        functools.partial(_sample_topk_body, n_past=n_past),
        grid=(n_seq,),
        in_specs=[pl.BlockSpec((1,) + q.shape[1:], lambda b: (b, 0, 0, 0)),
                  pl.BlockSpec((1,) + km_t.shape[1:], lambda b: (b, 0, 0, 0))],
        out_specs=pl.BlockSpec((1, N_HEADS, LANES), lambda b: (b, 0, 0)),
        out_shape=jax.ShapeDtypeStruct((n_seq, N_HEADS, LANES), jnp.int32),
        compiler_params=_cparams(1),
        name="sample_topk",
    )(q, km_t)


_Q_ROWS = 8


def _moba_sample_body(pt_ref, idx_ref, q_ref, kn_ref, vn_ref, *refs, n_past):
    del pt_ref
    n_sel = MOBA_TOPK * PAGES_PER_BLOCK
    k_pages, v_pages, o_ref = refs[:n_sel], refs[n_sel:2 * n_sel], refs[-1]
    b = pl.program_id(0)
    h = pl.program_id(1)
    keys = MOBA_TOPK * MOBA_BLOCK
    q = jnp.broadcast_to(q_ref[0, 0], (_Q_ROWS, HEAD_DIM))
    kt_sel = jnp.concatenate([r[0, 0] for r in k_pages], axis=1)
    vt_sel = jnp.concatenate([r[0, 0] for r in v_pages], axis=1)
    col = lax.broadcasted_iota(jnp.int32, (_Q_ROWS, keys), 1)
    slot = col // MOBA_BLOCK
    base = (b * N_HEADS + h) * MOBA_TOPK
    idx = jnp.zeros((_Q_ROWS, keys), jnp.int32)
    for r in range(MOBA_TOPK):
        idx = jnp.where(slot == r, idx_ref[base + r], idx)
    pos = n_past * MOBA_BLOCK
    dist = (pos - (idx * MOBA_BLOCK + col % MOBA_BLOCK)).astype(F32)
    slope = jnp.exp2(-8.0 * jnp.full((_Q_ROWS, keys), (h + 1).astype(F32), F32) / N_HEADS)
    s = _dot(q.astype(BF16), kt_sel.astype(BF16)) - slope * dist
    s = jnp.where(idx < n_past, s, NEG_INF)
    s_own = jnp.sum(q * kn_ref[0, 0], axis=-1, keepdims=True)
    m = jnp.maximum(jnp.max(s, axis=-1, keepdims=True), s_own)
    p = jnp.exp(s - m)
    p_own = jnp.exp(s_own - m)
    l = jnp.sum(p, axis=-1, keepdims=True) + p_own
    out = (_dot_nt(p.astype(BF16), vt_sel.astype(BF16)) + p_own * vn_ref[0, 0]) / l
    o_ref[0, 0] = out[0:1]


def _moba_sample(q, k_new, v_new, cache_kt, cache_vt, page_table, idx_flat, n_past):
    n_seq = page_table.shape[0]

    def page_spec(j):
        r, pg = divmod(j, PAGES_PER_BLOCK)

        def index(b, h, pt, idx):
            blk = idx[(b * N_HEADS + h) * MOBA_TOPK + r]
            return (pt[b, blk * PAGES_PER_BLOCK + pg], h // Q_PER_KV, 0, 0)

        return pl.BlockSpec((1, 1, HEAD_DIM, PAGE_SIZE), index)

    n_sel = MOBA_TOPK * PAGES_PER_BLOCK
    row = (1, 1, 1, HEAD_DIM)
    new_spec = pl.BlockSpec(row, lambda b, h, pt, idx: (b, h // Q_PER_KV, 0, 0))
    head_spec = pl.BlockSpec(row, lambda b, h, pt, idx: (b, h, 0, 0))
    return pl.pallas_call(
        functools.partial(_moba_sample_body, n_past=n_past),
        grid_spec=pltpu.PrefetchScalarGridSpec(
            num_scalar_prefetch=2,
            grid=(n_seq, N_HEADS),
            in_specs=[head_spec, new_spec, new_spec] + [page_spec(j) for j in range(n_sel)] * 2,
            out_specs=head_spec),
        out_shape=jax.ShapeDtypeStruct((n_seq, N_HEADS, 1, HEAD_DIM), F32),
        compiler_params=_cparams(2),
        name="moba_sample",
    )(page_table, idx_flat, q, k_new, v_new, *([cache_kt] * n_sel), *([cache_vt] * n_sel))


def _out_b_body(a_ref, *refs, prec, gated):
    if gated:
        x1_ref, w_ref, g_ref, y_ref = refs
        a = a_ref[...]
    else:
        gate_ref, x1_ref, w_ref, g_ref, y_ref = refs
        gate = gate_ref[...]
        a = (a_ref[...] * (gate * _sigmoid(gate))).astype(w_ref.dtype)
    o = _dot(a, w_ref[...], prec)
    y_ref[...] = x1_ref[...] + _rms_scale(o) * g_ref[...]


def _out_b(att, gate, x1, w, g_post, tm, prec):
    m, d = x1.shape
    tile = pl.BlockSpec((tm, d), lambda i: (i, 0))
    gated = gate is None
    acts = [att, x1] if gated else [att, gate, x1]
    return pl.pallas_call(
        functools.partial(_out_b_body, prec=prec, gated=gated),
        grid=(m // tm,),
        in_specs=[tile] * len(acts) + [_full((d, d)), _full((1, d))],
        out_specs=tile,
        out_shape=jax.ShapeDtypeStruct((m, d), F32),
        compiler_params=_cparams(1),
        name="out_b",
    )(*acts, w, g_post.reshape(1, d))


def _prompt_trunk(x, p):
    bsz, seqlen, d = x.shape
    m = bsz * seqlen
    nb = seqlen // MOBA_BLOCK
    nc = seqlen // S5_CHUNK
    x2 = x.reshape(m, d)
    tm = 512
    tile = pl.BlockSpec((tm, d), lambda i: (i, 0))
    u, gate_a = _norm_matmul(x2, p['g_pre_a'], p['w_in_a'].astype(BF16), None, (m // tm,), tile, [tile, tile],
                             [jax.ShapeDtypeStruct((m, d), F32)] * 2)
    ar, ai, bb_re, bb_im = _s5_discretize(p['lambda_re'], p['lambda_im'], p['log_dt'], p['b_re'], p['b_im'])
    ops = _s5_prompt_operators(ar, ai, bb_re, bb_im, p['c_re'], p['c_im'], p['d_skip'])
    y, h_fin = _s5_prompt(u.reshape(bsz, seqlen, d), ops, min(nc, 64))
    h_fin = h_fin.reshape(d // LANES, bsz, 2, S5_LANE_GROUPS, SSM_STATE).transpose(2, 1, 0, 3, 4)
    h_fin = h_fin.reshape(2, 1, bsz, SSM_GROUPS, SSM_STATE)
    tm = 256
    tile = pl.BlockSpec((tm, d), lambda i: (i, 0))
    x1 = _post_a(y.reshape(m, d), gate_a, x2, p['w_glu'].astype(BF16), p['b_glu'], p['w_out_a'].astype(BF16),
                 p['g_post_a'], None, (m // tm,), tile, tile)
    k, v, gate_b, qt, ka, vta, km, stats = _pre_b(x1, p['g_kv'], p['w_kv'].astype(BF16), p['g_pre_b'],
                                                  p['w_in_b'].astype(BF16), MOBA_BLOCK, None, (bsz, nb))
    km = km.reshape(bsz, nb, KV_HEADS, HEAD_DIM).transpose(0, 2, 1, 3)
    att = _moba_prompt(qt, gate_b.reshape(bsz, seqlen, d), ka, vta, km, _first_needed_pair(stats, bsz, nb))
    y_out = _out_b(att.reshape(m, d), None, x1, p['w_out_b'].astype(BF16), p['g_post_b'], 512, None)
    to_out = lambda t: t.reshape(bsz, KV_HEADS, HEAD_DIM, seqlen).transpose(0, 3, 1, 2)
    return (y_out.reshape(bsz, seqlen, d), h_fin[0], h_fin[1], to_out(k), to_out(v))


def _sample_trunk(x, h0_re, h0_im, cache_k, cache_v, page_table, p):
    n_seq, seqlen, d = x.shape
    assert seqlen == 1
    g, pch = SSM_GROUPS, SSM_GROUP
    n_past = page_table.shape[1] // PAGES_PER_BLOCK
    x2 = x.reshape(n_seq, d)
    whole = pl.BlockSpec((n_seq, d), lambda i: (0, 0))
    u, gate_a = _norm_matmul(x2, p['g_pre_a'], p['w_in_a'], HIGHEST, (1,), whole, [whole, whole],
                             [jax.ShapeDtypeStruct((n_seq, d), F32)] * 2)
    ar, ai, bb_re, bb_im = _s5_discretize(p['lambda_re'], p['lambda_im'], p['log_dt'], p['b_re'], p['b_im'])
    y_g, hr, hi = _s5_step(u.reshape(n_seq, g, pch).transpose(1, 0, 2),
                           h0_re.transpose(1, 0, 2), h0_im.transpose(1, 0, 2),
                           ar, ai, bb_re, bb_im, p['c_re'], p['c_im'], p['d_skip'])
    y = y_g.transpose(1, 0, 2).reshape(n_seq, d)
    x1 = _post_a(y, gate_a, x2, p['w_glu'], p['b_glu'], p['w_out_a'], p['g_post_a'], HIGHEST, (1,), whole, whole)
    k, v, gate_b, q = _pre_b(x1, p['g_kv'], p['w_kv'], p['g_pre_b'], p['w_in_b'], n_seq, HIGHEST)
    cache_kt = cache_k.transpose(0, 2, 3, 1)
    cache_vt = cache_v.transpose(0, 2, 3, 1)
    km_t = _block_means(cache_kt, page_table, n_past)
    idx = _sample_topk(q.reshape(n_seq, KV_HEADS, Q_PER_KV, HEAD_DIM), km_t)[:, :, :MOBA_TOPK].reshape(-1)
    att = _moba_sample(q.reshape(n_seq, N_HEADS, 1, HEAD_DIM), k.reshape(n_seq, KV_HEADS, 1, HEAD_DIM),
                       v.reshape(n_seq, KV_HEADS, 1, HEAD_DIM), cache_kt, cache_vt, page_table, idx, n_past)
    y_out = _out_b(att.reshape(n_seq, d), gate_b, x1, p['w_out_b'], p['g_post_b'], n_seq, HIGHEST)
    return (y_out.reshape(n_seq, 1, d), hr.transpose(1, 0, 2)[None], hi.transpose(1, 0, 2)[None],
            k.reshape(n_seq, 1, KV_HEADS, HEAD_DIM), v.reshape(n_seq, 1, KV_HEADS, HEAD_DIM))


def kernel(x_prompt, x_sample, state_ssm_re, state_ssm_im, cache_k, cache_v, page_table, g_pre_a, w_in_a, lambda_re, lambda_im, log_dt, b_re, b_im, c_re, c_im, d_skip, w_glu, b_glu, w_out_a, g_post_a, g_kv, w_kv, g_pre_b, w_in_b, w_out_b, g_post_b):
    p = {'g_pre_a': g_pre_a[0], 'w_in_a': w_in_a[0], 'lambda_re': lambda_re[0], 'lambda_im': lambda_im[0],
         'log_dt': log_dt[0], 'b_re': b_re[0], 'b_im': b_im[0], 'c_re': c_re[0], 'c_im': c_im[0],
         'd_skip': d_skip[0], 'w_glu': w_glu[0], 'b_glu': b_glu[0], 'w_out_a': w_out_a[0],
         'g_post_a': g_post_a[0], 'g_kv': g_kv, 'w_kv': w_kv, 'g_pre_b': g_pre_b[0],
         'w_in_b': w_in_b[0], 'w_out_b': w_out_b[0], 'g_post_b': g_post_b[0]}
    y_p, re_p, im_p, k_p, v_p = _prompt_trunk(x_prompt, p)
    y_s, re_s, im_s, k_s, v_s = _sample_trunk(x_sample, state_ssm_re[0], state_ssm_im[0],
                                              cache_k, cache_v, page_table, p)
    return (y_p, y_s, re_p, im_p, k_p, v_p, re_s, im_s, k_s, v_s)
```

```python
import functools
import math

import jax
import jax.numpy as jnp
from jax import lax
from jax.experimental import pallas as pl
from jax.experimental.pallas import tpu as pltpu

F32 = jnp.float32
BF16 = jnp.bfloat16
HIGHEST = lax.Precision.HIGHEST

D_MODEL = 1024
SSM_GROUP = 16
SSM_GROUPS = D_MODEL // SSM_GROUP
SSM_STATE = 64
HEAD_DIM = 64
N_HEADS = D_MODEL // HEAD_DIM
KV_HEADS = 4
Q_PER_KV = N_HEADS // KV_HEADS
KV_WIDTH = KV_HEADS * HEAD_DIM
MOBA_BLOCK = 256
MOBA_TOPK = 3
PAGE_SIZE = 128
PAGES_PER_BLOCK = MOBA_BLOCK // PAGE_SIZE
EPS = 1e-6
NEG_INF = -1e30
REMOVED = -3e38
SCALE = HEAD_DIM ** -0.5
S5_CHUNK = 16
LANES = 128
VMEM_LIMIT = 56 * 1024 * 1024


def _cparams(n_grid):
    return pltpu.CompilerParams(dimension_semantics=("arbitrary",) * n_grid,
                                vmem_limit_bytes=VMEM_LIMIT)


def _dot(a, b, prec=None):
    return jnp.dot(a, b, preferred_element_type=F32, precision=prec)


def _dot_nt(a, b, prec=None):
    return lax.dot_general(a, b, (((1,), (1,)), ((), ())),
                           preferred_element_type=F32, precision=prec)


def _rms_scale(x):
    return x * lax.rsqrt(jnp.mean(x * x, axis=-1, keepdims=True) + EPS)


def _sigmoid(x):
    return 1.0 / (1.0 + jnp.exp(-x))


def _full(shape):
    zeros = (0,) * len(shape)
    return pl.BlockSpec(shape, lambda *_: zeros)


def _tile(ref):
    x = ref[...]
    return x.reshape(x.shape[-2:])


def _norm_matmul_body(x_ref, g_ref, w_ref, *out_refs, prec):
    h = (_rms_scale(_tile(x_ref)) * g_ref[...]).astype(w_ref.dtype)
    off = 0
    for o_ref in out_refs:
        n = o_ref.shape[-1]
        o_ref[...] = _dot(h, w_ref[:, off:off + n], prec).astype(o_ref.dtype).reshape(o_ref.shape)
        off += n


def _norm_matmul(x, g, w, prec, grid, x_spec, out_specs, out_shape):
    d, n = w.shape
    return pl.pallas_call(
        functools.partial(_norm_matmul_body, prec=prec),
        grid=grid,
        in_specs=[x_spec, _full((1, d)), _full((d, n))],
        out_specs=out_specs,
        out_shape=out_shape,
        compiler_params=_cparams(len(grid)),
        name="norm_matmul",
    )(x, g.reshape(1, d), w)


S5_LANE_GROUPS = LANES // SSM_GROUP
S5_STATE_LANES = S5_LANE_GROUPS * SSM_STATE
S5_PAIR = 2 * LANES
S5_MC_ROWS = LANES


def _iota2(shape, dim):
    return lax.broadcasted_iota(jnp.int32, shape, dim)


def _expand_block_diagonal(mc_ref, sc_ref, rc_ref, m_s, s_s, r_s):
    p, n, lg = SSM_GROUP, SSM_STATE, S5_LANE_GROUPS
    sl = S5_STATE_LANES
    grp = lambda idx, per: (idx // per) % lg
    rows, cols = s_s.shape
    i, j = _iota2((2 * n, cols), 0), _iota2((2 * n, cols), 1)
    spread = (i == (j // sl) * n + j % n).astype(F32).astype(BF16)
    blk = 256
    for r0 in range(0, rows, blk):
        full = _dot(sc_ref[0, r0:r0 + blk, :], spread)
        keep = grp(_iota2((blk, cols), 0) + r0, p) == grp(_iota2((blk, cols), 1), n)
        s_s[r0:r0 + blk, :] = jnp.where(keep, full, 0.0).astype(BF16)
    rows, cols = r_s.shape
    i, j = _iota2((rows, 2 * n), 0), _iota2((rows, 2 * n), 1)
    spread = (j == (i // sl) * n + i % n).astype(F32).astype(BF16)
    for c0 in range(0, cols, blk):
        full = _dot(spread, rc_ref[0, :, c0:c0 + blk])
        keep = grp(_iota2((rows, blk), 0), n) == grp(_iota2((rows, blk), 1) + c0, p)
        r_s[:, c0:c0 + blk] = jnp.where(keep, full, 0.0).astype(BF16)
    n_delta, rows, cols = m_s.shape
    i, j = _iota2((rows, S5_MC_ROWS), 0), _iota2((rows, S5_MC_ROWS), 1)
    spread = (j == (i // LANES) * p + i % p).astype(F32).astype(BF16)
    keep = grp(_iota2((rows, cols), 0), p) == grp(_iota2((rows, cols), 1), p)
    for dl in range(n_delta):
        m_s[dl] = jnp.where(keep, _dot(spread, mc_ref[0, dl]), 0.0).astype(BF16)


def _s5_prompt_body(u_ref, mc_ref, sc_ref, rc_ref, a_ref, d_ref, y_ref, hfin_ref,
                    m_s, s_s, r_s, e_s, hin_s, h_s):
    bsz, tokens, _ = u_ref.shape
    t_chunk = S5_CHUNK
    rc = tokens // t_chunk
    sl = S5_STATE_LANES

    @pl.when(pl.program_id(1) == 0)
    def _():
        h_s[...] = jnp.zeros(h_s.shape, F32)
        _expand_block_diagonal(mc_ref, sc_ref, rc_ref, m_s, s_s, r_s)

    token = lambda ref, b, s: ref.at[b, pl.ds(s, rc, stride=t_chunk), :]
    xs = [jnp.concatenate([token(u_ref, b, s)[...] for b in range(bsz)], axis=0) for s in range(t_chunk)]
    xcat = jnp.concatenate([x.astype(BF16) for x in xs], axis=1)
    half = sl // LANES
    cols = lambda j: slice(j * LANES, (j + 1) * LANES)
    e = _dot(xcat, s_s[...])
    for j in range(2 * half):
        e_s[j] = e[:, cols(j)]
    ar = [jnp.broadcast_to(a_ref[0, 0:1, cols(j)], (bsz, LANES)) for j in range(half)]
    ai = [jnp.broadcast_to(a_ref[0, 1:2, cols(j)], (bsz, LANES)) for j in range(half)]

    def step(c, carry):
        hr, hi = carry
        at_c = pl.ds(c, bsz, stride=rc)
        new_r, new_i = [], []
        for j in range(half):
            hin_s[j, at_c, :] = hr[j]
            hin_s[half + j, at_c, :] = hi[j]
            new_r.append(ar[j] * hr[j] - ai[j] * hi[j] + e_s[j, at_c, :])
            new_i.append(ar[j] * hi[j] + ai[j] * hr[j] + e_s[half + j, at_c, :])
        return tuple(new_r), tuple(new_i)

    carry0 = (tuple(h_s[0, j] for j in range(half)), tuple(h_s[1, j] for j in range(half)))
    hr, hi = lax.fori_loop(0, rc, step, carry0)
    for j in range(half):
        h_s[0, j] = hr[j]
        h_s[1, j] = hi[j]
        hfin_ref[0, :, cols(j)] = hr[j]
        hfin_ref[0, :, cols(half + j)] = hi[j]
    hin = jnp.concatenate([hin_s[j] for j in range(2 * half)], axis=1).astype(BF16)
    d_row = d_ref[0]
    for i in range(t_chunk // 2):
        acc = _dot(hin, r_s[:, i * S5_PAIR:(i + 1) * S5_PAIR])
        for j in range(i + 1):
            acc += _dot(xcat[:, j * S5_PAIR:(j + 1) * S5_PAIR], m_s[i - j])
        for tt in range(2):
            t = 2 * i + tt
            y_t = acc[:, tt * LANES:(tt + 1) * LANES] + d_row * xs[t]
            for b in range(bsz):
                token(y_ref, b, t)[...] = y_t[b * rc:(b + 1) * rc]


def _s5_prompt(u, ops, rc):
    bsz, seqlen, d = u.shape
    n_lb = d // LANES
    mc, sc, rc_op, a16, dl = ops
    sl = S5_STATE_LANES
    t_chunk = S5_CHUNK
    tokens = rc * t_chunk
    act = pl.BlockSpec((bsz, tokens, LANES), lambda lb, ct: (0, ct, lb))
    per_lb = lambda shape: pl.BlockSpec((1,) + shape, lambda lb, ct: (lb,) + (0,) * len(shape))
    return pl.pallas_call(
        _s5_prompt_body,
        grid=(n_lb, seqlen // tokens),
        in_specs=[act, per_lb(mc.shape[1:]), per_lb(sc.shape[1:]), per_lb(rc_op.shape[1:]),
                  per_lb((2, sl)), per_lb((1, LANES))],
        out_specs=[act, per_lb((bsz, 2 * sl))],
        out_shape=[jax.ShapeDtypeStruct(u.shape, F32),
                   jax.ShapeDtypeStruct((n_lb, bsz, 2 * sl), F32)],
        scratch_shapes=[pltpu.VMEM((t_chunk // 2, S5_PAIR, S5_PAIR), BF16),
                        pltpu.VMEM((t_chunk * LANES, 2 * sl), BF16),
                        pltpu.VMEM((2 * sl, t_chunk * LANES), BF16),
                        pltpu.VMEM((2 * sl // LANES, bsz * rc, LANES), F32),
                        pltpu.VMEM((2 * sl // LANES, bsz * rc, LANES), F32),
                        pltpu.VMEM((2, sl // LANES, bsz, LANES), F32)],
        compiler_params=_cparams(2),
        name="s5_prompt",
    )(u, mc, sc, rc_op, a16, dl)


def _s5_discretize(lam_re, lam_im, log_dt, b_re, b_im):
    dt = jnp.exp(log_dt)[:, None]
    mag = jnp.exp(lam_re * dt)
    ar = mag * jnp.cos(lam_im * dt)
    ai = mag * jnp.sin(lam_im * dt)
    den = lam_re * lam_re + lam_im * lam_im
    nr = ar - 1.0
    coef_re = (nr * lam_re + ai * lam_im) / den
    coef_im = (ai * lam_re - nr * lam_im) / den
    bb_re = coef_re[..., None] * b_re - coef_im[..., None] * b_im
    bb_im = coef_re[..., None] * b_im + coef_im[..., None] * b_re
    return ar, ai, bb_re, bb_im


def _s5_prompt_operators(ar, ai, bb_re, bb_im, c_re, c_im, d_skip):
    t = S5_CHUNK
    g, n = ar.shape
    p = SSM_GROUP
    lg = S5_LANE_GROUPS
    n_lb = g // lg
    pr, pi = [jnp.ones_like(ar)], [jnp.zeros_like(ar)]
    for _ in range(t):
        pr, pi = pr + [pr[-1] * ar - pi[-1] * ai], pi + [pr[-1] * ai + pi[-1] * ar]
    pr = jnp.stack(pr)
    pi = jnp.stack(pi)
    cp_re = c_re[None] * pr[:, :, None, :] - c_im[None] * pi[:, :, None, :]
    cp_im = c_re[None] * pi[:, :, None, :] + c_im[None] * pr[:, :, None, :]
    k = (jnp.einsum('tgpn,gnq->gtpq', cp_re[:t], bb_re, precision=HIGHEST)
         - jnp.einsum('tgpn,gnq->gtpq', cp_im[:t], bb_im, precision=HIGHEST))
    k_lag = lambda lag: k[:, lag] if lag >= 0 else jnp.zeros_like(k[:, 0])
    kt = jnp.stack([jnp.stack([jnp.stack([k_lag(2 * dl + tt - ss) for tt in range(2)], axis=1)
                               for ss in range(2)], axis=1) for dl in range(t // 2)], axis=1)
    kt = kt.reshape(n_lb, lg, t // 2, 2, 2, p, p)
    m = kt.transpose(0, 2, 3, 6, 4, 1, 5).reshape(n_lb, t // 2, 2 * p, 2 * lg * p)
    m = jnp.pad(m, ((0, 0), (0, 0), (0, S5_MC_ROWS - 2 * p), (0, 0)))
    pr_rev = jnp.stack([pr[t - 1 - s] for s in range(t)])
    pi_rev = jnp.stack([pi[t - 1 - s] for s in range(t)])
    s_re = pr_rev[:, :, :, None] * bb_re[None] - pi_rev[:, :, :, None] * bb_im[None]
    s_im = pr_rev[:, :, :, None] * bb_im[None] + pi_rev[:, :, :, None] * bb_re[None]
    s_ri = jnp.stack([s_re, s_im]).reshape(2, t, n_lb, lg, n, p)
    s = s_ri.transpose(2, 1, 3, 5, 0, 4).reshape(n_lb, t * lg * p, 2 * n)
    r_ri = jnp.stack([cp_re[1:], -cp_im[1:]]).reshape(2, t, n_lb, lg, p, n)
    r = r_ri.transpose(2, 0, 5, 1, 3, 4).reshape(n_lb, 2 * n, t * lg * p)
    a16 = jnp.stack([pr[t].reshape(n_lb, lg * n), pi[t].reshape(n_lb, lg * n)], axis=1)
    return (m.astype(BF16), s.astype(BF16), r.astype(BF16), a16, d_skip.reshape(n_lb, 1, lg * p))


def _s5_step_body(u_ref, h0r_ref, h0i_ref, ar_ref, ai_ref, bbr_ref, bbi_ref, cr_ref, ci_ref,
                  d_ref, y_ref, hr_ref, hi_ref):
    u = u_ref[0]
    ar, ai = ar_ref[0], ai_ref[0]
    h0r, h0i = h0r_ref[0], h0i_ref[0]
    hr = _dot_nt(u, bbr_ref[0], HIGHEST) + ar * h0r - ai * h0i
    hi = _dot_nt(u, bbi_ref[0], HIGHEST) + ar * h0i + ai * h0r
    hr_ref[0] = hr
    hi_ref[0] = hi
    y_ref[0] = (_dot_nt(hr, cr_ref[0], HIGHEST) - _dot_nt(hi, ci_ref[0], HIGHEST)
                + d_ref[0] * u)


def _s5_step(u_g, h0r, h0i, ar, ai, bb_re, bb_im, c_re, c_im, d_skip):
    g, b, p = u_g.shape
    n = ar.shape[-1]
    per_group = lambda shape: pl.BlockSpec((1,) + shape, lambda i: (i, 0, 0))
    return pl.pallas_call(
        _s5_step_body,
        grid=(g,),
        in_specs=[per_group((b, p)), per_group((b, n)), per_group((b, n)),
                  per_group((1, n)), per_group((1, n)),
                  per_group((n, p)), per_group((n, p)),
                  per_group((p, n)), per_group((p, n)), per_group((1, p))],
        out_specs=[per_group((b, p)), per_group((b, n)), per_group((b, n))],
        out_shape=[jax.ShapeDtypeStruct((g, b, p), F32),
                   jax.ShapeDtypeStruct((g, b, n), F32),
                   jax.ShapeDtypeStruct((g, b, n), F32)],
        compiler_params=_cparams(1),
        name="s5_step",
    )(u_g, h0r, h0i, ar.reshape(g, 1, n), ai.reshape(g, 1, n), bb_re, bb_im, c_re, c_im,
      d_skip.reshape(g, 1, p))


def _post_a_body(y_ref, ga_ref, x_ref, wglu_ref, bglu_ref, wout_ref, gpost_ref, x1_ref, *, prec):
    y = _tile(y_ref)
    wdt = wglu_ref.dtype
    gy = 0.5 * y * (1.0 + lax.erf(y * math.sqrt(0.5)))
    z = _dot(gy.astype(wdt), wglu_ref[...], prec) + bglu_ref[...]
    gate = _tile(ga_ref)
    t = (gy * _sigmoid(z)) * (gate * _sigmoid(gate))
    o = _dot(t.astype(wdt), wout_ref[...], prec)
    x1_ref[...] = (_tile(x_ref) + _rms_scale(o) * gpost_ref[...]).reshape(x1_ref.shape)


def _post_a(y, gate, x, w_glu, b_glu, w_out, g_post, prec, grid, y_spec, x_spec):
    d = w_glu.shape[0]
    return pl.pallas_call(
        functools.partial(_post_a_body, prec=prec),
        grid=grid,
        in_specs=[y_spec, x_spec, x_spec, _full((d, d)), _full((1, d)), _full((d, d)), _full((1, d))],
        out_specs=x_spec,
        out_shape=jax.ShapeDtypeStruct(x.shape, F32),
        compiler_params=_cparams(len(grid)),
        name="post_a",
    )(y, gate, x, w_glu, b_glu.reshape(1, d), w_out, g_post.reshape(1, d))


ALIBI_ROWS = 16
FEAT_OFF = HEAD_DIM
V_ROWS = HEAD_DIM + 16


def _aug_width(nb):
    nbp = -(-nb // 16) * 16
    return -(-(FEAT_OFF + ALIBI_ROWS + nbp) // LANES) * LANES, nbp


def _pre_b_body(x1_ref, gkv_ref, wkv_ref, gb_ref, winb_ref, k_ref, v_ref, gate_ref, *rest, prec, nb):
    xn = _rms_scale(x1_ref[...])
    wdt = wkv_ref.dtype
    hk = (xn * gkv_ref[...]).astype(wdt)
    k = _dot(hk, wkv_ref[:, :KV_WIDTH], prec)
    v = _dot(hk, wkv_ref[:, KV_WIDTH:], prec)
    hq = (xn * gb_ref[...]).astype(wdt)
    q = _dot(hq, winb_ref[:, :D_MODEL], prec) * SCALE
    gate_ref[...] = _dot(hq, winb_ref[:, D_MODEL:], prec)
    if nb is None:
        q_ref, = rest
        q_ref[...] = q
        k_ref[...] = k
        v_ref[...] = v
        return
    qt_ref, ka_ref, vta_ref, km_ref, st_ref = rest
    tm = q.shape[0]
    qt_ref[0, 0] = q.T.astype(BF16)
    km_ref[0] = jnp.mean(k, axis=0, keepdims=True)
    pick = lambda width: (_iota2((width, LANES), 0) // HEAD_DIM == _iota2((width, LANES), 1)).astype(F32).astype(BF16)
    sq_norms = lambda x: jnp.max(_dot((x * x).astype(BF16), pick(x.shape[1])), axis=0, keepdims=True)
    st_ref[0, 0:1, :] = sq_norms(q)
    st_ref[0, 1:2, :] = sq_norms(k)
    n = pl.program_id(0) % nb
    width = ka_ref.shape[-1] - FEAT_OFF
    lane = lax.broadcasted_iota(jnp.int32, (tm, width), 1)
    rowi = lax.broadcasted_iota(jnp.int32, (tm, width), 0)
    feat = jnp.where(lane < 3, n, jnp.where(lane < 6, rowi, jnp.where(lane < 9, 1, 0)))
    feat = jnp.where(lane - ALIBI_ROWS == n, 1, feat).astype(F32).astype(BF16)
    kb = k.astype(BF16)
    vt = v.T
    k_ref[0] = k.T
    v_ref[0] = vt
    ones_row = (lax.broadcasted_iota(jnp.int32, (V_ROWS - HEAD_DIM, tm), 0) == 0).astype(F32).astype(BF16)
    for h in range(KV_HEADS):
        ka_ref[0, h, :, :FEAT_OFF] = kb[:, h * HEAD_DIM:(h + 1) * HEAD_DIM]
        ka_ref[0, h, :, FEAT_OFF:] = feat
        vta_ref[0, 0, h * V_ROWS:h * V_ROWS + HEAD_DIM, :] = vt[h * HEAD_DIM:(h + 1) * HEAD_DIM].astype(BF16)
        vta_ref[0, 0, h * V_ROWS + HEAD_DIM:(h + 1) * V_ROWS, :] = ones_row


def _pre_b(x1, g_kv, w_kv, g_pre_b, w_in_b, tm, prec, prompt_blocks=None):
    m, d = x1.shape
    row = lambda wd: pl.BlockSpec((tm, wd), lambda i: (i, 0))
    nb = None
    if prompt_blocks is None:
        kv_spec = row(KV_WIDTH)
        kv_shape = jax.ShapeDtypeStruct((m, KV_WIDTH), F32)
        out_specs = [kv_spec, kv_spec, row(d), row(d)]
        out_shape = [kv_shape, kv_shape, jax.ShapeDtypeStruct((m, d), F32), jax.ShapeDtypeStruct((m, d), F32)]
    else:
        bsz, nb = prompt_blocks
        assert tm == MOBA_BLOCK and m == bsz * nb * tm
        aug, _ = _aug_width(nb)
        kv_spec = pl.BlockSpec((1, KV_WIDTH, tm), lambda i: (i // nb, 0, i % nb))
        kv_shape = jax.ShapeDtypeStruct((bsz, KV_WIDTH, nb * tm), F32)
        out_specs = [kv_spec, kv_spec, row(d)]
        out_shape = [kv_shape, kv_shape, jax.ShapeDtypeStruct((m, d), F32)]
        out_specs += [pl.BlockSpec((1, 1, d, tm), lambda i: (i // nb, i % nb, 0, 0)),
                      pl.BlockSpec((1, KV_HEADS, tm, aug), lambda i: (i // nb, 0, i % nb, 0)),
                      pl.BlockSpec((1, 1, KV_HEADS * V_ROWS, tm), lambda i: (i // nb, i % nb, 0, 0)),
                      pl.BlockSpec((1, 1, KV_WIDTH), lambda i: (i, 0, 0)),
                      pl.BlockSpec((1, 2, LANES), lambda i: (i, 0, 0))]
        out_shape += [jax.ShapeDtypeStruct((bsz, nb, d, tm), BF16),
                      jax.ShapeDtypeStruct((bsz, KV_HEADS, nb * tm, aug), BF16),
                      jax.ShapeDtypeStruct((bsz, nb, KV_HEADS * V_ROWS, tm), BF16),
                      jax.ShapeDtypeStruct((bsz * nb, 1, KV_WIDTH), F32),
                      jax.ShapeDtypeStruct((bsz * nb, 2, LANES), F32)]
    return pl.pallas_call(
        functools.partial(_pre_b_body, prec=prec, nb=nb),
        grid=(m // tm,),
        in_specs=[row(d), _full((1, d)), _full((d, 2 * KV_WIDTH)), _full((1, d)), _full((d, 2 * d))],
        out_specs=out_specs,
        out_shape=out_shape,
        compiler_params=_cparams(1),
        name="pre_b",
    )(x1, g_kv.reshape(1, d), w_kv, g_pre_b.reshape(1, d), w_in_b)


def _top3_mask(gs, idx, valid, axis):
    n = gs.shape[axis]
    gs = jnp.where(valid, gs, NEG_INF)
    sel = jnp.zeros(gs.shape, F32)
    picks = []
    for _ in range(MOBA_TOPK):
        mx = jnp.max(gs, axis=axis, keepdims=True)
        first = jnp.min(jnp.where(gs == mx, idx, n), axis=axis, keepdims=True)
        pick = idx == first
        sel = jnp.where(pick & valid, 1.0, sel)
        gs = jnp.where(pick, REMOVED, gs)
        picks.append(first)
    return sel, picks


def _split3(x):
    hi = x.astype(BF16).astype(F32)
    mid = (x - hi).astype(BF16).astype(F32)
    lo = (x - hi - mid).astype(BF16).astype(F32)
    return [hi, mid, lo]


def _moba_prompt_body(first_ref, qt_ref, g_ref, ka_ref, vta_ref, km_ref, o_ref, qa_s, qd_s):
    h = pl.program_id(1)
    c = pl.program_id(2)
    blk = MOBA_BLOCK
    nb = km_ref.shape[2]
    aug, nbp = _aug_width(nb)
    km = km_ref[0, 0]
    row = lax.broadcasted_iota(jnp.int32, (nb, blk), 0)
    frow = lax.broadcasted_iota(jnp.int32, (ALIBI_ROWS, blk), 0)
    pos_q = (c * blk + lax.broadcasted_iota(jnp.int32, (1, blk), 1)).astype(F32)
    for g in range(Q_PER_KV):
        lanes = slice(g * blk, (g + 1) * blk)
        qg = qt_ref[0, 0, g * HEAD_DIM:(g + 1) * HEAD_DIM, :]
        sel, _ = _top3_mask(_dot(km, qg.astype(F32), HIGHEST), row, row < c, 0)
        bias = jnp.where(sel > 0.0, 0.0, NEG_INF)
        if nbp > nb:
            bias = jnp.concatenate([bias, jnp.zeros((nbp - nb, blk), F32)], axis=0)
        head1 = (h * Q_PER_KV + g + 1).astype(F32)
        slope = jnp.exp2(-8.0 * jnp.full((1, blk), head1, F32) / N_HEADS)
        terms = _split3(slope * float(blk)) + _split3(slope) + _split3(-slope * pos_q)
        feat = jnp.zeros((ALIBI_ROWS, blk), F32)
        for r, term in enumerate(terms):
            feat = jnp.where(frow == r, term, feat)
        for ref, b_rows in ((qa_s, bias.astype(BF16)), (qd_s, jnp.zeros((nbp, blk), BF16))):
            ref[:FEAT_OFF, lanes] = qg
            ref[FEAT_OFF:FEAT_OFF + ALIBI_ROWS, lanes] = feat.astype(BF16)
            ref[FEAT_OFF + ALIBI_ROWS:FEAT_OFF + ALIBI_ROWS + nbp, lanes] = b_rows
            if FEAT_OFF + ALIBI_ROWS + nbp < aug:
                ref[FEAT_OFF + ALIBI_ROWS + nbp:, lanes] = jnp.zeros((aug - FEAT_OFF - ALIBI_ROWS - nbp, blk), BF16)

    def keys(n):
        return ka_ref[0, 0, pl.ds(pl.multiple_of(n * blk, blk), blk), :]

    s = _dot(keys(c), qd_s[...])
    key = lax.broadcasted_iota(jnp.int32, s.shape, 0)
    qry = lax.broadcasted_iota(jnp.int32, s.shape, 1) & (blk - 1)
    s = jnp.where(key <= qry, s, NEG_INF)
    m = jnp.max(s, axis=0, keepdims=True)
    acc = _dot(vta_ref[0, c], jnp.exp(s - m).astype(BF16))

    def block_pair(i, carry):
        m, acc = carry
        n0, n1 = 2 * i, 2 * i + 1
        s0 = _dot(keys(n0), qa_s[...])
        s1 = _dot(keys(n1), qa_s[...])
        m2 = jnp.maximum(m, jnp.maximum(jnp.max(s0, axis=0, keepdims=True), jnp.max(s1, axis=0, keepdims=True)))
        p0 = jnp.exp(s0 - m2).astype(BF16)
        p1 = jnp.exp(s1 - m2).astype(BF16)
        return m2, jnp.exp(m - m2) * acc + _dot(vta_ref[0, n0], p0) + _dot(vta_ref[0, n1], p1)

    first = first_ref[(pl.program_id(0) * KV_HEADS + h) * nb + c]
    m, acc = lax.fori_loop(first, (c + 1) // 2, block_pair, (m, acc))
    out_t = acc[:HEAD_DIM] / acc[HEAD_DIM:HEAD_DIM + 1]
    att = jnp.concatenate([out_t[:, g * blk:(g + 1) * blk] for g in range(Q_PER_KV)], axis=0).T
    gate = g_ref[0]
    o_ref[0] = (att * (gate * _sigmoid(gate))).astype(o_ref.dtype)


_SKIP_LOGIT = -120.0
_NORM_SLACK = 1.02


def _first_needed_pair(stats, bsz, nb):
    q_max = jnp.sqrt(stats[:, 0, :N_HEADS]).reshape(bsz, nb, KV_HEADS, Q_PER_KV) * _NORM_SLACK
    k_max = jnp.sqrt(stats[:, 1, :KV_HEADS]).reshape(bsz, nb, KV_HEADS) * _NORM_SLACK
    slopes = jnp.exp2(-8.0 * jnp.arange(1, N_HEADS + 1, dtype=F32) / N_HEADS).reshape(KV_HEADS, Q_PER_KV)
    c = jnp.arange(nb)[:, None]
    n = jnp.arange(nb)[None, :]
    min_dist = ((c - n) * MOBA_BLOCK - (MOBA_BLOCK - 1)).astype(F32)
    k_pair = k_max[:, None, :, :] + k_max[:, :, None, :]
    bound = (q_max[:, :, None, :, :] * k_pair[..., None]
             - slopes[None, None, None] * min_dist[None, :, :, None, None])
    dead = jnp.all(bound < _SKIP_LOGIT, axis=-1) & (n < c)[None, :, :, None]
    lead = jnp.min(jnp.where(dead, nb, n[None, :, :, None]), axis=2)
    return (lead // 2).transpose(0, 2, 1).reshape(-1).astype(jnp.int32)


def _moba_prompt(qt, gate, ka, vta, km, first_pair):
    bsz, seqlen, d = gate.shape
    nb = seqlen // MOBA_BLOCK
    width = Q_PER_KV * HEAD_DIM
    aug, _ = _aug_width(nb)
    tile = pl.BlockSpec((1, MOBA_BLOCK, width), lambda b, h, c, fp: (b, c, h))
    return pl.pallas_call(
        _moba_prompt_body,
        grid_spec=pltpu.PrefetchScalarGridSpec(
            num_scalar_prefetch=1,
            grid=(bsz, KV_HEADS, nb),
            in_specs=[pl.BlockSpec((1, 1, width, MOBA_BLOCK), lambda b, h, c, fp: (b, c, h, 0)),
                      tile,
                      pl.BlockSpec((1, 1, seqlen, aug), lambda b, h, c, fp: (b, h, 0, 0)),
                      pl.BlockSpec((1, nb, V_ROWS, MOBA_BLOCK), lambda b, h, c, fp: (b, 0, h, 0)),
                      pl.BlockSpec((1, 1, nb, HEAD_DIM), lambda b, h, c, fp: (b, h, 0, 0))],
            out_specs=tile,
            scratch_shapes=[pltpu.VMEM((aug, Q_PER_KV * MOBA_BLOCK), BF16) for _ in range(2)]),
        out_shape=jax.ShapeDtypeStruct((bsz, seqlen, d), BF16),
        compiler_params=_cparams(3),
        name="moba_prompt",
    )(first_pair, qt, gate, ka, vta, km)


_KM_BLOCKS = 16


def _block_mean_body(pt_ref, *refs):
    del pt_ref
    pages, o_ref = refs[:-1], refs[-1]
    s = pl.program_id(1)

    @pl.when(s == 0)
    def _():
        o_ref[...] = jnp.zeros(o_ref.shape, F32)

    lane = lax.broadcasted_iota(jnp.int32, o_ref.shape[1:], 2)
    acc = o_ref[0]
    for j in range(_KM_BLOCKS):
        tot = sum(pages[PAGES_PER_BLOCK * j + r][0] for r in range(PAGES_PER_BLOCK))
        mean = jnp.sum(tot, axis=-1, keepdims=True) * (1.0 / MOBA_BLOCK)
        acc = jnp.where(lane == s * _KM_BLOCKS + j, mean, acc)
    o_ref[0] = acc


def _block_means(cache_t, page_table, n_blocks):
    n_seq = page_table.shape[0]
    assert n_blocks % _KM_BLOCKS == 0
    per_step = _KM_BLOCKS * PAGES_PER_BLOCK
    page_spec = lambda j: pl.BlockSpec(
        (1, KV_HEADS, HEAD_DIM, PAGE_SIZE), lambda b, s, pt, j=j: (pt[b, s * per_step + j], 0, 0, 0))
    return pl.pallas_call(
        _block_mean_body,
        grid_spec=pltpu.PrefetchScalarGridSpec(
            num_scalar_prefetch=1,
            grid=(n_seq, n_blocks // _KM_BLOCKS),
            in_specs=[page_spec(j) for j in range(per_step)],
            out_specs=pl.BlockSpec((1, KV_HEADS, HEAD_DIM, n_blocks), lambda b, s, pt: (b, 0, 0, 0))),
        out_shape=jax.ShapeDtypeStruct((n_seq, KV_HEADS, HEAD_DIM, n_blocks), F32),
        compiler_params=_cparams(2),
        name="block_means",
    )(page_table, *([cache_t] * per_step))


def _sample_topk_body(q_ref, km_ref, idx_ref, *, n_past):
    gs = jnp.concatenate([_dot(q_ref[0, k], km_ref[0, k], HIGHEST) for k in range(KV_HEADS)], axis=0)
    lane = lax.broadcasted_iota(jnp.int32, gs.shape, 1)
    _, picks = _top3_mask(gs, lane, lane < n_past, 1)
    out_lane = lax.broadcasted_iota(jnp.int32, idx_ref.shape[1:], 1)
    out = jnp.zeros(idx_ref.shape[1:], jnp.int32)
    for r, first in enumerate(picks):
        out = jnp.where(out_lane == r, first, out)
    idx_ref[0] = out


def _sample_topk(q, km_t):
    n_seq, _, _, n_past = km_t.shape
    return pl.pallas_call(
        functools.partial(_sample_topk_body, n_past=n_past),
        grid=(n_seq,),
        in_specs=[pl.BlockSpec((1,) + q.shape[1:], lambda b: (b, 0, 0, 0)),
                  pl.BlockSpec((1,) + km_t.shape[1:], lambda b: (b, 0, 0, 0))],
        out_specs=pl.BlockSpec((1, N_HEADS, LANES), lambda b: (b, 0, 0)),
        out_shape=jax.ShapeDtypeStruct((n_seq, N_HEADS, LANES), jnp.int32),
        compiler_params=_cparams(1),
        name="sample_topk",
    )(q, km_t)


_Q_ROWS = 8


def _moba_sample_body(pt_ref, idx_ref, q_ref, kn_ref, vn_ref, *refs, n_past):
    del pt_ref
    n_sel = MOBA_TOPK * PAGES_PER_BLOCK
    k_pages, v_pages, o_ref = refs[:n_sel], refs[n_sel:2 * n_sel], refs[-1]
    b = pl.program_id(0)
    h = pl.program_id(1)
    keys = MOBA_TOPK * MOBA_BLOCK
    q = jnp.broadcast_to(q_ref[0, 0], (_Q_ROWS, HEAD_DIM))
    kt_sel = jnp.concatenate([r[0, 0] for r in k_pages], axis=1)
    vt_sel = jnp.concatenate([r[0, 0] for r in v_pages], axis=1)
    col = lax.broadcasted_iota(jnp.int32, (_Q_ROWS, keys), 1)
    slot = col // MOBA_BLOCK
    base = (b * N_HEADS + h) * MOBA_TOPK
    idx = jnp.zeros((_Q_ROWS, keys), jnp.int32)
    for r in range(MOBA_TOPK):
        idx = jnp.where(slot == r, idx_ref[base + r], idx)
    pos = n_past * MOBA_BLOCK
    dist = (pos - (idx * MOBA_BLOCK + col % MOBA_BLOCK)).astype(F32)
    slope = jnp.exp2(-8.0 * jnp.full((_Q_ROWS, keys), (h + 1).astype(F32), F32) / N_HEADS)
    s = _dot(q.astype(BF16), kt_sel.astype(BF16)) - slope * dist
    s = jnp.where(idx < n_past, s, NEG_INF)
    s_own = jnp.sum(q * kn_ref[0, 0], axis=-1, keepdims=True)
    m = jnp.maximum(jnp.max(s, axis=-1, keepdims=True), s_own)
    p = jnp.exp(s - m)
    p_own = jnp.exp(s_own - m)
    l = jnp.sum(p, axis=-1, keepdims=True) + p_own
    out = (_dot_nt(p.astype(BF16), vt_sel.astype(BF16)) + p_own * vn_ref[0, 0]) / l
    o_ref[0, 0] = out[0:1]


def _moba_sample(q, k_new, v_new, cache_kt, cache_vt, page_table, idx_flat, n_past):
    n_seq = page_table.shape[0]

    def page_spec(j):
        r, pg = divmod(j, PAGES_PER_BLOCK)

        def index(b, h, pt, idx):
            blk = idx[(b * N_HEADS + h) * MOBA_TOPK + r]
            return (pt[b, blk * PAGES_PER_BLOCK + pg], h // Q_PER_KV, 0, 0)

        return pl.BlockSpec((1, 1, HEAD_DIM, PAGE_SIZE), index)

    n_sel = MOBA_TOPK * PAGES_PER_BLOCK
    row = (1, 1, 1, HEAD_DIM)
    new_spec = pl.BlockSpec(row, lambda b, h, pt, idx: (b, h // Q_PER_KV, 0, 0))
    head_spec = pl.BlockSpec(row, lambda b, h, pt, idx: (b, h, 0, 0))
    return pl.pallas_call(
        functools.partial(_moba_sample_body, n_past=n_past),
        grid_spec=pltpu.PrefetchScalarGridSpec(
            num_scalar_prefetch=2,
            grid=(n_seq, N_HEADS),
            in_specs=[head_spec, new_spec, new_spec] + [page_spec(j) for j in range(n_sel)] * 2,
            out_specs=head_spec),
        out_shape=jax.ShapeDtypeStruct((n_seq, N_HEADS, 1, HEAD_DIM), F32),
        compiler_params=_cparams(2),
        name="moba_sample",
    )(page_table, idx_flat, q, k_new, v_new, *([cache_kt] * n_sel), *([cache_vt] * n_sel))


def _out_b_body(a_ref, *refs, prec, gated):
    if gated:
        x1_ref, w_ref, g_ref, y_ref = refs
        a = a_ref[...]
    else:
        gate_ref, x1_ref, w_ref, g_ref, y_ref = refs
        gate = gate_ref[...]
        a = (a_ref[...] * (gate * _sigmoid(gate))).astype(w_ref.dtype)
    o = _dot(a, w_ref[...], prec)
    y_ref[...] = x1_ref[...] + _rms_scale(o) * g_ref[...]


def _out_b(att, gate, x1, w, g_post, tm, prec):
    m, d = x1.shape
    tile = pl.BlockSpec((tm, d), lambda i: (i, 0))
    gated = gate is None
    acts = [att, x1] if gated else [att, gate, x1]
    return pl.pallas_call(
        functools.partial(_out_b_body, prec=prec, gated=gated),
        grid=(m // tm,),
        in_specs=[tile] * len(acts) + [_full((d, d)), _full((1, d))],
        out_specs=tile,
        out_shape=jax.ShapeDtypeStruct((m, d), F32),
        compiler_params=_cparams(1),
        name="out_b",
    )(*acts, w, g_post.reshape(1, d))


TM_IN_A = 1024
TM_POST_A = 512
TM_OUT_B = 1024
S5_CHUNKS_PER_STEP = 64


def _prompt_trunk(x, p):
    bsz, seqlen, d = x.shape
    m = bsz * seqlen
    nb = seqlen // MOBA_BLOCK
    nc = seqlen // S5_CHUNK
    x2 = x.reshape(m, d)
    tm = TM_IN_A
    tile = pl.BlockSpec((tm, d), lambda i: (i, 0))
    u, gate_a = _norm_matmul(x2, p['g_pre_a'], p['w_in_a'].astype(BF16), None, (m // tm,), tile, [tile, tile],
                             [jax.ShapeDtypeStruct((m, d), F32)] * 2)
    ar, ai, bb_re, bb_im = _s5_discretize(p['lambda_re'], p['lambda_im'], p['log_dt'], p['b_re'], p['b_im'])
    ops = _s5_prompt_operators(ar, ai, bb_re, bb_im, p['c_re'], p['c_im'], p['d_skip'])
    y, h_fin = _s5_prompt(u.reshape(bsz, seqlen, d), ops, min(nc, S5_CHUNKS_PER_STEP))
    h_fin = h_fin.reshape(d // LANES, bsz, 2, S5_LANE_GROUPS, SSM_STATE).transpose(2, 1, 0, 3, 4)
    h_fin = h_fin.reshape(2, 1, bsz, SSM_GROUPS, SSM_STATE)
    tm = TM_POST_A
    tile = pl.BlockSpec((tm, d), lambda i: (i, 0))
    x1 = _post_a(y.reshape(m, d), gate_a, x2, p['w_glu'].astype(BF16), p['b_glu'], p['w_out_a'].astype(BF16),
                 p['g_post_a'], None, (m // tm,), tile, tile)
    k, v, gate_b, qt, ka, vta, km, stats = _pre_b(x1, p['g_kv'], p['w_kv'].astype(BF16), p['g_pre_b'],
                                                  p['w_in_b'].astype(BF16), MOBA_BLOCK, None, (bsz, nb))
    km = km.reshape(bsz, nb, KV_HEADS, HEAD_DIM).transpose(0, 2, 1, 3)
    att = _moba_prompt(qt, gate_b.reshape(bsz, seqlen, d), ka, vta, km, _first_needed_pair(stats, bsz, nb))
    y_out = _out_b(att.reshape(m, d), None, x1, p['w_out_b'].astype(BF16), p['g_post_b'], TM_OUT_B, None)
    to_out = lambda t: t.reshape(bsz, KV_HEADS, HEAD_DIM, seqlen).transpose(0, 3, 1, 2)
    return (y_out.reshape(bsz, seqlen, d), h_fin[0], h_fin[1], to_out(k), to_out(v))


def _sample_trunk(x, h0_re, h0_im, cache_k, cache_v, page_table, p):
    n_seq, seqlen, d = x.shape
    assert seqlen == 1
    g, pch = SSM_GROUPS, SSM_GROUP
    n_past = page_table.shape[1] // PAGES_PER_BLOCK
    x2 = x.reshape(n_seq, d)
    whole = pl.BlockSpec((n_seq, d), lambda i: (0, 0))
    u, gate_a = _norm_matmul(x2, p['g_pre_a'], p['w_in_a'], HIGHEST, (1,), whole, [whole, whole],
                             [jax.ShapeDtypeStruct((n_seq, d), F32)] * 2)
    ar, ai, bb_re, bb_im = _s5_discretize(p['lambda_re'], p['lambda_im'], p['log_dt'], p['b_re'], p['b_im'])
    y_g, hr, hi = _s5_step(u.reshape(n_seq, g, pch).transpose(1, 0, 2),
                           h0_re.transpose(1, 0, 2), h0_im.transpose(1, 0, 2),
                           ar, ai, bb_re, bb_im, p['c_re'], p['c_im'], p['d_skip'])
    y = y_g.transpose(1, 0, 2).reshape(n_seq, d)
    x1 = _post_a(y, gate_a, x2, p['w_glu'], p['b_glu'], p['w_out_a'], p['g_post_a'], HIGHEST, (1,), whole, whole)
    k, v, gate_b, q = _pre_b(x1, p['g_kv'], p['w_kv'], p['g_pre_b'], p['w_in_b'], n_seq, HIGHEST)
    cache_kt = cache_k.transpose(0, 2, 3, 1)
    cache_vt = cache_v.transpose(0, 2, 3, 1)
    km_t = _block_means(cache_kt, page_table, n_past)
    idx = _sample_topk(q.reshape(n_seq, KV_HEADS, Q_PER_KV, HEAD_DIM), km_t)[:, :, :MOBA_TOPK].reshape(-1)
    att = _moba_sample(q.reshape(n_seq, N_HEADS, 1, HEAD_DIM), k.reshape(n_seq, KV_HEADS, 1, HEAD_DIM),
                       v.reshape(n_seq, KV_HEADS, 1, HEAD_DIM), cache_kt, cache_vt, page_table, idx, n_past)
    y_out = _out_b(att.reshape(n_seq, d), gate_b, x1, p['w_out_b'], p['g_post_b'], n_seq, HIGHEST)
    return (y_out.reshape(n_seq, 1, d), hr.transpose(1, 0, 2)[None], hi.transpose(1, 0, 2)[None],
            k.reshape(n_seq, 1, KV_HEADS, HEAD_DIM), v.reshape(n_seq, 1, KV_HEADS, HEAD_DIM))


def kernel(x_prompt, x_sample, state_ssm_re, state_ssm_im, cache_k, cache_v, page_table, g_pre_a, w_in_a, lambda_re, lambda_im, log_dt, b_re, b_im, c_re, c_im, d_skip, w_glu, b_glu, w_out_a, g_post_a, g_kv, w_kv, g_pre_b, w_in_b, w_out_b, g_post_b):
    p = {'g_pre_a': g_pre_a[0], 'w_in_a': w_in_a[0], 'lambda_re': lambda_re[0], 'lambda_im': lambda_im[0],
         'log_dt': log_dt[0], 'b_re': b_re[0], 'b_im': b_im[0], 'c_re': c_re[0], 'c_im': c_im[0],
         'd_skip': d_skip[0], 'w_glu': w_glu[0], 'b_glu': b_glu[0], 'w_out_a': w_out_a[0],
         'g_post_a': g_post_a[0], 'g_kv': g_kv, 'w_kv': w_kv, 'g_pre_b': g_pre_b[0],
         'w_in_b': w_in_b[0], 'w_out_b': w_out_b[0], 'g_post_b': g_post_b[0]}
    y_p, re_p, im_p, k_p, v_p = _prompt_trunk(x_prompt, p)
    y_s, re_s, im_s, k_s, v_s = _sample_trunk(x_sample, state_ssm_re[0], state_ssm_im[0],
                                              cache_k, cache_v, page_table, p)
    return (y_p, y_s, re_p, im_p, k_p, v_p, re_s, im_s, k_s, v_s)
```

```python
import functools
import math

import jax
import jax.numpy as jnp
from jax import lax
from jax.experimental import pallas as pl
from jax.experimental.pallas import tpu as pltpu

F32 = jnp.float32
BF16 = jnp.bfloat16
HIGHEST = lax.Precision.HIGHEST

D_MODEL = 1024
SSM_GROUP = 16
SSM_GROUPS = D_MODEL // SSM_GROUP
SSM_STATE = 64
HEAD_DIM = 64
N_HEADS = D_MODEL // HEAD_DIM
KV_HEADS = 4
Q_PER_KV = N_HEADS // KV_HEADS
KV_WIDTH = KV_HEADS * HEAD_DIM
MOBA_BLOCK = 256
MOBA_TOPK = 3
PAGE_SIZE = 128
PAGES_PER_BLOCK = MOBA_BLOCK // PAGE_SIZE
EPS = 1e-6
NEG_INF = -1e30
REMOVED = -3e38
SCALE = HEAD_DIM ** -0.5
S5_CHUNK = 16
LANES = 128
VMEM_LIMIT = 56 * 1024 * 1024


def _cparams(n_grid):
    return pltpu.CompilerParams(dimension_semantics=("arbitrary",) * n_grid,
                                vmem_limit_bytes=VMEM_LIMIT)


def _dot(a, b, prec=None):
    return jnp.dot(a, b, preferred_element_type=F32, precision=prec)


def _dot_nt(a, b, prec=None):
    return lax.dot_general(a, b, (((1,), (1,)), ((), ())),
                           preferred_element_type=F32, precision=prec)


def _rms_scale(x):
    return x * lax.rsqrt(jnp.mean(x * x, axis=-1, keepdims=True) + EPS)


def _sigmoid(x):
    return 1.0 / (1.0 + jnp.exp(-x))


def _full(shape):
    zeros = (0,) * len(shape)
    return pl.BlockSpec(shape, lambda *_: zeros)


def _tile(ref):
    x = ref[...]
    return x.reshape(x.shape[-2:])


def _norm_matmul_body(x_ref, g_ref, w_ref, *out_refs, prec):
    h = (_rms_scale(_tile(x_ref)) * g_ref[...]).astype(w_ref.dtype)
    off = 0
    for o_ref in out_refs:
        n = o_ref.shape[-1]
        o_ref[...] = _dot(h, w_ref[:, off:off + n], prec).astype(o_ref.dtype).reshape(o_ref.shape)
        off += n


def _norm_matmul(x, g, w, prec, grid, x_spec, out_specs, out_shape):
    d, n = w.shape
    return pl.pallas_call(
        functools.partial(_norm_matmul_body, prec=prec),
        grid=grid,
        in_specs=[x_spec, _full((1, d)), _full((d, n))],
        out_specs=out_specs,
        out_shape=out_shape,
        compiler_params=_cparams(len(grid)),
        name="norm_matmul",
    )(x, g.reshape(1, d), w)


S5_SUB = 64
S5_SUBS = LANES // S5_SUB
S5_SUB_GROUPS = S5_SUB // SSM_GROUP
S5_SUB_STATE = S5_SUB_GROUPS * SSM_STATE
S5_TILE = 2 * LANES
S5_TILE_TOKENS = S5_TILE // S5_SUB
S5_LAGS = S5_CHUNK // S5_TILE_TOKENS
S5_MC_ROWS = LANES


def _iota2(shape, dim):
    return lax.broadcasted_iota(jnp.int32, shape, dim)


def _expand_block_diagonal(sb, mc_ref, sc_ref, rc_ref, m_s, s_s, r_s):
    p, n, lg = SSM_GROUP, SSM_STATE, S5_SUB_GROUPS
    ss = S5_SUB_STATE
    grp = lambda idx, per: (idx // per) % lg
    _, rows, cols = s_s.shape
    i, j = _iota2((2 * n, cols), 0), _iota2((2 * n, cols), 1)
    spread = (i == (j // ss) * n + j % n).astype(F32).astype(BF16)
    blk = 256
    for r0 in range(0, rows, blk):
        full = _dot(sc_ref[0, sb, r0:r0 + blk, :], spread)
        keep = grp(_iota2((blk, cols), 0) + r0, p) == grp(_iota2((blk, cols), 1), n)
        s_s[sb, r0:r0 + blk, :] = jnp.where(keep, full, 0.0).astype(BF16)
    _, rows, cols = r_s.shape
    i, j = _iota2((rows, 2 * n), 0), _iota2((rows, 2 * n), 1)
    spread = (j == (i // ss) * n + i % n).astype(F32).astype(BF16)
    for c0 in range(0, cols, blk):
        full = _dot(spread, rc_ref[0, sb, :, c0:c0 + blk])
        keep = grp(_iota2((rows, blk), 0), n) == grp(_iota2((rows, blk), 1) + c0, p)
        r_s[sb, :, c0:c0 + blk] = jnp.where(keep, full, 0.0).astype(BF16)
    _, n_lag, rows, cols = m_s.shape
    i, j = _iota2((rows, S5_MC_ROWS), 0), _iota2((rows, S5_MC_ROWS), 1)
    spread = (j == (i // S5_SUB) * p + i % p).astype(F32).astype(BF16)
    keep = grp(_iota2((rows, cols), 0), p) == grp(_iota2((rows, cols), 1), p)
    for dl in range(n_lag):
        m_s[sb, dl] = jnp.where(keep, _dot(spread, mc_ref[0, sb, dl]), 0.0).astype(BF16)


def _s5_prompt_body(u_ref, mc_ref, sc_ref, rc_ref, a_ref, d_ref, y_ref, hfin_ref,
                    m_s, s_s, r_s, e_s, hin_s, h_s):
    bsz, tokens, _ = u_ref.shape
    t_chunk = S5_CHUNK
    rc = tokens // t_chunk
    ss = S5_SUB_STATE
    tk = S5_TILE_TOKENS

    @pl.when(pl.program_id(1) == 0)
    def _():
        h_s[...] = jnp.zeros(h_s.shape, F32)
        for sb in range(S5_SUBS):
            _expand_block_diagonal(sb, mc_ref, sc_ref, rc_ref, m_s, s_s, r_s)

    token = lambda ref, b, s: ref.at[b, pl.ds(s, rc, stride=t_chunk), :]
    xs = [jnp.concatenate([token(u_ref, b, s)[...] for b in range(bsz)], axis=0) for s in range(t_chunk)]
    xb = [x.astype(BF16) for x in xs]
    half = ss // LANES
    cols = lambda j: slice(j * LANES, (j + 1) * LANES)
    y_parts = [[None] * S5_SUBS for _ in range(t_chunk)]
    for sb in range(S5_SUBS):
        lanes = slice(sb * S5_SUB, (sb + 1) * S5_SUB)
        x_tiles = [jnp.concatenate([xb[tk * j + k][:, lanes] for k in range(tk)], axis=1) for j in range(S5_LAGS)]
        e = _dot(jnp.concatenate(x_tiles, axis=1), s_s[sb])
        for j in range(2 * half):
            e_s[sb, j] = e[:, cols(j)]
        ar = [jnp.broadcast_to(a_ref[0, sb, 0:1, cols(j)], (bsz, LANES)) for j in range(half)]
        ai = [jnp.broadcast_to(a_ref[0, sb, 1:2, cols(j)], (bsz, LANES)) for j in range(half)]

        def step(c, carry, sb=sb, ar=ar, ai=ai):
            hr, hi = carry
            at_c = pl.ds(c, bsz, stride=rc)
            new_r, new_i = [], []
            for j in range(half):
                hin_s[sb, j, at_c, :] = hr[j]
                hin_s[sb, half + j, at_c, :] = hi[j]
                new_r.append(ar[j] * hr[j] - ai[j] * hi[j] + e_s[sb, j, at_c, :])
                new_i.append(ar[j] * hi[j] + ai[j] * hr[j] + e_s[sb, half + j, at_c, :])
            return tuple(new_r), tuple(new_i)

        carry0 = (tuple(h_s[sb, 0, j] for j in range(half)), tuple(h_s[sb, 1, j] for j in range(half)))
        hr, hi = lax.fori_loop(0, rc, step, carry0)
        for j in range(half):
            h_s[sb, 0, j] = hr[j]
            h_s[sb, 1, j] = hi[j]
            hfin_ref[0, :, pl.ds(sb * 2 * ss + j * LANES, LANES)] = hr[j]
            hfin_ref[0, :, pl.ds(sb * 2 * ss + ss + j * LANES, LANES)] = hi[j]
        hin = jnp.concatenate([hin_s[sb, j] for j in range(2 * half)], axis=1).astype(BF16)
        for i in range(S5_LAGS):
            acc = _dot(hin, r_s[sb, :, i * S5_TILE:(i + 1) * S5_TILE])
            for j in range(i + 1):
                acc += _dot(x_tiles[j], m_s[sb, i - j])
            for k in range(tk):
                y_parts[tk * i + k][sb] = acc[:, k * S5_SUB:(k + 1) * S5_SUB]
    d_row = d_ref[0]
    for t in range(t_chunk):
        y_t = jnp.concatenate(y_parts[t], axis=1) + d_row * xs[t]
        for b in range(bsz):
            token(y_ref, b, t)[...] = y_t[b * rc:(b + 1) * rc]


def _s5_prompt(u, ops, rc):
    bsz, seqlen, d = u.shape
    n_lb = d // LANES
    mc, sc, rc_op, a16, dl = ops
    ss = S5_SUB_STATE
    t_chunk = S5_CHUNK
    tokens = rc * t_chunk
    rows = bsz * rc
    act = pl.BlockSpec((bsz, tokens, LANES), lambda lb, ct: (0, ct, lb))
    per_lb = lambda shape: pl.BlockSpec((1,) + shape, lambda lb, ct: (lb,) + (0,) * len(shape))
    state_tiles = 2 * ss // LANES
    return pl.pallas_call(
        _s5_prompt_body,
        grid=(n_lb, seqlen // tokens),
        in_specs=[act, per_lb(mc.shape[1:]), per_lb(sc.shape[1:]), per_lb(rc_op.shape[1:]),
                  per_lb(a16.shape[1:]), per_lb((1, LANES))],
        out_specs=[act, per_lb((bsz, S5_SUBS * 2 * ss))],
        out_shape=[jax.ShapeDtypeStruct(u.shape, F32),
                   jax.ShapeDtypeStruct((n_lb, bsz, S5_SUBS * 2 * ss), F32)],
        scratch_shapes=[pltpu.VMEM((S5_SUBS, S5_LAGS, S5_TILE, S5_TILE), BF16),
                        pltpu.VMEM((S5_SUBS, t_chunk * S5_SUB, 2 * ss), BF16),
                        pltpu.VMEM((S5_SUBS, 2 * ss, t_chunk * S5_SUB), BF16),
                        pltpu.VMEM((S5_SUBS, state_tiles, rows, LANES), F32),
                        pltpu.VMEM((S5_SUBS, state_tiles, rows, LANES), F32),
                        pltpu.VMEM((S5_SUBS, 2, ss // LANES, bsz, LANES), F32)],
        compiler_params=_cparams(2),
        name="s5_prompt",
    )(u, mc, sc, rc_op, a16, dl)


def _s5_discretize(lam_re, lam_im, log_dt, b_re, b_im):
    dt = jnp.exp(log_dt)[:, None]
    mag = jnp.exp(lam_re * dt)
    ar = mag * jnp.cos(lam_im * dt)
    ai = mag * jnp.sin(lam_im * dt)
    den = lam_re * lam_re + lam_im * lam_im
    nr = ar - 1.0
    coef_re = (nr * lam_re + ai * lam_im) / den
    coef_im = (ai * lam_re - nr * lam_im) / den
    bb_re = coef_re[..., None] * b_re - coef_im[..., None] * b_im
    bb_im = coef_re[..., None] * b_im + coef_im[..., None] * b_re
    return ar, ai, bb_re, bb_im


def _s5_prompt_operators(ar, ai, bb_re, bb_im, c_re, c_im, d_skip):
    t = S5_CHUNK
    g, n = ar.shape
    p = SSM_GROUP
    lg, tk, n_lag = S5_SUB_GROUPS, S5_TILE_TOKENS, S5_LAGS
    n_lb = g * p // LANES
    pr, pi = [jnp.ones_like(ar)], [jnp.zeros_like(ar)]
    for _ in range(t):
        pr, pi = pr + [pr[-1] * ar - pi[-1] * ai], pi + [pr[-1] * ai + pi[-1] * ar]
    pr = jnp.stack(pr)
    pi = jnp.stack(pi)
    cp_re = c_re[None] * pr[:, :, None, :] - c_im[None] * pi[:, :, None, :]
    cp_im = c_re[None] * pi[:, :, None, :] + c_im[None] * pr[:, :, None, :]
    k = (jnp.einsum('tgpn,gnq->gtpq', cp_re[:t], bb_re, precision=HIGHEST)
         - jnp.einsum('tgpn,gnq->gtpq', cp_im[:t], bb_im, precision=HIGHEST))
    k_lag = lambda lag: k[:, lag] if lag >= 0 else jnp.zeros_like(k[:, 0])
    kt = jnp.stack([jnp.stack([jnp.stack([k_lag(tk * dl + tt - ss) for tt in range(tk)], axis=1)
                               for ss in range(tk)], axis=1) for dl in range(n_lag)], axis=1)
    kt = kt.reshape(n_lb, S5_SUBS, lg, n_lag, tk, tk, p, p)
    m = kt.transpose(0, 1, 3, 4, 7, 5, 2, 6).reshape(n_lb, S5_SUBS, n_lag, tk * p, tk * lg * p)
    m = jnp.pad(m, ((0, 0), (0, 0), (0, 0), (0, S5_MC_ROWS - tk * p), (0, 0)))
    pr_rev = jnp.stack([pr[t - 1 - s] for s in range(t)])
    pi_rev = jnp.stack([pi[t - 1 - s] for s in range(t)])
    s_re = pr_rev[:, :, :, None] * bb_re[None] - pi_rev[:, :, :, None] * bb_im[None]
    s_im = pr_rev[:, :, :, None] * bb_im[None] + pi_rev[:, :, :, None] * bb_re[None]
    s_ri = jnp.stack([s_re, s_im]).reshape(2, t, n_lb, S5_SUBS, lg, n, p)
    s = s_ri.transpose(2, 3, 1, 4, 6, 0, 5).reshape(n_lb, S5_SUBS, t * lg * p, 2 * n)
    r_ri = jnp.stack([cp_re[1:], -cp_im[1:]]).reshape(2, t, n_lb, S5_SUBS, lg, p, n)
    r = r_ri.transpose(2, 3, 0, 6, 1, 4, 5).reshape(n_lb, S5_SUBS, 2 * n, t * lg * p)
    a16 = jnp.stack([pr[t].reshape(n_lb, S5_SUBS, lg * n), pi[t].reshape(n_lb, S5_SUBS, lg * n)], axis=2)
    return (m.astype(BF16), s.astype(BF16), r.astype(BF16), a16, d_skip.reshape(n_lb, 1, LANES))


def _s5_step_body(u_ref, h0r_ref, h0i_ref, ar_ref, ai_ref, bbr_ref, bbi_ref, cr_ref, ci_ref,
                  d_ref, y_ref, hr_ref, hi_ref):
    u = u_ref[0]
    ar, ai = ar_ref[0], ai_ref[0]
    h0r, h0i = h0r_ref[0], h0i_ref[0]
    hr = _dot_nt(u, bbr_ref[0], HIGHEST) + ar * h0r - ai * h0i
    hi = _dot_nt(u, bbi_ref[0], HIGHEST) + ar * h0i + ai * h0r
    hr_ref[0] = hr
    hi_ref[0] = hi
    y_ref[0] = (_dot_nt(hr, cr_ref[0], HIGHEST) - _dot_nt(hi, ci_ref[0], HIGHEST)
                + d_ref[0] * u)


def _s5_step(u_g, h0r, h0i, ar, ai, bb_re, bb_im, c_re, c_im, d_skip):
    g, b, p = u_g.shape
    n = ar.shape[-1]
    per_group = lambda shape: pl.BlockSpec((1,) + shape, lambda i: (i, 0, 0))
    return pl.pallas_call(
        _s5_step_body,
        grid=(g,),
        in_specs=[per_group((b, p)), per_group((b, n)), per_group((b, n)),
                  per_group((1, n)), per_group((1, n)),
                  per_group((n, p)), per_group((n, p)),
                  per_group((p, n)), per_group((p, n)), per_group((1, p))],
        out_specs=[per_group((b, p)), per_group((b, n)), per_group((b, n))],
        out_shape=[jax.ShapeDtypeStruct((g, b, p), F32),
                   jax.ShapeDtypeStruct((g, b, n), F32),
                   jax.ShapeDtypeStruct((g, b, n), F32)],
        compiler_params=_cparams(1),
        name="s5_step",
    )(u_g, h0r, h0i, ar.reshape(g, 1, n), ai.reshape(g, 1, n), bb_re, bb_im, c_re, c_im,
      d_skip.reshape(g, 1, p))


def _post_a_body(y_ref, ga_ref, x_ref, wglu_ref, bglu_ref, wout_ref, gpost_ref, x1_ref, *, prec):
    y = _tile(y_ref)
    wdt = wglu_ref.dtype
    gy = 0.5 * y * (1.0 + lax.erf(y * math.sqrt(0.5)))
    z = _dot(gy.astype(wdt), wglu_ref[...], prec) + bglu_ref[...]
    gate = _tile(ga_ref)
    t = (gy * _sigmoid(z)) * (gate * _sigmoid(gate))
    o = _dot(t.astype(wdt), wout_ref[...], prec)
    x1_ref[...] = (_tile(x_ref) + _rms_scale(o) * gpost_ref[...]).reshape(x1_ref.shape)


def _post_a(y, gate, x, w_glu, b_glu, w_out, g_post, prec, grid, y_spec, x_spec):
    d = w_glu.shape[0]
    return pl.pallas_call(
        functools.partial(_post_a_body, prec=prec),
        grid=grid,
        in_specs=[y_spec, x_spec, x_spec, _full((d, d)), _full((1, d)), _full((d, d)), _full((1, d))],
        out_specs=x_spec,
        out_shape=jax.ShapeDtypeStruct(x.shape, F32),
        compiler_params=_cparams(len(grid)),
        name="post_a",
    )(y, gate, x, w_glu, b_glu.reshape(1, d), w_out, g_post.reshape(1, d))


ALIBI_ROWS = 16
FEAT_OFF = HEAD_DIM
V_ROWS = HEAD_DIM + 16


def _aug_width(nb):
    nbp = -(-nb // 16) * 16
    return -(-(FEAT_OFF + ALIBI_ROWS + nbp) // LANES) * LANES, nbp


def _pre_b_body(x1_ref, gkv_ref, wkv_ref, gb_ref, winb_ref, k_ref, v_ref, gate_ref, *rest, prec, nb):
    xn = _rms_scale(x1_ref[...])
    wdt = wkv_ref.dtype
    hk = (xn * gkv_ref[...]).astype(wdt)
    k = _dot(hk, wkv_ref[:, :KV_WIDTH], prec)
    v = _dot(hk, wkv_ref[:, KV_WIDTH:], prec)
    hq = (xn * gb_ref[...]).astype(wdt)
    q = _dot(hq, winb_ref[:, :D_MODEL], prec) * SCALE
    gate_ref[...] = _dot(hq, winb_ref[:, D_MODEL:], prec)
    if nb is None:
        q_ref, = rest
        q_ref[...] = q
        k_ref[...] = k
        v_ref[...] = v
        return
    qt_ref, ka_ref, vta_ref, km_ref, st_ref = rest
    tm = q.shape[0]
    qt_ref[0, 0] = q.T.astype(BF16)
    km_ref[0] = jnp.mean(k, axis=0, keepdims=True)
    pick = lambda width: (_iota2((width, LANES), 0) // HEAD_DIM == _iota2((width, LANES), 1)).astype(F32).astype(BF16)
    sq_norms = lambda x: jnp.max(_dot((x * x).astype(BF16), pick(x.shape[1])), axis=0, keepdims=True)
    st_ref[0, 0:1, :] = sq_norms(q)
    st_ref[0, 1:2, :] = sq_norms(k)
    n = pl.program_id(0) % nb
    width = ka_ref.shape[-1] - FEAT_OFF
    lane = lax.broadcasted_iota(jnp.int32, (tm, width), 1)
    rowi = lax.broadcasted_iota(jnp.int32, (tm, width), 0)
    feat = jnp.where(lane < 3, n, jnp.where(lane < 6, rowi, jnp.where(lane < 9, 1, 0)))
    feat = jnp.where(lane - ALIBI_ROWS == n, 1, feat).astype(F32).astype(BF16)
    kb = k.astype(BF16)
    vt = v.T
    k_ref[0] = k.T
    v_ref[0] = vt
    ones_row = (lax.broadcasted_iota(jnp.int32, (V_ROWS - HEAD_DIM, tm), 0) == 0).astype(F32).astype(BF16)
    for h in range(KV_HEADS):
        ka_ref[0, h, :, :FEAT_OFF] = kb[:, h * HEAD_DIM:(h + 1) * HEAD_DIM]
        ka_ref[0, h, :, FEAT_OFF:] = feat
        vta_ref[0, 0, h * V_ROWS:h * V_ROWS + HEAD_DIM, :] = vt[h * HEAD_DIM:(h + 1) * HEAD_DIM].astype(BF16)
        vta_ref[0, 0, h * V_ROWS + HEAD_DIM:(h + 1) * V_ROWS, :] = ones_row


def _pre_b(x1, g_kv, w_kv, g_pre_b, w_in_b, tm, prec, prompt_blocks=None):
    m, d = x1.shape
    row = lambda wd: pl.BlockSpec((tm, wd), lambda i: (i, 0))
    nb = None
    if prompt_blocks is None:
        kv_spec = row(KV_WIDTH)
        kv_shape = jax.ShapeDtypeStruct((m, KV_WIDTH), F32)
        out_specs = [kv_spec, kv_spec, row(d), row(d)]
        out_shape = [kv_shape, kv_shape, jax.ShapeDtypeStruct((m, d), F32), jax.ShapeDtypeStruct((m, d), F32)]
    else:
        bsz, nb = prompt_blocks
        assert tm == MOBA_BLOCK and m == bsz * nb * tm
        aug, _ = _aug_width(nb)
        kv_spec = pl.BlockSpec((1, KV_WIDTH, tm), lambda i: (i // nb, 0, i % nb))
        kv_shape = jax.ShapeDtypeStruct((bsz, KV_WIDTH, nb * tm), F32)
        out_specs = [kv_spec, kv_spec, row(d)]
        out_shape = [kv_shape, kv_shape, jax.ShapeDtypeStruct((m, d), F32)]
        out_specs += [pl.BlockSpec((1, 1, d, tm), lambda i: (i // nb, i % nb, 0, 0)),
                      pl.BlockSpec((1, KV_HEADS, tm, aug), lambda i: (i // nb, 0, i % nb, 0)),
                      pl.BlockSpec((1, 1, KV_HEADS * V_ROWS, tm), lambda i: (i // nb, i % nb, 0, 0)),
                      pl.BlockSpec((1, 1, KV_WIDTH), lambda i: (i, 0, 0)),
                      pl.BlockSpec((1, 2, LANES), lambda i: (i, 0, 0))]
        out_shape += [jax.ShapeDtypeStruct((bsz, nb, d, tm), BF16),
                      jax.ShapeDtypeStruct((bsz, KV_HEADS, nb * tm, aug), BF16),
                      jax.ShapeDtypeStruct((bsz, nb, KV_HEADS * V_ROWS, tm), BF16),
                      jax.ShapeDtypeStruct((bsz * nb, 1, KV_WIDTH), F32),
                      jax.ShapeDtypeStruct((bsz * nb, 2, LANES), F32)]
    return pl.pallas_call(
        functools.partial(_pre_b_body, prec=prec, nb=nb),
        grid=(m // tm,),
        in_specs=[row(d), _full((1, d)), _full((d, 2 * KV_WIDTH)), _full((1, d)), _full((d, 2 * d))],
        out_specs=out_specs,
        out_shape=out_shape,
        compiler_params=_cparams(1),
        name="pre_b",
    )(x1, g_kv.reshape(1, d), w_kv, g_pre_b.reshape(1, d), w_in_b)


def _top3_mask(gs, idx, valid, axis):
    n = gs.shape[axis]
    gs = jnp.where(valid, gs, NEG_INF)
    sel = jnp.zeros(gs.shape, F32)
    picks = []
    for _ in range(MOBA_TOPK):
        mx = jnp.max(gs, axis=axis, keepdims=True)
        first = jnp.min(jnp.where(gs == mx, idx, n), axis=axis, keepdims=True)
        pick = idx == first
        sel = jnp.where(pick & valid, 1.0, sel)
        gs = jnp.where(pick, REMOVED, gs)
        picks.append(first)
    return sel, picks


def _split3(x):
    hi = x.astype(BF16).astype(F32)
    mid = (x - hi).astype(BF16).astype(F32)
    lo = (x - hi - mid).astype(BF16).astype(F32)
    return [hi, mid, lo]


def _moba_prompt_body(first_ref, qt_ref, g_ref, ka_ref, vta_ref, km_ref, o_ref, qa_s, qd_s):
    h = pl.program_id(1)
    c = pl.program_id(2)
    blk = MOBA_BLOCK
    nb = km_ref.shape[2]
    aug, nbp = _aug_width(nb)
    km = km_ref[0, 0]
    row = lax.broadcasted_iota(jnp.int32, (nb, blk), 0)
    frow = lax.broadcasted_iota(jnp.int32, (ALIBI_ROWS, blk), 0)
    pos_q = (c * blk + lax.broadcasted_iota(jnp.int32, (1, blk), 1)).astype(F32)
    for g in range(Q_PER_KV):
        lanes = slice(g * blk, (g + 1) * blk)
        qg = qt_ref[0, 0, g * HEAD_DIM:(g + 1) * HEAD_DIM, :]
        sel, _ = _top3_mask(_dot(km, qg.astype(F32), HIGHEST), row, row < c, 0)
        bias = jnp.where(sel > 0.0, 0.0, NEG_INF)
        if nbp > nb:
            bias = jnp.concatenate([bias, jnp.zeros((nbp - nb, blk), F32)], axis=0)
        head1 = (h * Q_PER_KV + g + 1).astype(F32)
        slope = jnp.exp2(-8.0 * jnp.full((1, blk), head1, F32) / N_HEADS)
        terms = _split3(slope * float(blk)) + _split3(slope) + _split3(-slope * pos_q)
        feat = jnp.zeros((ALIBI_ROWS, blk), F32)
        for r, term in enumerate(terms):
            feat = jnp.where(frow == r, term, feat)
        for ref, b_rows in ((qa_s, bias.astype(BF16)), (qd_s, jnp.zeros((nbp, blk), BF16))):
            ref[:FEAT_OFF, lanes] = qg
            ref[FEAT_OFF:FEAT_OFF + ALIBI_ROWS, lanes] = feat.astype(BF16)
            ref[FEAT_OFF + ALIBI_ROWS:FEAT_OFF + ALIBI_ROWS + nbp, lanes] = b_rows
            if FEAT_OFF + ALIBI_ROWS + nbp < aug:
                ref[FEAT_OFF + ALIBI_ROWS + nbp:, lanes] = jnp.zeros((aug - FEAT_OFF - ALIBI_ROWS - nbp, blk), BF16)

    def keys(n):
        return ka_ref[0, 0, pl.ds(pl.multiple_of(n * blk, blk), blk), :]

    s = _dot(keys(c), qd_s[...])
    key = lax.broadcasted_iota(jnp.int32, s.shape, 0)
    qry = lax.broadcasted_iota(jnp.int32, s.shape, 1) & (blk - 1)
    s = jnp.where(key <= qry, s, NEG_INF)
    m = jnp.max(s, axis=0, keepdims=True)
    acc = _dot(vta_ref[0, c], jnp.exp(s - m).astype(BF16))

    def block_pair(i, carry):
        m, acc = carry
        n0, n1 = 2 * i, 2 * i + 1
        s0 = _dot(keys(n0), qa_s[...])
        s1 = _dot(keys(n1), qa_s[...])
        m2 = jnp.maximum(m, jnp.maximum(jnp.max(s0, axis=0, keepdims=True), jnp.max(s1, axis=0, keepdims=True)))
        p0 = jnp.exp(s0 - m2).astype(BF16)
        p1 = jnp.exp(s1 - m2).astype(BF16)
        return m2, jnp.exp(m - m2) * acc + _dot(vta_ref[0, n0], p0) + _dot(vta_ref[0, n1], p1)

    first = first_ref[(pl.program_id(0) * KV_HEADS + h) * nb + c]
    m, acc = lax.fori_loop(first, (c + 1) // 2, block_pair, (m, acc))
    out_t = acc[:HEAD_DIM] / acc[HEAD_DIM:HEAD_DIM + 1]
    att = jnp.concatenate([out_t[:, g * blk:(g + 1) * blk] for g in range(Q_PER_KV)], axis=0).T
    gate = g_ref[0]
    o_ref[0] = (att * (gate * _sigmoid(gate))).astype(o_ref.dtype)


_SKIP_LOGIT = -120.0
_NORM_SLACK = 1.02


def _first_needed_pair(stats, bsz, nb):
    q_max = jnp.sqrt(stats[:, 0, :N_HEADS]).reshape(bsz, nb, KV_HEADS, Q_PER_KV) * _NORM_SLACK
    k_max = jnp.sqrt(stats[:, 1, :KV_HEADS]).reshape(bsz, nb, KV_HEADS) * _NORM_SLACK
    slopes = jnp.exp2(-8.0 * jnp.arange(1, N_HEADS + 1, dtype=F32) / N_HEADS).reshape(KV_HEADS, Q_PER_KV)
    c = jnp.arange(nb)[:, None]
    n = jnp.arange(nb)[None, :]
    min_dist = ((c - n) * MOBA_BLOCK - (MOBA_BLOCK - 1)).astype(F32)
    k_pair = k_max[:, None, :, :] + k_max[:, :, None, :]
    bound = (q_max[:, :, None, :, :] * k_pair[..., None]
             - slopes[None, None, None] * min_dist[None, :, :, None, None])
    dead = jnp.all(bound < _SKIP_LOGIT, axis=-1) & (n < c)[None, :, :, None]
    lead = jnp.min(jnp.where(dead, nb, n[None, :, :, None]), axis=2)
    return (lead // 2).transpose(0, 2, 1).reshape(-1).astype(jnp.int32)


def _moba_prompt(qt, gate, ka, vta, km, first_pair):
    bsz, seqlen, d = gate.shape
    nb = seqlen // MOBA_BLOCK
    width = Q_PER_KV * HEAD_DIM
    aug, _ = _aug_width(nb)
    tile = pl.BlockSpec((1, MOBA_BLOCK, width), lambda b, h, c, fp: (b, c, h))
    return pl.pallas_call(
        _moba_prompt_body,
        grid_spec=pltpu.PrefetchScalarGridSpec(
            num_scalar_prefetch=1,
            grid=(bsz, KV_HEADS, nb),
            in_specs=[pl.BlockSpec((1, 1, width, MOBA_BLOCK), lambda b, h, c, fp: (b, c, h, 0)),
                      tile,
                      pl.BlockSpec((1, 1, seqlen, aug), lambda b, h, c, fp: (b, h, 0, 0)),
                      pl.BlockSpec((1, nb, V_ROWS, MOBA_BLOCK), lambda b, h, c, fp: (b, 0, h, 0)),
                      pl.BlockSpec((1, 1, nb, HEAD_DIM), lambda b, h, c, fp: (b, h, 0, 0))],
            out_specs=tile,
            scratch_shapes=[pltpu.VMEM((aug, Q_PER_KV * MOBA_BLOCK), BF16) for _ in range(2)]),
        out_shape=jax.ShapeDtypeStruct((bsz, seqlen, d), BF16),
        compiler_params=_cparams(3),
        name="moba_prompt",
    )(first_pair, qt, gate, ka, vta, km)


_KM_BLOCKS = 16


def _block_mean_body(pt_ref, *refs):
    del pt_ref
    pages, o_ref = refs[:-1], refs[-1]
    s = pl.program_id(1)

    @pl.when(s == 0)
    def _():
        o_ref[...] = jnp.zeros(o_ref.shape, F32)

    lane = lax.broadcasted_iota(jnp.int32, o_ref.shape[1:], 2)
    acc = o_ref[0]
    for j in range(_KM_BLOCKS):
        tot = sum(pages[PAGES_PER_BLOCK * j + r][0] for r in range(PAGES_PER_BLOCK))
        mean = jnp.sum(tot, axis=-1, keepdims=True) * (1.0 / MOBA_BLOCK)
        acc = jnp.where(lane == s * _KM_BLOCKS + j, mean, acc)
    o_ref[0] = acc


def _block_means(cache_t, page_table, n_blocks):
    n_seq = page_table.shape[0]
    assert n_blocks % _KM_BLOCKS == 0
    per_step = _KM_BLOCKS * PAGES_PER_BLOCK
    page_spec = lambda j: pl.BlockSpec(
        (1, KV_HEADS, HEAD_DIM, PAGE_SIZE), lambda b, s, pt, j=j: (pt[b, s * per_step + j], 0, 0, 0))
    return pl.pallas_call(
        _block_mean_body,
        grid_spec=pltpu.PrefetchScalarGridSpec(
            num_scalar_prefetch=1,
            grid=(n_seq, n_blocks // _KM_BLOCKS),
            in_specs=[page_spec(j) for j in range(per_step)],
            out_specs=pl.BlockSpec((1, KV_HEADS, HEAD_DIM, n_blocks), lambda b, s, pt: (b, 0, 0, 0))),
        out_shape=jax.ShapeDtypeStruct((n_seq, KV_HEADS, HEAD_DIM, n_blocks), F32),
        compiler_params=_cparams(2),
        name="block_means",
    )(page_table, *([cache_t] * per_step))


def _sample_topk_body(q_ref, km_ref, idx_ref, *, n_past):
    gs = jnp.concatenate([_dot(q_ref[0, k], km_ref[0, k], HIGHEST) for k in range(KV_HEADS)], axis=0)
    lane = lax.broadcasted_iota(jnp.int32, gs.shape, 1)
    _, picks = _top3_mask(gs, lane, lane < n_past, 1)
    out_lane = lax.broadcasted_iota(jnp.int32, idx_ref.shape[1:], 1)
    out = jnp.zeros(idx_ref.shape[1:], jnp.int32)
    for r, first in enumerate(picks):
        out = jnp.where(out_lane == r, first, out)
    idx_ref[0] = out


def _sample_topk(q, km_t):
    n_seq, _, _, n_past = km_t.shape
    return pl.pallas_call(
        functools.partial(_sample_topk_body, n_past=n_past),
        grid=(n_seq,),
        in_specs=[pl.BlockSpec((1,) + q.shape[1:], lambda b: (b, 0, 0, 0)),
                  pl.BlockSpec((1,) + km_t.shape[1:], lambda b: (b, 0, 0, 0))],
        out_specs=pl.BlockSpec((1, N_HEADS, LANES), lambda b: (b, 0, 0)),
        out_shape=jax.ShapeDtypeStruct((n_seq, N_HEADS, LANES), jnp.int32),
        compiler_params=_cparams(1),
        name="sample_topk",
    )(q, km_t)


_Q_ROWS = 8


def _moba_sample_body(pt_ref, idx_ref, q_ref, kn_ref, vn_ref, *refs, n_past):
    del pt_ref
    n_sel = MOBA_TOPK * PAGES_PER_BLOCK
    k_pages, v_pages, o_ref = refs[:n_sel], refs[n_sel:2 * n_sel], refs[-1]
    b = pl.program_id(0)
    h = pl.program_id(1)
    keys = MOBA_TOPK * MOBA_BLOCK
    q = jnp.broadcast_to(q_ref[0, 0], (_Q_ROWS, HEAD_DIM))
    kt_sel = jnp.concatenate([r[0, 0] for r in k_pages], axis=1)
    vt_sel = jnp.concatenate([r[0, 0] for r in v_pages], axis=1)
    col = lax.broadcasted_iota(jnp.int32, (_Q_ROWS, keys), 1)
    slot = col // MOBA_BLOCK
    base = (b * N_HEADS + h) * MOBA_TOPK
    idx = jnp.zeros((_Q_ROWS, keys), jnp.int32)
    for r in range(MOBA_TOPK):
        idx = jnp.where(slot == r, idx_ref[base + r], idx)
    pos = n_past * MOBA_BLOCK
    dist = (pos - (idx * MOBA_BLOCK + col % MOBA_BLOCK)).astype(F32)
    slope = jnp.exp2(-8.0 * jnp.full((_Q_ROWS, keys), (h + 1).astype(F32), F32) / N_HEADS)
    s = _dot(q.astype(BF16), kt_sel.astype(BF16)) - slope * dist
    s = jnp.where(idx < n_past, s, NEG_INF)
    s_own = jnp.sum(q * kn_ref[0, 0], axis=-1, keepdims=True)
    m = jnp.maximum(jnp.max(s, axis=-1, keepdims=True), s_own)
    p = jnp.exp(s - m)
    p_own = jnp.exp(s_own - m)
    l = jnp.sum(p, axis=-1, keepdims=True) + p_own
    out = (_dot_nt(p.astype(BF16), vt_sel.astype(BF16)) + p_own * vn_ref[0, 0]) / l
    o_ref[0, 0] = out[0:1]


def _moba_sample(q, k_new, v_new, cache_kt, cache_vt, page_table, idx_flat, n_past):
    n_seq = page_table.shape[0]

    def page_spec(j):
        r, pg = divmod(j, PAGES_PER_BLOCK)

        def index(b, h, pt, idx):
            blk = idx[(b * N_HEADS + h) * MOBA_TOPK + r]
            return (pt[b, blk * PAGES_PER_BLOCK + pg], h // Q_PER_KV, 0, 0)

        return pl.BlockSpec((1, 1, HEAD_DIM, PAGE_SIZE), index)

    n_sel = MOBA_TOPK * PAGES_PER_BLOCK
    row = (1, 1, 1, HEAD_DIM)
    new_spec = pl.BlockSpec(row, lambda b, h, pt, idx: (b, h // Q_PER_KV, 0, 0))
    head_spec = pl.BlockSpec(row, lambda b, h, pt, idx: (b, h, 0, 0))
    return pl.pallas_call(
        functools.partial(_moba_sample_body, n_past=n_past),
        grid_spec=pltpu.PrefetchScalarGridSpec(
            num_scalar_prefetch=2,
            grid=(n_seq, N_HEADS),
            in_specs=[head_spec, new_spec, new_spec] + [page_spec(j) for j in range(n_sel)] * 2,
            out_specs=head_spec),
        out_shape=jax.ShapeDtypeStruct((n_seq, N_HEADS, 1, HEAD_DIM), F32),
        compiler_params=_cparams(2),
        name="moba_sample",
    )(page_table, idx_flat, q, k_new, v_new, *([cache_kt] * n_sel), *([cache_vt] * n_sel))


def _out_b_body(a_ref, *refs, prec, gated):
    if gated:
        x1_ref, w_ref, g_ref, y_ref = refs
        a = a_ref[...]
    else:
        gate_ref, x1_ref, w_ref, g_ref, y_ref = refs
        gate = gate_ref[...]
        a = (a_ref[...] * (gate * _sigmoid(gate))).astype(w_ref.dtype)
    o = _dot(a, w_ref[...], prec)
    y_ref[...] = x1_ref[...] + _rms_scale(o) * g_ref[...]


def _out_b(att, gate, x1, w, g_post, tm, prec):
    m, d = x1.shape
    tile = pl.BlockSpec((tm, d), lambda i: (i, 0))
    gated = gate is None
    acts = [att, x1] if gated else [att, gate, x1]
    return pl.pallas_call(
        functools.partial(_out_b_body, prec=prec, gated=gated),
        grid=(m // tm,),
        in_specs=[tile] * len(acts) + [_full((d, d)), _full((1, d))],
        out_specs=tile,
        out_shape=jax.ShapeDtypeStruct((m, d), F32),
        compiler_params=_cparams(1),
        name="out_b",
    )(*acts, w, g_post.reshape(1, d))


TM_IN_A = 1024
TM_POST_A = 512
TM_OUT_B = 1024
S5_CHUNKS_PER_STEP = 64


def _prompt_trunk(x, p):
    bsz, seqlen, d = x.shape
    m = bsz * seqlen
    nb = seqlen // MOBA_BLOCK
    nc = seqlen // S5_CHUNK
    x2 = x.reshape(m, d)
    tm = TM_IN_A
    tile = pl.BlockSpec((tm, d), lambda i: (i, 0))
    u, gate_a = _norm_matmul(x2, p['g_pre_a'], p['w_in_a'].astype(BF16), None, (m // tm,), tile, [tile, tile],
                             [jax.ShapeDtypeStruct((m, d), F32)] * 2)
    ar, ai, bb_re, bb_im = _s5_discretize(p['lambda_re'], p['lambda_im'], p['log_dt'], p['b_re'], p['b_im'])
    ops = _s5_prompt_operators(ar, ai, bb_re, bb_im, p['c_re'], p['c_im'], p['d_skip'])
    y, h_fin = _s5_prompt(u.reshape(bsz, seqlen, d), ops, min(nc, S5_CHUNKS_PER_STEP))
    h_fin = h_fin.reshape(d // LANES, bsz, S5_SUBS, 2, S5_SUB_GROUPS, SSM_STATE).transpose(3, 1, 0, 2, 4, 5)
    h_fin = h_fin.reshape(2, 1, bsz, SSM_GROUPS, SSM_STATE)
    tm = TM_POST_A
    tile = pl.BlockSpec((tm, d), lambda i: (i, 0))
    x1 = _post_a(y.reshape(m, d), gate_a, x2, p['w_glu'].astype(BF16), p['b_glu'], p['w_out_a'].astype(BF16),
                 p['g_post_a'], None, (m // tm,), tile, tile)
    k, v, gate_b, qt, ka, vta, km, stats = _pre_b(x1, p['g_kv'], p['w_kv'].astype(BF16), p['g_pre_b'],
                                                  p['w_in_b'].astype(BF16), MOBA_BLOCK, None, (bsz, nb))
    km = km.reshape(bsz, nb, KV_HEADS, HEAD_DIM).transpose(0, 2, 1, 3)
    att = _moba_prompt(qt, gate_b.reshape(bsz, seqlen, d), ka, vta, km, _first_needed_pair(stats, bsz, nb))
    y_out = _out_b(att.reshape(m, d), None, x1, p['w_out_b'].astype(BF16), p['g_post_b'], TM_OUT_B, None)
    to_out = lambda t: t.reshape(bsz, KV_HEADS, HEAD_DIM, seqlen).transpose(0, 3, 1, 2)
    return (y_out.reshape(bsz, seqlen, d), h_fin[0], h_fin[1], to_out(k), to_out(v))


def _sample_trunk(x, h0_re, h0_im, cache_k, cache_v, page_table, p):
    n_seq, seqlen, d = x.shape
    assert seqlen == 1
    g, pch = SSM_GROUPS, SSM_GROUP
    n_past = page_table.shape[1] // PAGES_PER_BLOCK
    x2 = x.reshape(n_seq, d)
    whole = pl.BlockSpec((n_seq, d), lambda i: (0, 0))
    u, gate_a = _norm_matmul(x2, p['g_pre_a'], p['w_in_a'], HIGHEST, (1,), whole, [whole, whole],
                             [jax.ShapeDtypeStruct((n_seq, d), F32)] * 2)
    ar, ai, bb_re, bb_im = _s5_discretize(p['lambda_re'], p['lambda_im'], p['log_dt'], p['b_re'], p['b_im'])
    y_g, hr, hi = _s5_step(u.reshape(n_seq, g, pch).transpose(1, 0, 2),
                           h0_re.transpose(1, 0, 2), h0_im.transpose(1, 0, 2),
                           ar, ai, bb_re, bb_im, p['c_re'], p['c_im'], p['d_skip'])
    y = y_g.transpose(1, 0, 2).reshape(n_seq, d)
    x1 = _post_a(y, gate_a, x2, p['w_glu'], p['b_glu'], p['w_out_a'], p['g_post_a'], HIGHEST, (1,), whole, whole)
    k, v, gate_b, q = _pre_b(x1, p['g_kv'], p['w_kv'], p['g_pre_b'], p['w_in_b'], n_seq, HIGHEST)
    cache_kt = cache_k.transpose(0, 2, 3, 1)
    cache_vt = cache_v.transpose(0, 2, 3, 1)
    km_t = _block_means(cache_kt, page_table, n_past)
    idx = _sample_topk(q.reshape(n_seq, KV_HEADS, Q_PER_KV, HEAD_DIM), km_t)[:, :, :MOBA_TOPK].reshape(-1)
    att = _moba_sample(q.reshape(n_seq, N_HEADS, 1, HEAD_DIM), k.reshape(n_seq, KV_HEADS, 1, HEAD_DIM),
                       v.reshape(n_seq, KV_HEADS, 1, HEAD_DIM), cache_kt, cache_vt, page_table, idx, n_past)
    y_out = _out_b(att.reshape(n_seq, d), gate_b, x1, p['w_out_b'], p['g_post_b'], n_seq, HIGHEST)
    return (y_out.reshape(n_seq, 1, d), hr.transpose(1, 0, 2)[None], hi.transpose(1, 0, 2)[None],
            k.reshape(n_seq, 1, KV_HEADS, HEAD_DIM), v.reshape(n_seq, 1, KV_HEADS, HEAD_DIM))


def kernel(x_prompt, x_sample, state_ssm_re, state_ssm_im, cache_k, cache_v, page_table, g_pre_a, w_in_a, lambda_re, lambda_im, log_dt, b_re, b_im, c_re, c_im, d_skip, w_glu, b_glu, w_out_a, g_post_a, g_kv, w_kv, g_pre_b, w_in_b, w_out_b, g_post_b):
    p = {'g_pre_a': g_pre_a[0], 'w_in_a': w_in_a[0], 'lambda_re': lambda_re[0], 'lambda_im': lambda_im[0],
         'log_dt': log_dt[0], 'b_re': b_re[0], 'b_im': b_im[0], 'c_re': c_re[0], 'c_im': c_im[0],
         'd_skip': d_skip[0], 'w_glu': w_glu[0], 'b_glu': b_glu[0], 'w_out_a': w_out_a[0],
         'g_post_a': g_post_a[0], 'g_kv': g_kv, 'w_kv': w_kv, 'g_pre_b': g_pre_b[0],
         'w_in_b': w_in_b[0], 'w_out_b': w_out_b[0], 'g_post_b': g_post_b[0]}
    y_p, re_p, im_p, k_p, v_p = _prompt_trunk(x_prompt, p)
    y_s, re_s, im_s, k_s, v_s = _sample_trunk(x_sample, state_ssm_re[0], state_ssm_im[0],
                                              cache_k, cache_v, page_table, p)
    return (y_p, y_s, re_p, im_p, k_p, v_p, re_s, im_s, k_s, v_s)
```

```python
import functools
import math

import jax
import jax.numpy as jnp
from jax import lax
from jax.experimental import pallas as pl
from jax.experimental.pallas import tpu as pltpu

F32 = jnp.float32
BF16 = jnp.bfloat16
HIGHEST = lax.Precision.HIGHEST

D_MODEL = 1024
SSM_GROUP = 16
SSM_GROUPS = D_MODEL // SSM_GROUP
SSM_STATE = 64
HEAD_DIM = 64
N_HEADS = D_MODEL // HEAD_DIM
KV_HEADS = 4
Q_PER_KV = N_HEADS // KV_HEADS
KV_WIDTH = KV_HEADS * HEAD_DIM
MOBA_BLOCK = 256
MOBA_TOPK = 3
PAGE_SIZE = 128
PAGES_PER_BLOCK = MOBA_BLOCK // PAGE_SIZE
EPS = 1e-6
NEG_INF = -1e30
REMOVED = -3e38
SCALE = HEAD_DIM ** -0.5
S5_CHUNK = 16
LANES = 128
VMEM_LIMIT = 56 * 1024 * 1024


def _cparams(n_grid):
    return pltpu.CompilerParams(dimension_semantics=("arbitrary",) * n_grid,
                                vmem_limit_bytes=VMEM_LIMIT)


def _dot(a, b, prec=None):
    return jnp.dot(a, b, preferred_element_type=F32, precision=prec)


def _dot_nt(a, b, prec=None):
    return lax.dot_general(a, b, (((1,), (1,)), ((), ())),
                           preferred_element_type=F32, precision=prec)


def _rms_scale(x):
    return x * lax.rsqrt(jnp.mean(x * x, axis=-1, keepdims=True) + EPS)


def _sigmoid(x):
    return 1.0 / (1.0 + jnp.exp(-x))


def _full(shape):
    zeros = (0,) * len(shape)
    return pl.BlockSpec(shape, lambda *_: zeros)


def _tile(ref):
    x = ref[...]
    return x.reshape(x.shape[-2:])


def _norm_matmul_body(x_ref, g_ref, w_ref, *out_refs, prec):
    h = (_rms_scale(_tile(x_ref)) * g_ref[...]).astype(w_ref.dtype)
    off = 0
    for o_ref in out_refs:
        n = o_ref.shape[-1]
        o_ref[...] = _dot(h, w_ref[:, off:off + n], prec).astype(o_ref.dtype).reshape(o_ref.shape)
        off += n


def _norm_matmul(x, g, w, prec, grid, x_spec, out_specs, out_shape):
    d, n = w.shape
    return pl.pallas_call(
        functools.partial(_norm_matmul_body, prec=prec),
        grid=grid,
        in_specs=[x_spec, _full((1, d)), _full((d, n))],
        out_specs=out_specs,
        out_shape=out_shape,
        compiler_params=_cparams(len(grid)),
        name="norm_matmul",
    )(x, g.reshape(1, d), w)


S5_SUB = 64
S5_SUBS = LANES // S5_SUB
S5_SUB_GROUPS = S5_SUB // SSM_GROUP
S5_SUB_STATE = S5_SUB_GROUPS * SSM_STATE
S5_TILE = 2 * LANES
S5_TILE_TOKENS = S5_TILE // S5_SUB
S5_LAGS = S5_CHUNK // S5_TILE_TOKENS
S5_MC_ROWS = LANES


def _iota2(shape, dim):
    return lax.broadcasted_iota(jnp.int32, shape, dim)


def _expand_block_diagonal(sb, mc_ref, sc_ref, rc_ref, m_s, s_s, r_s):
    p, n, lg = SSM_GROUP, SSM_STATE, S5_SUB_GROUPS
    ss = S5_SUB_STATE
    grp = lambda idx, per: (idx // per) % lg
    _, rows, cols = s_s.shape
    i, j = _iota2((2 * n, cols), 0), _iota2((2 * n, cols), 1)
    spread = (i == (j // ss) * n + j % n).astype(F32).astype(BF16)
    blk = 256
    for r0 in range(0, rows, blk):
        full = _dot(sc_ref[0, sb, r0:r0 + blk, :], spread)
        keep = grp(_iota2((blk, cols), 0) + r0, p) == grp(_iota2((blk, cols), 1), n)
        s_s[sb, r0:r0 + blk, :] = jnp.where(keep, full, 0.0).astype(BF16)
    _, rows, cols = r_s.shape
    i, j = _iota2((rows, 2 * n), 0), _iota2((rows, 2 * n), 1)
    spread = (j == (i // ss) * n + i % n).astype(F32).astype(BF16)
    for c0 in range(0, cols, blk):
        full = _dot(spread, rc_ref[0, sb, :, c0:c0 + blk])
        keep = grp(_iota2((rows, blk), 0), n) == grp(_iota2((rows, blk), 1) + c0, p)
        r_s[sb, :, c0:c0 + blk] = jnp.where(keep, full, 0.0).astype(BF16)
    _, n_lag, rows, cols = m_s.shape
    i, j = _iota2((rows, S5_MC_ROWS), 0), _iota2((rows, S5_MC_ROWS), 1)
    spread = (j == (i // S5_SUB) * p + i % p).astype(F32).astype(BF16)
    keep = grp(_iota2((rows, cols), 0), p) == grp(_iota2((rows, cols), 1), p)
    for dl in range(n_lag):
        m_s[sb, dl] = jnp.where(keep, _dot(spread, mc_ref[0, sb, dl]), 0.0).astype(BF16)


def _s5_prompt_body(u_ref, mc_ref, sc_ref, rc_ref, a_ref, d_ref, y_ref, hfin_ref,
                    m_s, s_s, r_s, e_s, hin_s, h_s):
    bsz, tokens, _ = u_ref.shape
    t_chunk = S5_CHUNK
    rc = tokens // t_chunk
    ss = S5_SUB_STATE
    tk = S5_TILE_TOKENS

    @pl.when(pl.program_id(1) == 0)
    def _():
        h_s[...] = jnp.zeros(h_s.shape, F32)
        for sb in range(S5_SUBS):
            _expand_block_diagonal(sb, mc_ref, sc_ref, rc_ref, m_s, s_s, r_s)

    token = lambda ref, b, s: ref.at[b, pl.ds(s, rc, stride=t_chunk), :]
    xs = [jnp.concatenate([token(u_ref, b, s)[...] for b in range(bsz)], axis=0) for s in range(t_chunk)]
    xb = [x.astype(BF16) for x in xs]
    half = ss // LANES
    cols = lambda j: slice(j * LANES, (j + 1) * LANES)
    y_parts = [[None] * S5_SUBS for _ in range(t_chunk)]
    for sb in range(S5_SUBS):
        lanes = slice(sb * S5_SUB, (sb + 1) * S5_SUB)
        x_tiles = [jnp.concatenate([xb[tk * j + k][:, lanes] for k in range(tk)], axis=1) for j in range(S5_LAGS)]
        e = _dot(jnp.concatenate(x_tiles, axis=1), s_s[sb])
        for j in range(2 * half):
            e_s[sb, j] = e[:, cols(j)]
        ar = [jnp.broadcast_to(a_ref[0, sb, 0:1, cols(j)], (bsz, LANES)) for j in range(half)]
        ai = [jnp.broadcast_to(a_ref[0, sb, 1:2, cols(j)], (bsz, LANES)) for j in range(half)]

        def step(c, carry, sb=sb, ar=ar, ai=ai):
            hr, hi = carry
            at_c = pl.ds(c, bsz, stride=rc)
            new_r, new_i = [], []
            for j in range(half):
                hin_s[sb, j, at_c, :] = hr[j]
                hin_s[sb, half + j, at_c, :] = hi[j]
                new_r.append(ar[j] * hr[j] - ai[j] * hi[j] + e_s[sb, j, at_c, :])
                new_i.append(ar[j] * hi[j] + ai[j] * hr[j] + e_s[sb, half + j, at_c, :])
            return tuple(new_r), tuple(new_i)

        carry0 = (tuple(h_s[sb, 0, j] for j in range(half)), tuple(h_s[sb, 1, j] for j in range(half)))
        hr, hi = lax.fori_loop(0, rc, step, carry0)
        for j in range(half):
            h_s[sb, 0, j] = hr[j]
            h_s[sb, 1, j] = hi[j]
            hfin_ref[0, :, pl.ds(sb * 2 * ss + j * LANES, LANES)] = hr[j]
            hfin_ref[0, :, pl.ds(sb * 2 * ss + ss + j * LANES, LANES)] = hi[j]
        hin = jnp.concatenate([hin_s[sb, j] for j in range(2 * half)], axis=1).astype(BF16)
        for i in range(S5_LAGS):
            acc = _dot(hin, r_s[sb, :, i * S5_TILE:(i + 1) * S5_TILE])
            for j in range(i + 1):
                acc += _dot(x_tiles[j], m_s[sb, i - j])
            for k in range(tk):
                y_parts[tk * i + k][sb] = acc[:, k * S5_SUB:(k + 1) * S5_SUB]
    d_row = d_ref[0]
    for t in range(t_chunk):
        y_t = jnp.concatenate(y_parts[t], axis=1) + d_row * xs[t]
        for b in range(bsz):
            token(y_ref, b, t)[...] = y_t[b * rc:(b + 1) * rc]


def _s5_prompt(u, ops, rc):
    bsz, seqlen, d = u.shape
    n_lb = d // LANES
    mc, sc, rc_op, a16, dl = ops
    ss = S5_SUB_STATE
    t_chunk = S5_CHUNK
    tokens = rc * t_chunk
    rows = bsz * rc
    act = pl.BlockSpec((bsz, tokens, LANES), lambda lb, ct: (0, ct, lb))
    per_lb = lambda shape: pl.BlockSpec((1,) + shape, lambda lb, ct: (lb,) + (0,) * len(shape))
    state_tiles = 2 * ss // LANES
    return pl.pallas_call(
        _s5_prompt_body,
        grid=(n_lb, seqlen // tokens),
        in_specs=[act, per_lb(mc.shape[1:]), per_lb(sc.shape[1:]), per_lb(rc_op.shape[1:]),
                  per_lb(a16.shape[1:]), per_lb((1, LANES))],
        out_specs=[act, per_lb((bsz, S5_SUBS * 2 * ss))],
        out_shape=[jax.ShapeDtypeStruct(u.shape, F32),
                   jax.ShapeDtypeStruct((n_lb, bsz, S5_SUBS * 2 * ss), F32)],
        scratch_shapes=[pltpu.VMEM((S5_SUBS, S5_LAGS, S5_TILE, S5_TILE), BF16),
                        pltpu.VMEM((S5_SUBS, t_chunk * S5_SUB, 2 * ss), BF16),
                        pltpu.VMEM((S5_SUBS, 2 * ss, t_chunk * S5_SUB), BF16),
                        pltpu.VMEM((S5_SUBS, state_tiles, rows, LANES), F32),
                        pltpu.VMEM((S5_SUBS, state_tiles, rows, LANES), F32),
                        pltpu.VMEM((S5_SUBS, 2, ss // LANES, bsz, LANES), F32)],
        compiler_params=_cparams(2),
        name="s5_prompt",
    )(u, mc, sc, rc_op, a16, dl)


def _s5_discretize(lam_re, lam_im, log_dt, b_re, b_im):
    dt = jnp.exp(log_dt)[:, None]
    mag = jnp.exp(lam_re * dt)
    ar = mag * jnp.cos(lam_im * dt)
    ai = mag * jnp.sin(lam_im * dt)
    den = lam_re * lam_re + lam_im * lam_im
    nr = ar - 1.0
    coef_re = (nr * lam_re + ai * lam_im) / den
    coef_im = (ai * lam_re - nr * lam_im) / den
    bb_re = coef_re[..., None] * b_re - coef_im[..., None] * b_im
    bb_im = coef_re[..., None] * b_im + coef_im[..., None] * b_re
    return ar, ai, bb_re, bb_im


def _s5_prompt_operators(ar, ai, bb_re, bb_im, c_re, c_im, d_skip):
    t = S5_CHUNK
    g, n = ar.shape
    p = SSM_GROUP
    lg, tk, n_lag = S5_SUB_GROUPS, S5_TILE_TOKENS, S5_LAGS
    n_lb = g * p // LANES
    pr, pi = [jnp.ones_like(ar)], [jnp.zeros_like(ar)]
    for _ in range(t):
        pr, pi = pr + [pr[-1] * ar - pi[-1] * ai], pi + [pr[-1] * ai + pi[-1] * ar]
    pr = jnp.stack(pr)
    pi = jnp.stack(pi)
    cp_re = c_re[None] * pr[:, :, None, :] - c_im[None] * pi[:, :, None, :]
    cp_im = c_re[None] * pi[:, :, None, :] + c_im[None] * pr[:, :, None, :]
    k = (jnp.einsum('tgpn,gnq->gtpq', cp_re[:t], bb_re, precision=HIGHEST)
         - jnp.einsum('tgpn,gnq->gtpq', cp_im[:t], bb_im, precision=HIGHEST))
    k_lag = lambda lag: k[:, lag] if lag >= 0 else jnp.zeros_like(k[:, 0])
    kt = jnp.stack([jnp.stack([jnp.stack([k_lag(tk * dl + tt - ss) for tt in range(tk)], axis=1)
                               for ss in range(tk)], axis=1) for dl in range(n_lag)], axis=1)
    kt = kt.reshape(n_lb, S5_SUBS, lg, n_lag, tk, tk, p, p)
    m = kt.transpose(0, 1, 3, 4, 7, 5, 2, 6).reshape(n_lb, S5_SUBS, n_lag, tk * p, tk * lg * p)
    m = jnp.pad(m, ((0, 0), (0, 0), (0, 0), (0, S5_MC_ROWS - tk * p), (0, 0)))
    pr_rev = jnp.stack([pr[t - 1 - s] for s in range(t)])
    pi_rev = jnp.stack([pi[t - 1 - s] for s in range(t)])
    s_re = pr_rev[:, :, :, None] * bb_re[None] - pi_rev[:, :, :, None] * bb_im[None]
    s_im = pr_rev[:, :, :, None] * bb_im[None] + pi_rev[:, :, :, None] * bb_re[None]
    s_ri = jnp.stack([s_re, s_im]).reshape(2, t, n_lb, S5_SUBS, lg, n, p)
    s = s_ri.transpose(2, 3, 1, 4, 6, 0, 5).reshape(n_lb, S5_SUBS, t * lg * p, 2 * n)
    r_ri = jnp.stack([cp_re[1:], -cp_im[1:]]).reshape(2, t, n_lb, S5_SUBS, lg, p, n)
    r = r_ri.transpose(2, 3, 0, 6, 1, 4, 5).reshape(n_lb, S5_SUBS, 2 * n, t * lg * p)
    a16 = jnp.stack([pr[t].reshape(n_lb, S5_SUBS, lg * n), pi[t].reshape(n_lb, S5_SUBS, lg * n)], axis=2)
    return (m.astype(BF16), s.astype(BF16), r.astype(BF16), a16, d_skip.reshape(n_lb, 1, LANES))


def _s5_step_body(u_ref, h0r_ref, h0i_ref, ar_ref, ai_ref, bbr_ref, bbi_ref, cr_ref, ci_ref,
                  d_ref, y_ref, hr_ref, hi_ref):
    u = u_ref[0]
    ar, ai = ar_ref[0], ai_ref[0]
    h0r, h0i = h0r_ref[0], h0i_ref[0]
    hr = _dot_nt(u, bbr_ref[0], HIGHEST) + ar * h0r - ai * h0i
    hi = _dot_nt(u, bbi_ref[0], HIGHEST) + ar * h0i + ai * h0r
    hr_ref[0] = hr
    hi_ref[0] = hi
    y_ref[0] = (_dot_nt(hr, cr_ref[0], HIGHEST) - _dot_nt(hi, ci_ref[0], HIGHEST)
                + d_ref[0] * u)


def _s5_step(u_g, h0r, h0i, ar, ai, bb_re, bb_im, c_re, c_im, d_skip):
    g, b, p = u_g.shape
    n = ar.shape[-1]
    per_group = lambda shape: pl.BlockSpec((1,) + shape, lambda i: (i, 0, 0))
    return pl.pallas_call(
        _s5_step_body,
        grid=(g,),
        in_specs=[per_group((b, p)), per_group((b, n)), per_group((b, n)),
                  per_group((1, n)), per_group((1, n)),
                  per_group((n, p)), per_group((n, p)),
                  per_group((p, n)), per_group((p, n)), per_group((1, p))],
        out_specs=[per_group((b, p)), per_group((b, n)), per_group((b, n))],
        out_shape=[jax.ShapeDtypeStruct((g, b, p), F32),
                   jax.ShapeDtypeStruct((g, b, n), F32),
                   jax.ShapeDtypeStruct((g, b, n), F32)],
        compiler_params=_cparams(1),
        name="s5_step",
    )(u_g, h0r, h0i, ar.reshape(g, 1, n), ai.reshape(g, 1, n), bb_re, bb_im, c_re, c_im,
      d_skip.reshape(g, 1, p))


def _post_a_body(y_ref, ga_ref, x_ref, wglu_ref, bglu_ref, wout_ref, gpost_ref, x1_ref, *, prec):
    y = _tile(y_ref)
    wdt = wglu_ref.dtype
    gy = 0.5 * y * (1.0 + lax.erf(y * math.sqrt(0.5)))
    z = _dot(gy.astype(wdt), wglu_ref[...], prec) + bglu_ref[...]
    gate = _tile(ga_ref)
    t = (gy * _sigmoid(z)) * (gate * _sigmoid(gate))
    o = _dot(t.astype(wdt), wout_ref[...], prec)
    x1_ref[...] = (_tile(x_ref) + _rms_scale(o) * gpost_ref[...]).reshape(x1_ref.shape)


def _post_a(y, gate, x, w_glu, b_glu, w_out, g_post, prec, grid, y_spec, x_spec):
    d = w_glu.shape[0]
    return pl.pallas_call(
        functools.partial(_post_a_body, prec=prec),
        grid=grid,
        in_specs=[y_spec, x_spec, x_spec, _full((d, d)), _full((1, d)), _full((d, d)), _full((1, d))],
        out_specs=x_spec,
        out_shape=jax.ShapeDtypeStruct(x.shape, F32),
        compiler_params=_cparams(len(grid)),
        name="post_a",
    )(y, gate, x, w_glu, b_glu.reshape(1, d), w_out, g_post.reshape(1, d))


ALIBI_ROWS = 16
FEAT_OFF = HEAD_DIM
V_ROWS = HEAD_DIM + 16


def _aug_width(nb):
    nbp = -(-nb // 16) * 16
    return -(-(FEAT_OFF + ALIBI_ROWS + nbp) // LANES) * LANES, nbp


def _pre_b_body(x1_ref, gkv_ref, wkv_ref, gb_ref, winb_ref, k_ref, v_ref, gate_ref, *rest, prec, nb):
    xn = _rms_scale(x1_ref[...])
    wdt = wkv_ref.dtype
    hk = (xn * gkv_ref[...]).astype(wdt)
    k = _dot(hk, wkv_ref[:, :KV_WIDTH], prec)
    v = _dot(hk, wkv_ref[:, KV_WIDTH:], prec)
    hq = (xn * gb_ref[...]).astype(wdt)
    q = _dot(hq, winb_ref[:, :D_MODEL], prec) * SCALE
    gate_ref[...] = _dot(hq, winb_ref[:, D_MODEL:], prec)
    if nb is None:
        q_ref, = rest
        q_ref[...] = q
        k_ref[...] = k
        v_ref[...] = v
        return
    qt_ref, ka_ref, vta_ref, km_ref, st_ref = rest
    blk = MOBA_BLOCK
    width = ka_ref.shape[-1] - FEAT_OFF
    lane = lax.broadcasted_iota(jnp.int32, (blk, width), 1)
    rowi = lax.broadcasted_iota(jnp.int32, (blk, width), 0)
    ones_row = (lax.broadcasted_iota(jnp.int32, (V_ROWS - HEAD_DIM, blk), 0) == 0).astype(F32).astype(BF16)
    pick = lambda wd: (_iota2((wd, LANES), 0) // HEAD_DIM == _iota2((wd, LANES), 1)).astype(F32).astype(BF16)
    sq_norms = lambda x: jnp.max(_dot((x * x).astype(BF16), pick(x.shape[1])), axis=0, keepdims=True)
    blocks = q.shape[0] // blk
    for j in range(blocks):
        rows = slice(j * blk, (j + 1) * blk)
        qj, kj, vj = q[rows], k[rows], v[rows]
        qt_ref[0, j] = qj.T.astype(BF16)
        km_ref[j] = jnp.mean(kj, axis=0, keepdims=True)
        st_ref[j, 0:1, :] = sq_norms(qj)
        st_ref[j, 1:2, :] = sq_norms(kj)
        n = (pl.program_id(0) * blocks + j) % nb
        feat = jnp.where(lane < 3, n, jnp.where(lane < 6, rowi, jnp.where(lane < 9, 1, 0)))
        feat = jnp.where(lane - ALIBI_ROWS == n, 1, feat).astype(F32).astype(BF16)
        kb = kj.astype(BF16)
        vt = vj.T
        k_ref[0, :, rows] = kj.T
        v_ref[0, :, rows] = vt
        for h in range(KV_HEADS):
            ka_ref[0, h, rows, :FEAT_OFF] = kb[:, h * HEAD_DIM:(h + 1) * HEAD_DIM]
            ka_ref[0, h, rows, FEAT_OFF:] = feat
            vta_ref[0, j, h * V_ROWS:h * V_ROWS + HEAD_DIM, :] = vt[h * HEAD_DIM:(h + 1) * HEAD_DIM].astype(BF16)
            vta_ref[0, j, h * V_ROWS + HEAD_DIM:(h + 1) * V_ROWS, :] = ones_row


def _pre_b(x1, g_kv, w_kv, g_pre_b, w_in_b, tm, prec, prompt_blocks=None):
    m, d = x1.shape
    row = lambda wd: pl.BlockSpec((tm, wd), lambda i: (i, 0))
    nb = None
    if prompt_blocks is None:
        kv_spec = row(KV_WIDTH)
        kv_shape = jax.ShapeDtypeStruct((m, KV_WIDTH), F32)
        out_specs = [kv_spec, kv_spec, row(d), row(d)]
        out_shape = [kv_shape, kv_shape, jax.ShapeDtypeStruct((m, d), F32), jax.ShapeDtypeStruct((m, d), F32)]
    else:
        bsz, nb = prompt_blocks
        blk = MOBA_BLOCK
        bps = tm // blk
        steps = nb // bps
        assert tm == bps * blk and nb == steps * bps and m == bsz * nb * blk
        aug, _ = _aug_width(nb)
        kv_spec = pl.BlockSpec((1, KV_WIDTH, tm), lambda i: (i // steps, 0, i % steps))
        kv_shape = jax.ShapeDtypeStruct((bsz, KV_WIDTH, nb * blk), F32)
        out_specs = [kv_spec, kv_spec, row(d)]
        out_shape = [kv_shape, kv_shape, jax.ShapeDtypeStruct((m, d), F32)]
        out_specs += [pl.BlockSpec((1, bps, d, blk), lambda i: (i // steps, i % steps, 0, 0)),
                      pl.BlockSpec((1, KV_HEADS, tm, aug), lambda i: (i // steps, 0, i % steps, 0)),
                      pl.BlockSpec((1, bps, KV_HEADS * V_ROWS, blk), lambda i: (i // steps, i % steps, 0, 0)),
                      pl.BlockSpec((bps, 1, KV_WIDTH), lambda i: (i, 0, 0)),
                      pl.BlockSpec((bps, 2, LANES), lambda i: (i, 0, 0))]
        out_shape += [jax.ShapeDtypeStruct((bsz, nb, d, blk), BF16),
                      jax.ShapeDtypeStruct((bsz, KV_HEADS, nb * blk, aug), BF16),
                      jax.ShapeDtypeStruct((bsz, nb, KV_HEADS * V_ROWS, blk), BF16),
                      jax.ShapeDtypeStruct((bsz * nb, 1, KV_WIDTH), F32),
                      jax.ShapeDtypeStruct((bsz * nb, 2, LANES), F32)]
    return pl.pallas_call(
        functools.partial(_pre_b_body, prec=prec, nb=nb),
        grid=(m // tm,),
        in_specs=[row(d), _full((1, d)), _full((d, 2 * KV_WIDTH)), _full((1, d)), _full((d, 2 * d))],
        out_specs=out_specs,
        out_shape=out_shape,
        compiler_params=_cparams(1),
        name="pre_b",
    )(x1, g_kv.reshape(1, d), w_kv, g_pre_b.reshape(1, d), w_in_b)


def _top3_mask(gs, idx, valid, axis):
    n = gs.shape[axis]
    gs = jnp.where(valid, gs, NEG_INF)
    sel = jnp.zeros(gs.shape, F32)
    picks = []
    for _ in range(MOBA_TOPK):
        mx = jnp.max(gs, axis=axis, keepdims=True)
        first = jnp.min(jnp.where(gs == mx, idx, n), axis=axis, keepdims=True)
        pick = idx == first
        sel = jnp.where(pick & valid, 1.0, sel)
        gs = jnp.where(pick, REMOVED, gs)
        picks.append(first)
    return sel, picks


def _split3(x):
    hi = x.astype(BF16).astype(F32)
    mid = (x - hi).astype(BF16).astype(F32)
    lo = (x - hi - mid).astype(BF16).astype(F32)
    return [hi, mid, lo]


def _moba_prompt_body(first_ref, qt_ref, g_ref, ka_ref, vta_ref, km_ref, o_ref, qa_s, qd_s):
    h = pl.program_id(1)
    c = pl.program_id(2)
    blk = MOBA_BLOCK
    nb = km_ref.shape[2]
    aug, nbp = _aug_width(nb)
    km = km_ref[0, 0]
    row = lax.broadcasted_iota(jnp.int32, (nb, blk), 0)
    frow = lax.broadcasted_iota(jnp.int32, (ALIBI_ROWS, blk), 0)
    pos_q = (c * blk + lax.broadcasted_iota(jnp.int32, (1, blk), 1)).astype(F32)
    for g in range(Q_PER_KV):
        lanes = slice(g * blk, (g + 1) * blk)
        qg = qt_ref[0, 0, g * HEAD_DIM:(g + 1) * HEAD_DIM, :]
        sel, _ = _top3_mask(_dot(km, qg.astype(F32), HIGHEST), row, row < c, 0)
        bias = jnp.where(sel > 0.0, 0.0, NEG_INF)
        if nbp > nb:
            bias = jnp.concatenate([bias, jnp.zeros((nbp - nb, blk), F32)], axis=0)
        head1 = (h * Q_PER_KV + g + 1).astype(F32)
        slope = jnp.exp2(-8.0 * jnp.full((1, blk), head1, F32) / N_HEADS)
        terms = _split3(slope * float(blk)) + _split3(slope) + _split3(-slope * pos_q)
        feat = jnp.zeros((ALIBI_ROWS, blk), F32)
        for r, term in enumerate(terms):
            feat = jnp.where(frow == r, term, feat)
        for ref, b_rows in ((qa_s, bias.astype(BF16)), (qd_s, jnp.zeros((nbp, blk), BF16))):
            ref[:FEAT_OFF, lanes] = qg
            ref[FEAT_OFF:FEAT_OFF + ALIBI_ROWS, lanes] = feat.astype(BF16)
            ref[FEAT_OFF + ALIBI_ROWS:FEAT_OFF + ALIBI_ROWS + nbp, lanes] = b_rows
            if FEAT_OFF + ALIBI_ROWS + nbp < aug:
                ref[FEAT_OFF + ALIBI_ROWS + nbp:, lanes] = jnp.zeros((aug - FEAT_OFF - ALIBI_ROWS - nbp, blk), BF16)

    def keys(n):
        return ka_ref[0, 0, pl.ds(pl.multiple_of(n * blk, blk), blk), :]

    s = _dot(keys(c), qd_s[...])
    key = lax.broadcasted_iota(jnp.int32, s.shape, 0)
    qry = lax.broadcasted_iota(jnp.int32, s.shape, 1) & (blk - 1)
    s = jnp.where(key <= qry, s, NEG_INF)
    m = jnp.max(s, axis=0, keepdims=True)
    acc = _dot(vta_ref[0, c], jnp.exp(s - m).astype(BF16))

    def block_pair(i, carry):
        m, acc = carry
        n0, n1 = 2 * i, 2 * i + 1
        s0 = _dot(keys(n0), qa_s[...])
        s1 = _dot(keys(n1), qa_s[...])
        m2 = jnp.maximum(m, jnp.maximum(jnp.max(s0, axis=0, keepdims=True), jnp.max(s1, axis=0, keepdims=True)))
        p0 = jnp.exp(s0 - m2).astype(BF16)
        p1 = jnp.exp(s1 - m2).astype(BF16)
        return m2, jnp.exp(m - m2) * acc + _dot(vta_ref[0, n0], p0) + _dot(vta_ref[0, n1], p1)

    first = first_ref[(pl.program_id(0) * KV_HEADS + h) * nb + c]
    m, acc = lax.fori_loop(first, (c + 1) // 2, block_pair, (m, acc))
    out_t = acc[:HEAD_DIM] / acc[HEAD_DIM:HEAD_DIM + 1]
    att = jnp.concatenate([out_t[:, g * blk:(g + 1) * blk] for g in range(Q_PER_KV)], axis=0).T
    gate = g_ref[0]
    o_ref[0] = (att * (gate * _sigmoid(gate))).astype(o_ref.dtype)


_SKIP_LOGIT = -120.0
_NORM_SLACK = 1.02


def _first_needed_pair(stats, bsz, nb):
    q_max = jnp.sqrt(stats[:, 0, :N_HEADS]).reshape(bsz, nb, KV_HEADS, Q_PER_KV) * _NORM_SLACK
    k_max = jnp.sqrt(stats[:, 1, :KV_HEADS]).reshape(bsz, nb, KV_HEADS) * _NORM_SLACK
    slopes = jnp.exp2(-8.0 * jnp.arange(1, N_HEADS + 1, dtype=F32) / N_HEADS).reshape(KV_HEADS, Q_PER_KV)
    c = jnp.arange(nb)[:, None]
    n = jnp.arange(nb)[None, :]
    min_dist = ((c - n) * MOBA_BLOCK - (MOBA_BLOCK - 1)).astype(F32)
    k_pair = k_max[:, None, :, :] + k_max[:, :, None, :]
    bound = (q_max[:, :, None, :, :] * k_pair[..., None]
             - slopes[None, None, None] * min_dist[None, :, :, None, None])
    dead = jnp.all(bound < _SKIP_LOGIT, axis=-1) & (n < c)[None, :, :, None]
    lead = jnp.min(jnp.where(dead, nb, n[None, :, :, None]), axis=2)
    return (lead // 2).transpose(0, 2, 1).reshape(-1).astype(jnp.int32)


def _moba_prompt(qt, gate, ka, vta, km, first_pair):
    bsz, seqlen, d = gate.shape
    nb = seqlen // MOBA_BLOCK
    width = Q_PER_KV * HEAD_DIM
    aug, _ = _aug_width(nb)
    tile = pl.BlockSpec((1, MOBA_BLOCK, width), lambda b, h, c, fp: (b, c, h))
    return pl.pallas_call(
        _moba_prompt_body,
        grid_spec=pltpu.PrefetchScalarGridSpec(
            num_scalar_prefetch=1,
            grid=(bsz, KV_HEADS, nb),
            in_specs=[pl.BlockSpec((1, 1, width, MOBA_BLOCK), lambda b, h, c, fp: (b, c, h, 0)),
                      tile,
                      pl.BlockSpec((1, 1, seqlen, aug), lambda b, h, c, fp: (b, h, 0, 0)),
                      pl.BlockSpec((1, nb, V_ROWS, MOBA_BLOCK), lambda b, h, c, fp: (b, 0, h, 0)),
                      pl.BlockSpec((1, 1, nb, HEAD_DIM), lambda b, h, c, fp: (b, h, 0, 0))],
            out_specs=tile,
            scratch_shapes=[pltpu.VMEM((aug, Q_PER_KV * MOBA_BLOCK), BF16) for _ in range(2)]),
        out_shape=jax.ShapeDtypeStruct((bsz, seqlen, d), BF16),
        compiler_params=_cparams(3),
        name="moba_prompt",
    )(first_pair, qt, gate, ka, vta, km)


_KM_BLOCKS = 16


def _block_mean_body(pt_ref, *refs):
    del pt_ref
    pages, o_ref = refs[:-1], refs[-1]
    s = pl.program_id(1)

    @pl.when(s == 0)
    def _():
        o_ref[...] = jnp.zeros(o_ref.shape, F32)

    lane = lax.broadcasted_iota(jnp.int32, o_ref.shape[1:], 2)
    acc = o_ref[0]
    for j in range(_KM_BLOCKS):
        tot = sum(pages[PAGES_PER_BLOCK * j + r][0] for r in range(PAGES_PER_BLOCK))
        mean = jnp.sum(tot, axis=-1, keepdims=True) * (1.0 / MOBA_BLOCK)
        acc = jnp.where(lane == s * _KM_BLOCKS + j, mean, acc)
    o_ref[0] = acc


def _block_means(cache_t, page_table, n_blocks):
    n_seq = page_table.shape[0]
    assert n_blocks % _KM_BLOCKS == 0
    per_step = _KM_BLOCKS * PAGES_PER_BLOCK
    page_spec = lambda j: pl.BlockSpec(
        (1, KV_HEADS, HEAD_DIM, PAGE_SIZE), lambda b, s, pt, j=j: (pt[b, s * per_step + j], 0, 0, 0))
    return pl.pallas_call(
        _block_mean_body,
        grid_spec=pltpu.PrefetchScalarGridSpec(
            num_scalar_prefetch=1,
            grid=(n_seq, n_blocks // _KM_BLOCKS),
            in_specs=[page_spec(j) for j in range(per_step)],
            out_specs=pl.BlockSpec((1, KV_HEADS, HEAD_DIM, n_blocks), lambda b, s, pt: (b, 0, 0, 0))),
        out_shape=jax.ShapeDtypeStruct((n_seq, KV_HEADS, HEAD_DIM, n_blocks), F32),
        compiler_params=_cparams(2),
        name="block_means",
    )(page_table, *([cache_t] * per_step))


def _sample_topk_body(q_ref, km_ref, pt_ref, sel_ref, *, n_past):
    gs = jnp.concatenate([_dot(q_ref[0, k], km_ref[0, k], HIGHEST) for k in range(KV_HEADS)], axis=0)
    lane = lax.broadcasted_iota(jnp.int32, gs.shape, 1)
    _, picks = _top3_mask(gs, lane, lane < n_past, 1)
    pages = pt_ref[0].astype(F32)
    page_lane = lax.broadcasted_iota(jnp.int32, (N_HEADS, pages.shape[1]), 1)
    out_lane = lax.broadcasted_iota(jnp.int32, sel_ref.shape[1:], 1)
    out = jnp.zeros(sel_ref.shape[1:], jnp.int32)
    for r, first in enumerate(picks):
        out = jnp.where(out_lane == r, first, out)
        for pg in range(PAGES_PER_BLOCK):
            hit = page_lane == first * PAGES_PER_BLOCK + pg
            page = jnp.sum(jnp.where(hit, pages, 0.0), axis=1, keepdims=True).astype(jnp.int32)
            out = jnp.where(out_lane == MOBA_TOPK + PAGES_PER_BLOCK * r + pg, page, out)
    sel_ref[0] = out


def _sample_topk(q, km_t, page_table):
    n_seq, _, _, n_past = km_t.shape
    n_pages = page_table.shape[1]
    return pl.pallas_call(
        functools.partial(_sample_topk_body, n_past=n_past),
        grid=(n_seq,),
        in_specs=[pl.BlockSpec((1,) + q.shape[1:], lambda b: (b, 0, 0, 0)),
                  pl.BlockSpec((1,) + km_t.shape[1:], lambda b: (b, 0, 0, 0)),
                  pl.BlockSpec((1, 1, n_pages), lambda b: (b, 0, 0))],
        out_specs=pl.BlockSpec((1, N_HEADS, LANES), lambda b: (b, 0, 0)),
        out_shape=jax.ShapeDtypeStruct((n_seq, N_HEADS, LANES), jnp.int32),
        compiler_params=_cparams(1),
        name="sample_topk",
    )(q, km_t, page_table.reshape(n_seq, 1, n_pages))


_Q_ROWS = 8


def _moba_sample_body(idx_ref, pages_ref, q_ref, kn_ref, vn_ref, *refs, n_past):
    del pages_ref
    n_sel = MOBA_TOPK * PAGES_PER_BLOCK
    k_pages, v_pages, o_ref = refs[:n_sel], refs[n_sel:2 * n_sel], refs[-1]
    b = pl.program_id(0)
    h = pl.program_id(1)
    keys = MOBA_TOPK * MOBA_BLOCK
    q = jnp.broadcast_to(q_ref[0, 0], (_Q_ROWS, HEAD_DIM))
    kt_sel = jnp.concatenate([r[0, 0] for r in k_pages], axis=1)
    vt_sel = jnp.concatenate([r[0, 0] for r in v_pages], axis=1)
    col = lax.broadcasted_iota(jnp.int32, (_Q_ROWS, keys), 1)
    slot = col // MOBA_BLOCK
    base = (b * N_HEADS + h) * MOBA_TOPK
    idx = jnp.zeros((_Q_ROWS, keys), jnp.int32)
    for r in range(MOBA_TOPK):
        idx = jnp.where(slot == r, idx_ref[base + r], idx)
    pos = n_past * MOBA_BLOCK
    dist = (pos - (idx * MOBA_BLOCK + col % MOBA_BLOCK)).astype(F32)
    slope = jnp.exp2(-8.0 * jnp.full((_Q_ROWS, keys), (h + 1).astype(F32), F32) / N_HEADS)
    s = _dot(q.astype(BF16), kt_sel.astype(BF16)) - slope * dist
    s = jnp.where(idx < n_past, s, NEG_INF)
    s_own = jnp.sum(q * kn_ref[0, 0], axis=-1, keepdims=True)
    m = jnp.maximum(jnp.max(s, axis=-1, keepdims=True), s_own)
    p = jnp.exp(s - m)
    p_own = jnp.exp(s_own - m)
    l = jnp.sum(p, axis=-1, keepdims=True) + p_own
    out = (_dot_nt(p.astype(BF16), vt_sel.astype(BF16)) + p_own * vn_ref[0, 0]) / l
    o_ref[0, 0] = out[0:1]


def _moba_sample(q, k_new, v_new, cache_kt, cache_vt, idx_flat, pages_flat, n_past):
    n_seq = q.shape[0]
    n_sel = MOBA_TOPK * PAGES_PER_BLOCK

    def page_spec(j):
        return pl.BlockSpec((1, 1, HEAD_DIM, PAGE_SIZE),
                            lambda b, h, idx, pages: (pages[(b * N_HEADS + h) * n_sel + j], h // Q_PER_KV, 0, 0))

    row = (1, 1, 1, HEAD_DIM)
    new_spec = pl.BlockSpec(row, lambda b, h, idx, pages: (b, h // Q_PER_KV, 0, 0))
    head_spec = pl.BlockSpec(row, lambda b, h, idx, pages: (b, h, 0, 0))
    return pl.pallas_call(
        functools.partial(_moba_sample_body, n_past=n_past),
        grid_spec=pltpu.PrefetchScalarGridSpec(
            num_scalar_prefetch=2,
            grid=(n_seq, N_HEADS),
            in_specs=[head_spec, new_spec, new_spec] + [page_spec(j) for j in range(n_sel)] * 2,
            out_specs=head_spec),
        out_shape=jax.ShapeDtypeStruct((n_seq, N_HEADS, 1, HEAD_DIM), F32),
        compiler_params=_cparams(2),
        name="moba_sample",
    )(idx_flat, pages_flat, q, k_new, v_new, *([cache_kt] * n_sel), *([cache_vt] * n_sel))


def _out_b_body(a_ref, *refs, prec, gated):
    if gated:
        x1_ref, w_ref, g_ref, y_ref = refs
        a = a_ref[...]
    else:
        gate_ref, x1_ref, w_ref, g_ref, y_ref = refs
        gate = gate_ref[...]
        a = (a_ref[...] * (gate * _sigmoid(gate))).astype(w_ref.dtype)
    o = _dot(a, w_ref[...], prec)
    y_ref[...] = x1_ref[...] + _rms_scale(o) * g_ref[...]


def _out_b(att, gate, x1, w, g_post, tm, prec):
    m, d = x1.shape
    tile = pl.BlockSpec((tm, d), lambda i: (i, 0))
    gated = gate is None
    acts = [att, x1] if gated else [att, gate, x1]
    return pl.pallas_call(
        functools.partial(_out_b_body, prec=prec, gated=gated),
        grid=(m // tm,),
        in_specs=[tile] * len(acts) + [_full((d, d)), _full((1, d))],
        out_specs=tile,
        out_shape=jax.ShapeDtypeStruct((m, d), F32),
        compiler_params=_cparams(1),
        name="out_b",
    )(*acts, w, g_post.reshape(1, d))


TM_IN_A = 1024
TM_POST_A = 512
TM_OUT_B = 1024
TM_PRE_B = 2 * MOBA_BLOCK
S5_CHUNKS_PER_STEP = 64


def _prompt_trunk(x, p):
    bsz, seqlen, d = x.shape
    m = bsz * seqlen
    nb = seqlen // MOBA_BLOCK
    nc = seqlen // S5_CHUNK
    x2 = x.reshape(m, d)
    tm = TM_IN_A
    tile = pl.BlockSpec((tm, d), lambda i: (i, 0))
    u, gate_a = _norm_matmul(x2, p['g_pre_a'], p['w_in_a'].astype(BF16), None, (m // tm,), tile, [tile, tile],
                             [jax.ShapeDtypeStruct((m, d), F32)] * 2)
    ar, ai, bb_re, bb_im = _s5_discretize(p['lambda_re'], p['lambda_im'], p['log_dt'], p['b_re'], p['b_im'])
    ops = _s5_prompt_operators(ar, ai, bb_re, bb_im, p['c_re'], p['c_im'], p['d_skip'])
    y, h_fin = _s5_prompt(u.reshape(bsz, seqlen, d), ops, min(nc, S5_CHUNKS_PER_STEP))
    h_fin = h_fin.reshape(d // LANES, bsz, S5_SUBS, 2, S5_SUB_GROUPS, SSM_STATE).transpose(3, 1, 0, 2, 4, 5)
    h_fin = h_fin.reshape(2, 1, bsz, SSM_GROUPS, SSM_STATE)
    tm = TM_POST_A
    tile = pl.BlockSpec((tm, d), lambda i: (i, 0))
    x1 = _post_a(y.reshape(m, d), gate_a, x2, p['w_glu'].astype(BF16), p['b_glu'], p['w_out_a'].astype(BF16),
                 p['g_post_a'], None, (m // tm,), tile, tile)
    k, v, gate_b, qt, ka, vta, km, stats = _pre_b(x1, p['g_kv'], p['w_kv'].astype(BF16), p['g_pre_b'],
                                                  p['w_in_b'].astype(BF16), TM_PRE_B, None, (bsz, nb))
    km = km.reshape(bsz, nb, KV_HEADS, HEAD_DIM).transpose(0, 2, 1, 3)
    att = _moba_prompt(qt, gate_b.reshape(bsz, seqlen, d), ka, vta, km, _first_needed_pair(stats, bsz, nb))
    y_out = _out_b(att.reshape(m, d), None, x1, p['w_out_b'].astype(BF16), p['g_post_b'], TM_OUT_B, None)
    to_out = lambda t: t.reshape(bsz, KV_HEADS, HEAD_DIM, seqlen).transpose(0, 3, 1, 2)
    return (y_out.reshape(bsz, seqlen, d), h_fin[0], h_fin[1], to_out(k), to_out(v))


def _sample_trunk(x, h0_re, h0_im, cache_k, cache_v, page_table, p):
    n_seq, seqlen, d = x.shape
    assert seqlen == 1
    g, pch = SSM_GROUPS, SSM_GROUP
    n_past = page_table.shape[1] // PAGES_PER_BLOCK
    x2 = x.reshape(n_seq, d)
    whole = pl.BlockSpec((n_seq, d), lambda i: (0, 0))
    u, gate_a = _norm_matmul(x2, p['g_pre_a'], p['w_in_a'], HIGHEST, (1,), whole, [whole, whole],
                             [jax.ShapeDtypeStruct((n_seq, d), F32)] * 2)
    ar, ai, bb_re, bb_im = _s5_discretize(p['lambda_re'], p['lambda_im'], p['log_dt'], p['b_re'], p['b_im'])
    y_g, hr, hi = _s5_step(u.reshape(n_seq, g, pch).transpose(1, 0, 2),
                           h0_re.transpose(1, 0, 2), h0_im.transpose(1, 0, 2),
                           ar, ai, bb_re, bb_im, p['c_re'], p['c_im'], p['d_skip'])
    y = y_g.transpose(1, 0, 2).reshape(n_seq, d)
    x1 = _post_a(y, gate_a, x2, p['w_glu'], p['b_glu'], p['w_out_a'], p['g_post_a'], HIGHEST, (1,), whole, whole)
    k, v, gate_b, q = _pre_b(x1, p['g_kv'], p['w_kv'], p['g_pre_b'], p['w_in_b'], n_seq, HIGHEST)
    cache_kt = cache_k.transpose(0, 2, 3, 1)
    cache_vt = cache_v.transpose(0, 2, 3, 1)
    km_t = _block_means(cache_kt, page_table, n_past)
    sel = _sample_topk(q.reshape(n_seq, KV_HEADS, Q_PER_KV, HEAD_DIM), km_t, page_table)
    n_sel = MOBA_TOPK * PAGES_PER_BLOCK
    att = _moba_sample(q.reshape(n_seq, N_HEADS, 1, HEAD_DIM), k.reshape(n_seq, KV_HEADS, 1, HEAD_DIM),
                       v.reshape(n_seq, KV_HEADS, 1, HEAD_DIM), cache_kt, cache_vt,
                       sel[:, :, :MOBA_TOPK].reshape(-1), sel[:, :, MOBA_TOPK:MOBA_TOPK + n_sel].reshape(-1), n_past)
    y_out = _out_b(att.reshape(n_seq, d), gate_b, x1, p['w_out_b'], p['g_post_b'], n_seq, HIGHEST)
    return (y_out.reshape(n_seq, 1, d), hr.transpose(1, 0, 2)[None], hi.transpose(1, 0, 2)[None],
            k.reshape(n_seq, 1, KV_HEADS, HEAD_DIM), v.reshape(n_seq, 1, KV_HEADS, HEAD_DIM))


def kernel(x_prompt, x_sample, state_ssm_re, state_ssm_im, cache_k, cache_v, page_table, g_pre_a, w_in_a, lambda_re, lambda_im, log_dt, b_re, b_im, c_re, c_im, d_skip, w_glu, b_glu, w_out_a, g_post_a, g_kv, w_kv, g_pre_b, w_in_b, w_out_b, g_post_b):
    p = {'g_pre_a': g_pre_a[0], 'w_in_a': w_in_a[0], 'lambda_re': lambda_re[0], 'lambda_im': lambda_im[0],
         'log_dt': log_dt[0], 'b_re': b_re[0], 'b_im': b_im[0], 'c_re': c_re[0], 'c_im': c_im[0],
         'd_skip': d_skip[0], 'w_glu': w_glu[0], 'b_glu': b_glu[0], 'w_out_a': w_out_a[0],
         'g_post_a': g_post_a[0], 'g_kv': g_kv, 'w_kv': w_kv, 'g_pre_b': g_pre_b[0],
         'w_in_b': w_in_b[0], 'w_out_b': w_out_b[0], 'g_post_b': g_post_b[0]}
    y_p, re_p, im_p, k_p, v_p = _prompt_trunk(x_prompt, p)
    y_s, re_s, im_s, k_s, v_s = _sample_trunk(x_sample, state_ssm_re[0], state_ssm_im[0],
                                              cache_k, cache_v, page_table, p)
    return (y_p, y_s, re_p, im_p, k_p, v_p, re_s, im_s, k_s, v_s)
```

```python
import functools
import math

import jax
import jax.numpy as jnp
from jax import lax
from jax.experimental import pallas as pl
from jax.experimental.pallas import tpu as pltpu

F32 = jnp.float32
BF16 = jnp.bfloat16
HIGHEST = lax.Precision.HIGHEST

D_MODEL = 1024
SSM_GROUP = 16
SSM_GROUPS = D_MODEL // SSM_GROUP
SSM_STATE = 64
HEAD_DIM = 64
N_HEADS = D_MODEL // HEAD_DIM
KV_HEADS = 4
Q_PER_KV = N_HEADS // KV_HEADS
KV_WIDTH = KV_HEADS * HEAD_DIM
MOBA_BLOCK = 256
MOBA_TOPK = 3
PAGE_SIZE = 128
PAGES_PER_BLOCK = MOBA_BLOCK // PAGE_SIZE
EPS = 1e-6
NEG_INF = -1e30
REMOVED = -3e38
SCALE = HEAD_DIM ** -0.5
S5_CHUNK = 16
LANES = 128
VMEM_LIMIT = 56 * 1024 * 1024


def _cparams(n_grid):
    return pltpu.CompilerParams(dimension_semantics=("arbitrary",) * n_grid,
                                vmem_limit_bytes=VMEM_LIMIT)


def _dot(a, b, prec=None):
    return jnp.dot(a, b, preferred_element_type=F32, precision=prec)


def _dot_nt(a, b, prec=None):
    return lax.dot_general(a, b, (((1,), (1,)), ((), ())),
                           preferred_element_type=F32, precision=prec)


def _rms_scale(x):
    return x * lax.rsqrt(jnp.mean(x * x, axis=-1, keepdims=True) + EPS)


def _sigmoid(x):
    return 1.0 / (1.0 + jnp.exp(-x))


def _full(shape):
    zeros = (0,) * len(shape)
    return pl.BlockSpec(shape, lambda *_: zeros)


def _tile(ref):
    x = ref[...]
    return x.reshape(x.shape[-2:])


def _norm_matmul_body(x_ref, g_ref, w_ref, *out_refs, prec):
    h = (_rms_scale(_tile(x_ref)) * g_ref[...]).astype(w_ref.dtype)
    off = 0
    for o_ref in out_refs:
        n = o_ref.shape[-1]
        o_ref[...] = _dot(h, w_ref[:, off:off + n], prec).astype(o_ref.dtype).reshape(o_ref.shape)
        off += n


def _norm_matmul(x, g, w, prec, grid, x_spec, out_specs, out_shape):
    d, n = w.shape
    return pl.pallas_call(
        functools.partial(_norm_matmul_body, prec=prec),
        grid=grid,
        in_specs=[x_spec, _full((1, d)), _full((d, n))],
        out_specs=out_specs,
        out_shape=out_shape,
        compiler_params=_cparams(len(grid)),
        name="norm_matmul",
    )(x, g.reshape(1, d), w)


S5_SUB = 64
S5_SUBS = LANES // S5_SUB
S5_SUB_GROUPS = S5_SUB // SSM_GROUP
S5_SUB_STATE = S5_SUB_GROUPS * SSM_STATE
S5_TILE = 2 * LANES
S5_TILE_TOKENS = S5_TILE // S5_SUB
S5_LAGS = S5_CHUNK // S5_TILE_TOKENS
S5_MC_ROWS = LANES


def _iota2(shape, dim):
    return lax.broadcasted_iota(jnp.int32, shape, dim)


def _expand_block_diagonal(sb, mc_ref, sc_ref, rc_ref, m_s, s_s, r_s):
    p, n, lg = SSM_GROUP, SSM_STATE, S5_SUB_GROUPS
    ss = S5_SUB_STATE
    grp = lambda idx, per: (idx // per) % lg
    _, rows, cols = s_s.shape
    i, j = _iota2((2 * n, cols), 0), _iota2((2 * n, cols), 1)
    spread = (i == (j // ss) * n + j % n).astype(F32).astype(BF16)
    blk = 256
    for r0 in range(0, rows, blk):
        full = _dot(sc_ref[0, sb, r0:r0 + blk, :], spread)
        keep = grp(_iota2((blk, cols), 0) + r0, p) == grp(_iota2((blk, cols), 1), n)
        s_s[sb, r0:r0 + blk, :] = jnp.where(keep, full, 0.0).astype(BF16)
    _, rows, cols = r_s.shape
    i, j = _iota2((rows, 2 * n), 0), _iota2((rows, 2 * n), 1)
    spread = (j == (i // ss) * n + i % n).astype(F32).astype(BF16)
    for c0 in range(0, cols, blk):
        full = _dot(spread, rc_ref[0, sb, :, c0:c0 + blk])
        keep = grp(_iota2((rows, blk), 0), n) == grp(_iota2((rows, blk), 1) + c0, p)
        r_s[sb, :, c0:c0 + blk] = jnp.where(keep, full, 0.0).astype(BF16)
    _, n_lag, rows, cols = m_s.shape
    i, j = _iota2((rows, S5_MC_ROWS), 0), _iota2((rows, S5_MC_ROWS), 1)
    spread = (j == (i // S5_SUB) * p + i % p).astype(F32).astype(BF16)
    keep = grp(_iota2((rows, cols), 0), p) == grp(_iota2((rows, cols), 1), p)
    for dl in range(n_lag):
        m_s[sb, dl] = jnp.where(keep, _dot(spread, mc_ref[0, sb, dl]), 0.0).astype(BF16)


def _s5_prompt_body(u_ref, mc_ref, sc_ref, rc_ref, a_ref, d_ref, y_ref, hfin_ref,
                    m_s, s_s, r_s, e_s, hin_s, h_s):
    bsz, tokens, _ = u_ref.shape
    t_chunk = S5_CHUNK
    rc = tokens // t_chunk
    ss = S5_SUB_STATE
    tk = S5_TILE_TOKENS

    @pl.when(pl.program_id(1) == 0)
    def _():
        h_s[...] = jnp.zeros(h_s.shape, F32)
        for sb in range(S5_SUBS):
            _expand_block_diagonal(sb, mc_ref, sc_ref, rc_ref, m_s, s_s, r_s)

    token = lambda ref, b, s: ref.at[b, pl.ds(s, rc, stride=t_chunk), :]
    xs = [jnp.concatenate([token(u_ref, b, s)[...] for b in range(bsz)], axis=0) for s in range(t_chunk)]
    xb = [x.astype(BF16) for x in xs]
    half = ss // LANES
    cols = lambda j: slice(j * LANES, (j + 1) * LANES)
    y_parts = [[None] * S5_SUBS for _ in range(t_chunk)]
    for sb in range(S5_SUBS):
        lanes = slice(sb * S5_SUB, (sb + 1) * S5_SUB)
        x_tiles = [jnp.concatenate([xb[tk * j + k][:, lanes] for k in range(tk)], axis=1) for j in range(S5_LAGS)]
        e = _dot(jnp.concatenate(x_tiles, axis=1), s_s[sb])
        for j in range(2 * half):
            e_s[sb, j] = e[:, cols(j)]
        ar = [jnp.broadcast_to(a_ref[0, sb, 0:1, cols(j)], (bsz, LANES)) for j in range(half)]
        ai = [jnp.broadcast_to(a_ref[0, sb, 1:2, cols(j)], (bsz, LANES)) for j in range(half)]

        def step(c, carry, sb=sb, ar=ar, ai=ai):
            hr, hi = carry
            at_c = pl.ds(c, bsz, stride=rc)
            new_r, new_i = [], []
            for j in range(half):
                hin_s[sb, j, at_c, :] = hr[j]
                hin_s[sb, half + j, at_c, :] = hi[j]
                new_r.append(ar[j] * hr[j] - ai[j] * hi[j] + e_s[sb, j, at_c, :])
                new_i.append(ar[j] * hi[j] + ai[j] * hr[j] + e_s[sb, half + j, at_c, :])
            return tuple(new_r), tuple(new_i)

        carry0 = (tuple(h_s[sb, 0, j] for j in range(half)), tuple(h_s[sb, 1, j] for j in range(half)))
        hr, hi = lax.fori_loop(0, rc, step, carry0)
        for j in range(half):
            h_s[sb, 0, j] = hr[j]
            h_s[sb, 1, j] = hi[j]
            hfin_ref[0, :, pl.ds(sb * 2 * ss + j * LANES, LANES)] = hr[j]
            hfin_ref[0, :, pl.ds(sb * 2 * ss + ss + j * LANES, LANES)] = hi[j]
        hin = jnp.concatenate([hin_s[sb, j] for j in range(2 * half)], axis=1).astype(BF16)
        for i in range(S5_LAGS):
            acc = _dot(hin, r_s[sb, :, i * S5_TILE:(i + 1) * S5_TILE])
            for j in range(i + 1):
                acc += _dot(x_tiles[j], m_s[sb, i - j])
            for k in range(tk):
                y_parts[tk * i + k][sb] = acc[:, k * S5_SUB:(k + 1) * S5_SUB]
    d_row = d_ref[0]
    for t in range(t_chunk):
        y_t = jnp.concatenate(y_parts[t], axis=1) + d_row * xs[t]
        for b in range(bsz):
            token(y_ref, b, t)[...] = y_t[b * rc:(b + 1) * rc]


def _s5_prompt(u, ops, rc):
    bsz, seqlen, d = u.shape
    n_lb = d // LANES
    mc, sc, rc_op, a16, dl = ops
    ss = S5_SUB_STATE
    t_chunk = S5_CHUNK
    tokens = rc * t_chunk
    rows = bsz * rc
    act = pl.BlockSpec((bsz, tokens, LANES), lambda lb, ct: (0, ct, lb))
    per_lb = lambda shape: pl.BlockSpec((1,) + shape, lambda lb, ct: (lb,) + (0,) * len(shape))
    state_tiles = 2 * ss // LANES
    return pl.pallas_call(
        _s5_prompt_body,
        grid=(n_lb, seqlen // tokens),
        in_specs=[act, per_lb(mc.shape[1:]), per_lb(sc.shape[1:]), per_lb(rc_op.shape[1:]),
                  per_lb(a16.shape[1:]), per_lb((1, LANES))],
        out_specs=[act, per_lb((bsz, S5_SUBS * 2 * ss))],
        out_shape=[jax.ShapeDtypeStruct(u.shape, F32),
                   jax.ShapeDtypeStruct((n_lb, bsz, S5_SUBS * 2 * ss), F32)],
        scratch_shapes=[pltpu.VMEM((S5_SUBS, S5_LAGS, S5_TILE, S5_TILE), BF16),
                        pltpu.VMEM((S5_SUBS, t_chunk * S5_SUB, 2 * ss), BF16),
                        pltpu.VMEM((S5_SUBS, 2 * ss, t_chunk * S5_SUB), BF16),
                        pltpu.VMEM((S5_SUBS, state_tiles, rows, LANES), F32),
                        pltpu.VMEM((S5_SUBS, state_tiles, rows, LANES), F32),
                        pltpu.VMEM((S5_SUBS, 2, ss // LANES, bsz, LANES), F32)],
        compiler_params=_cparams(2),
        name="s5_prompt",
    )(u, mc, sc, rc_op, a16, dl)


def _s5_discretize(lam_re, lam_im, log_dt, b_re, b_im):
    dt = jnp.exp(log_dt)[:, None]
    mag = jnp.exp(lam_re * dt)
    ar = mag * jnp.cos(lam_im * dt)
    ai = mag * jnp.sin(lam_im * dt)
    den = lam_re * lam_re + lam_im * lam_im
    nr = ar - 1.0
    coef_re = (nr * lam_re + ai * lam_im) / den
    coef_im = (ai * lam_re - nr * lam_im) / den
    bb_re = coef_re[..., None] * b_re - coef_im[..., None] * b_im
    bb_im = coef_re[..., None] * b_im + coef_im[..., None] * b_re
    return ar, ai, bb_re, bb_im


def _s5_prompt_operators(ar, ai, bb_re, bb_im, c_re, c_im, d_skip):
    t = S5_CHUNK
    g, n = ar.shape
    p = SSM_GROUP
    lg, tk, n_lag = S5_SUB_GROUPS, S5_TILE_TOKENS, S5_LAGS
    n_lb = g * p // LANES
    pr, pi = [jnp.ones_like(ar)], [jnp.zeros_like(ar)]
    for _ in range(t):
        pr, pi = pr + [pr[-1] * ar - pi[-1] * ai], pi + [pr[-1] * ai + pi[-1] * ar]
    pr = jnp.stack(pr)
    pi = jnp.stack(pi)
    cp_re = c_re[None] * pr[:, :, None, :] - c_im[None] * pi[:, :, None, :]
    cp_im = c_re[None] * pi[:, :, None, :] + c_im[None] * pr[:, :, None, :]
    k = (jnp.einsum('tgpn,gnq->gtpq', cp_re[:t], bb_re, precision=HIGHEST)
         - jnp.einsum('tgpn,gnq->gtpq', cp_im[:t], bb_im, precision=HIGHEST))
    k_lag = lambda lag: k[:, lag] if lag >= 0 else jnp.zeros_like(k[:, 0])
    kt = jnp.stack([jnp.stack([jnp.stack([k_lag(tk * dl + tt - ss) for tt in range(tk)], axis=1)
                               for ss in range(tk)], axis=1) for dl in range(n_lag)], axis=1)
    kt = kt.reshape(n_lb, S5_SUBS, lg, n_lag, tk, tk, p, p)
    m = kt.transpose(0, 1, 3, 4, 7, 5, 2, 6).reshape(n_lb, S5_SUBS, n_lag, tk * p, tk * lg * p)
    m = jnp.pad(m, ((0, 0), (0, 0), (0, 0), (0, S5_MC_ROWS - tk * p), (0, 0)))
    pr_rev = jnp.stack([pr[t - 1 - s] for s in range(t)])
    pi_rev = jnp.stack([pi[t - 1 - s] for s in range(t)])
    s_re = pr_rev[:, :, :, None] * bb_re[None] - pi_rev[:, :, :, None] * bb_im[None]
    s_im = pr_rev[:, :, :, None] * bb_im[None] + pi_rev[:, :, :, None] * bb_re[None]
    s_ri = jnp.stack([s_re, s_im]).reshape(2, t, n_lb, S5_SUBS, lg, n, p)
    s = s_ri.transpose(2, 3, 1, 4, 6, 0, 5).reshape(n_lb, S5_SUBS, t * lg * p, 2 * n)
    r_ri = jnp.stack([cp_re[1:], -cp_im[1:]]).reshape(2, t, n_lb, S5_SUBS, lg, p, n)
    r = r_ri.transpose(2, 3, 0, 6, 1, 4, 5).reshape(n_lb, S5_SUBS, 2 * n, t * lg * p)
    a16 = jnp.stack([pr[t].reshape(n_lb, S5_SUBS, lg * n), pi[t].reshape(n_lb, S5_SUBS, lg * n)], axis=2)
    return (m.astype(BF16), s.astype(BF16), r.astype(BF16), a16, d_skip.reshape(n_lb, 1, LANES))


def _s5_step_body(u_ref, h0r_ref, h0i_ref, ar_ref, ai_ref, bbr_ref, bbi_ref, cr_ref, ci_ref,
                  d_ref, y_ref, hr_ref, hi_ref):
    u = u_ref[0]
    ar, ai = ar_ref[0], ai_ref[0]
    h0r, h0i = h0r_ref[0], h0i_ref[0]
    hr = _dot_nt(u, bbr_ref[0], HIGHEST) + ar * h0r - ai * h0i
    hi = _dot_nt(u, bbi_ref[0], HIGHEST) + ar * h0i + ai * h0r
    hr_ref[0] = hr
    hi_ref[0] = hi
    y_ref[0] = (_dot_nt(hr, cr_ref[0], HIGHEST) - _dot_nt(hi, ci_ref[0], HIGHEST)
                + d_ref[0] * u)


def _s5_step(u_g, h0r, h0i, ar, ai, bb_re, bb_im, c_re, c_im, d_skip):
    g, b, p = u_g.shape
    n = ar.shape[-1]
    per_group = lambda shape: pl.BlockSpec((1,) + shape, lambda i: (i, 0, 0))
    return pl.pallas_call(
        _s5_step_body,
        grid=(g,),
        in_specs=[per_group((b, p)), per_group((b, n)), per_group((b, n)),
                  per_group((1, n)), per_group((1, n)),
                  per_group((n, p)), per_group((n, p)),
                  per_group((p, n)), per_group((p, n)), per_group((1, p))],
        out_specs=[per_group((b, p)), per_group((b, n)), per_group((b, n))],
        out_shape=[jax.ShapeDtypeStruct((g, b, p), F32),
                   jax.ShapeDtypeStruct((g, b, n), F32),
                   jax.ShapeDtypeStruct((g, b, n), F32)],
        compiler_params=_cparams(1),
        name="s5_step",
    )(u_g, h0r, h0i, ar.reshape(g, 1, n), ai.reshape(g, 1, n), bb_re, bb_im, c_re, c_im,
      d_skip.reshape(g, 1, p))


def _post_a_body(y_ref, ga_ref, x_ref, wglu_ref, bglu_ref, wout_ref, gpost_ref, x1_ref, *, prec):
    y = _tile(y_ref)
    wdt = wglu_ref.dtype
    gy = 0.5 * y * (1.0 + lax.erf(y * math.sqrt(0.5)))
    z = _dot(gy.astype(wdt), wglu_ref[...], prec) + bglu_ref[...]
    gate = _tile(ga_ref)
    t = (gy * _sigmoid(z)) * (gate * _sigmoid(gate))
    o = _dot(t.astype(wdt), wout_ref[...], prec)
    x1_ref[...] = (_tile(x_ref) + _rms_scale(o) * gpost_ref[...]).reshape(x1_ref.shape)


def _post_a(y, gate, x, w_glu, b_glu, w_out, g_post, prec, grid, y_spec, x_spec):
    d = w_glu.shape[0]
    return pl.pallas_call(
        functools.partial(_post_a_body, prec=prec),
        grid=grid,
        in_specs=[y_spec, x_spec, x_spec, _full((d, d)), _full((1, d)), _full((d, d)), _full((1, d))],
        out_specs=x_spec,
        out_shape=jax.ShapeDtypeStruct(x.shape, F32),
        compiler_params=_cparams(len(grid)),
        name="post_a",
    )(y, gate, x, w_glu, b_glu.reshape(1, d), w_out, g_post.reshape(1, d))


ALIBI_ROWS = 16
FEAT_OFF = HEAD_DIM
V_ROWS = HEAD_DIM + 16


def _aug_width(nb):
    nbp = -(-nb // 16) * 16
    return -(-(FEAT_OFF + ALIBI_ROWS + nbp) // LANES) * LANES, nbp


def _pre_b_body(x1_ref, gkv_ref, wkv_ref, gb_ref, winb_ref, k_ref, v_ref, gate_ref, *rest, prec, nb):
    xn = _rms_scale(x1_ref[...])
    wdt = wkv_ref.dtype
    hk = (xn * gkv_ref[...]).astype(wdt)
    k = _dot(hk, wkv_ref[:, :KV_WIDTH], prec)
    v = _dot(hk, wkv_ref[:, KV_WIDTH:], prec)
    hq = (xn * gb_ref[...]).astype(wdt)
    q = _dot(hq, winb_ref[:, :D_MODEL], prec) * SCALE
    gate_ref[...] = _dot(hq, winb_ref[:, D_MODEL:], prec)
    if nb is None:
        q_ref, = rest
        q_ref[...] = q
        k_ref[...] = k
        v_ref[...] = v
        return
    qt_ref, ka_ref, vta_ref, km_ref, st_ref = rest
    blk = MOBA_BLOCK
    width = ka_ref.shape[-1] - FEAT_OFF
    lane = lax.broadcasted_iota(jnp.int32, (blk, width), 1)
    rowi = lax.broadcasted_iota(jnp.int32, (blk, width), 0)
    ones_row = (lax.broadcasted_iota(jnp.int32, (V_ROWS - HEAD_DIM, blk), 0) == 0).astype(F32).astype(BF16)
    pick = lambda wd: (_iota2((wd, LANES), 0) // HEAD_DIM == _iota2((wd, LANES), 1)).astype(F32).astype(BF16)
    sq_norms = lambda x: jnp.max(_dot((x * x).astype(BF16), pick(x.shape[1])), axis=0, keepdims=True)
    blocks = q.shape[0] // blk
    for j in range(blocks):
        rows = slice(j * blk, (j + 1) * blk)
        qj, kj, vj = q[rows], k[rows], v[rows]
        qt_ref[0, j] = qj.T.astype(BF16)
        km_ref[j] = jnp.mean(kj, axis=0, keepdims=True)
        st_ref[j, 0:1, :] = sq_norms(qj)
        st_ref[j, 1:2, :] = sq_norms(kj)
        n = (pl.program_id(0) * blocks + j) % nb
        feat = jnp.where(lane < 3, n, jnp.where(lane < 6, rowi, jnp.where(lane < 9, 1, 0)))
        feat = jnp.where(lane - ALIBI_ROWS == n, 1, feat).astype(F32).astype(BF16)
        kb = kj.astype(BF16)
        vt = vj.T
        k_ref[0, :, rows] = kj.T
        v_ref[0, :, rows] = vt
        for h in range(KV_HEADS):
            ka_ref[0, h, rows, :FEAT_OFF] = kb[:, h * HEAD_DIM:(h + 1) * HEAD_DIM]
            ka_ref[0, h, rows, FEAT_OFF:] = feat
            vta_ref[0, j, h * V_ROWS:h * V_ROWS + HEAD_DIM, :] = vt[h * HEAD_DIM:(h + 1) * HEAD_DIM].astype(BF16)
            vta_ref[0, j, h * V_ROWS + HEAD_DIM:(h + 1) * V_ROWS, :] = ones_row


def _pre_b(x1, g_kv, w_kv, g_pre_b, w_in_b, tm, prec, prompt_blocks=None):
    m, d = x1.shape
    row = lambda wd: pl.BlockSpec((tm, wd), lambda i: (i, 0))
    nb = None
    if prompt_blocks is None:
        kv_spec = row(KV_WIDTH)
        kv_shape = jax.ShapeDtypeStruct((m, KV_WIDTH), F32)
        out_specs = [kv_spec, kv_spec, row(d), row(d)]
        out_shape = [kv_shape, kv_shape, jax.ShapeDtypeStruct((m, d), F32), jax.ShapeDtypeStruct((m, d), F32)]
    else:
        bsz, nb = prompt_blocks
        blk = MOBA_BLOCK
        bps = tm // blk
        steps = nb // bps
        assert tm == bps * blk and nb == steps * bps and m == bsz * nb * blk
        aug, _ = _aug_width(nb)
        kv_spec = pl.BlockSpec((1, KV_WIDTH, tm), lambda i: (i // steps, 0, i % steps))
        kv_shape = jax.ShapeDtypeStruct((bsz, KV_WIDTH, nb * blk), F32)
        out_specs = [kv_spec, kv_spec, row(d)]
        out_shape = [kv_shape, kv_shape, jax.ShapeDtypeStruct((m, d), F32)]
        out_specs += [pl.BlockSpec((1, bps, d, blk), lambda i: (i // steps, i % steps, 0, 0)),
                      pl.BlockSpec((1, KV_HEADS, tm, aug), lambda i: (i // steps, 0, i % steps, 0)),
                      pl.BlockSpec((1, bps, KV_HEADS * V_ROWS, blk), lambda i: (i // steps, i % steps, 0, 0)),
                      pl.BlockSpec((bps, 1, KV_WIDTH), lambda i: (i, 0, 0)),
                      pl.BlockSpec((bps, 2, LANES), lambda i: (i, 0, 0))]
        out_shape += [jax.ShapeDtypeStruct((bsz, nb, d, blk), BF16),
                      jax.ShapeDtypeStruct((bsz, KV_HEADS, nb * blk, aug), BF16),
                      jax.ShapeDtypeStruct((bsz, nb, KV_HEADS * V_ROWS, blk), BF16),
                      jax.ShapeDtypeStruct((bsz * nb, 1, KV_WIDTH), F32),
                      jax.ShapeDtypeStruct((bsz * nb, 2, LANES), F32)]
    return pl.pallas_call(
        functools.partial(_pre_b_body, prec=prec, nb=nb),
        grid=(m // tm,),
        in_specs=[row(d), _full((1, d)), _full((d, 2 * KV_WIDTH)), _full((1, d)), _full((d, 2 * d))],
        out_specs=out_specs,
        out_shape=out_shape,
        compiler_params=_cparams(1),
        name="pre_b",
    )(x1, g_kv.reshape(1, d), w_kv, g_pre_b.reshape(1, d), w_in_b)


def _top3_mask(gs, idx, valid, axis):
    n = gs.shape[axis]
    gs = jnp.where(valid, gs, NEG_INF)
    sel = jnp.zeros(gs.shape, F32)
    picks = []
    for _ in range(MOBA_TOPK):
        mx = jnp.max(gs, axis=axis, keepdims=True)
        first = jnp.min(jnp.where(gs == mx, idx, n), axis=axis, keepdims=True)
        pick = idx == first
        sel = jnp.where(pick & valid, 1.0, sel)
        gs = jnp.where(pick, REMOVED, gs)
        picks.append(first)
    return sel, picks


def _split3(x):
    hi = x.astype(BF16).astype(F32)
    mid = (x - hi).astype(BF16).astype(F32)
    lo = (x - hi - mid).astype(BF16).astype(F32)
    return [hi, mid, lo]


def _moba_prompt_body(first_ref, qt_ref, g_ref, ka_ref, vta_ref, km_ref, o_ref, qa_s, qd_s):
    h = pl.program_id(1)
    c = pl.program_id(2)
    blk = MOBA_BLOCK
    nb = km_ref.shape[2]
    aug, nbp = _aug_width(nb)
    km = km_ref[0, 0]
    row = lax.broadcasted_iota(jnp.int32, (nb, blk), 0)
    frow = lax.broadcasted_iota(jnp.int32, (ALIBI_ROWS, blk), 0)
    pos_q = (c * blk + lax.broadcasted_iota(jnp.int32, (1, blk), 1)).astype(F32)
    for g in range(Q_PER_KV):
        lanes = slice(g * blk, (g + 1) * blk)
        qg = qt_ref[0, 0, g * HEAD_DIM:(g + 1) * HEAD_DIM, :]
        sel, _ = _top3_mask(_dot(km, qg.astype(F32), HIGHEST), row, row < c, 0)
        bias = jnp.where(sel > 0.0, 0.0, NEG_INF)
        if nbp > nb:
            bias = jnp.concatenate([bias, jnp.zeros((nbp - nb, blk), F32)], axis=0)
        head1 = (h * Q_PER_KV + g + 1).astype(F32)
        slope = jnp.exp2(-8.0 * jnp.full((1, blk), head1, F32) / N_HEADS)
        terms = _split3(slope * float(blk)) + _split3(slope) + _split3(-slope * pos_q)
        feat = jnp.zeros((ALIBI_ROWS, blk), F32)
        for r, term in enumerate(terms):
            feat = jnp.where(frow == r, term, feat)
        for ref, b_rows in ((qa_s, bias.astype(BF16)), (qd_s, jnp.zeros((nbp, blk), BF16))):
            ref[:FEAT_OFF, lanes] = qg
            ref[FEAT_OFF:FEAT_OFF + ALIBI_ROWS, lanes] = feat.astype(BF16)
            ref[FEAT_OFF + ALIBI_ROWS:FEAT_OFF + ALIBI_ROWS + nbp, lanes] = b_rows
            if FEAT_OFF + ALIBI_ROWS + nbp < aug:
                ref[FEAT_OFF + ALIBI_ROWS + nbp:, lanes] = jnp.zeros((aug - FEAT_OFF - ALIBI_ROWS - nbp, blk), BF16)

    def keys(n):
        return ka_ref[0, 0, pl.ds(pl.multiple_of(n * blk, blk), blk), :]

    s = _dot(keys(c), qd_s[...])
    key = lax.broadcasted_iota(jnp.int32, s.shape, 0)
    qry = lax.broadcasted_iota(jnp.int32, s.shape, 1) & (blk - 1)
    s = jnp.where(key <= qry, s, NEG_INF)
    m = jnp.max(s, axis=0, keepdims=True)
    acc = _dot(vta_ref[0, c], jnp.exp(s - m).astype(BF16))

    def block_pair(i, carry):
        m, acc = carry
        n0, n1 = 2 * i, 2 * i + 1
        s0 = _dot(keys(n0), qa_s[...])
        s1 = _dot(keys(n1), qa_s[...])
        m2 = jnp.maximum(m, jnp.maximum(jnp.max(s0, axis=0, keepdims=True), jnp.max(s1, axis=0, keepdims=True)))
        p0 = jnp.exp(s0 - m2).astype(BF16)
        p1 = jnp.exp(s1 - m2).astype(BF16)
        return m2, jnp.exp(m - m2) * acc + _dot(vta_ref[0, n0], p0) + _dot(vta_ref[0, n1], p1)

    first = first_ref[(pl.program_id(0) * KV_HEADS + h) * nb + c]
    m, acc = lax.fori_loop(first, (c + 1) // 2, block_pair, (m, acc))
    out_t = acc[:HEAD_DIM] / acc[HEAD_DIM:HEAD_DIM + 1]
    att = jnp.concatenate([out_t[:, g * blk:(g + 1) * blk] for g in range(Q_PER_KV)], axis=0).T
    gate = g_ref[0]
    o_ref[0] = (att * (gate * _sigmoid(gate))).astype(o_ref.dtype)


_SKIP_LOGIT = -120.0
_NORM_SLACK = 1.02


def _first_needed_pair(stats, bsz, nb):
    q_max = jnp.sqrt(stats[:, 0, :N_HEADS]).reshape(bsz, nb, KV_HEADS, Q_PER_KV) * _NORM_SLACK
    k_max = jnp.sqrt(stats[:, 1, :KV_HEADS]).reshape(bsz, nb, KV_HEADS) * _NORM_SLACK
    slopes = jnp.exp2(-8.0 * jnp.arange(1, N_HEADS + 1, dtype=F32) / N_HEADS).reshape(KV_HEADS, Q_PER_KV)
    c = jnp.arange(nb)[:, None]
    n = jnp.arange(nb)[None, :]
    min_dist = ((c - n) * MOBA_BLOCK - (MOBA_BLOCK - 1)).astype(F32)
    k_pair = k_max[:, None, :, :] + k_max[:, :, None, :]
    bound = (q_max[:, :, None, :, :] * k_pair[..., None]
             - slopes[None, None, None] * min_dist[None, :, :, None, None])
    dead = jnp.all(bound < _SKIP_LOGIT, axis=-1) & (n < c)[None, :, :, None]
    lead = jnp.min(jnp.where(dead, nb, n[None, :, :, None]), axis=2)
    return (lead // 2).transpose(0, 2, 1).reshape(-1).astype(jnp.int32)


def _moba_prompt(qt, gate, ka, vta, km, first_pair):
    bsz, seqlen, d = gate.shape
    nb = seqlen // MOBA_BLOCK
    width = Q_PER_KV * HEAD_DIM
    aug, _ = _aug_width(nb)
    tile = pl.BlockSpec((1, MOBA_BLOCK, width), lambda b, h, c, fp: (b, c, h))
    return pl.pallas_call(
        _moba_prompt_body,
        grid_spec=pltpu.PrefetchScalarGridSpec(
            num_scalar_prefetch=1,
            grid=(bsz, KV_HEADS, nb),
            in_specs=[pl.BlockSpec((1, 1, width, MOBA_BLOCK), lambda b, h, c, fp: (b, c, h, 0)),
                      tile,
                      pl.BlockSpec((1, 1, seqlen, aug), lambda b, h, c, fp: (b, h, 0, 0)),
                      pl.BlockSpec((1, nb, V_ROWS, MOBA_BLOCK), lambda b, h, c, fp: (b, 0, h, 0)),
                      pl.BlockSpec((1, 1, nb, HEAD_DIM), lambda b, h, c, fp: (b, h, 0, 0))],
            out_specs=tile,
            scratch_shapes=[pltpu.VMEM((aug, Q_PER_KV * MOBA_BLOCK), BF16) for _ in range(2)]),
        out_shape=jax.ShapeDtypeStruct((bsz, seqlen, d), BF16),
        compiler_params=_cparams(3),
        name="moba_prompt",
    )(first_pair, qt, gate, ka, vta, km)


_KM_BLOCKS = 16


def _block_mean_body(pt_ref, *refs):
    del pt_ref
    pages, o_ref = refs[:-1], refs[-1]
    s = pl.program_id(1)

    @pl.when(s == 0)
    def _():
        o_ref[...] = jnp.zeros(o_ref.shape, F32)

    lane = lax.broadcasted_iota(jnp.int32, o_ref.shape[1:], 2)
    acc = o_ref[0]
    for j in range(_KM_BLOCKS):
        tot = sum(pages[PAGES_PER_BLOCK * j + r][0] for r in range(PAGES_PER_BLOCK))
        mean = jnp.sum(tot, axis=-1, keepdims=True) * (1.0 / MOBA_BLOCK)
        acc = jnp.where(lane == s * _KM_BLOCKS + j, mean, acc)
    o_ref[0] = acc


def _block_means(cache_t, page_table, n_blocks):
    n_seq = page_table.shape[0]
    assert n_blocks % _KM_BLOCKS == 0
    per_step = _KM_BLOCKS * PAGES_PER_BLOCK
    page_spec = lambda j: pl.BlockSpec(
        (1, KV_HEADS, HEAD_DIM, PAGE_SIZE), lambda b, s, pt, j=j: (pt[b, s * per_step + j], 0, 0, 0))
    return pl.pallas_call(
        _block_mean_body,
        grid_spec=pltpu.PrefetchScalarGridSpec(
            num_scalar_prefetch=1,
            grid=(n_seq, n_blocks // _KM_BLOCKS),
            in_specs=[page_spec(j) for j in range(per_step)],
            out_specs=pl.BlockSpec((1, KV_HEADS, HEAD_DIM, n_blocks), lambda b, s, pt: (b, 0, 0, 0))),
        out_shape=jax.ShapeDtypeStruct((n_seq, KV_HEADS, HEAD_DIM, n_blocks), F32),
        compiler_params=_cparams(2),
        name="block_means",
    )(page_table, *([cache_t] * per_step))


def _sample_topk_body(q_ref, km_ref, pt_ref, sel_ref, *, n_past):
    gs = jnp.concatenate([_dot(q_ref[0, k], km_ref[0, k], HIGHEST) for k in range(KV_HEADS)], axis=0)
    lane = lax.broadcasted_iota(jnp.int32, gs.shape, 1)
    _, picks = _top3_mask(gs, lane, lane < n_past, 1)
    pages = pt_ref[0].astype(F32)
    page_lane = lax.broadcasted_iota(jnp.int32, (N_HEADS, pages.shape[1]), 1)
    out_lane = lax.broadcasted_iota(jnp.int32, sel_ref.shape[1:], 1)
    out = jnp.zeros(sel_ref.shape[1:], jnp.int32)
    for r, first in enumerate(picks):
        out = jnp.where(out_lane == r, first, out)
        for pg in range(PAGES_PER_BLOCK):
            hit = page_lane == first * PAGES_PER_BLOCK + pg
            page = jnp.sum(jnp.where(hit, pages, 0.0), axis=1, keepdims=True).astype(jnp.int32)
            out = jnp.where(out_lane == MOBA_TOPK + PAGES_PER_BLOCK * r + pg, page, out)
    sel_ref[0] = out


def _sample_topk(q, km_t, page_table):
    n_seq, _, _, n_past = km_t.shape
    n_pages = page_table.shape[1]
    return pl.pallas_call(
        functools.partial(_sample_topk_body, n_past=n_past),
        grid=(n_seq,),
        in_specs=[pl.BlockSpec((1,) + q.shape[1:], lambda b: (b, 0, 0, 0)),
                  pl.BlockSpec((1,) + km_t.shape[1:], lambda b: (b, 0, 0, 0)),
                  pl.BlockSpec((1, 1, n_pages), lambda b: (b, 0, 0))],
        out_specs=pl.BlockSpec((1, N_HEADS, LANES), lambda b: (b, 0, 0)),
        out_shape=jax.ShapeDtypeStruct((n_seq, N_HEADS, LANES), jnp.int32),
        compiler_params=_cparams(1),
        name="sample_topk",
    )(q, km_t, page_table.reshape(n_seq, 1, n_pages))


_Q_ROWS = 8


def _moba_sample_body(idx_ref, pages_ref, q_ref, kn_ref, vn_ref, *refs, n_past):
    del pages_ref
    n_sel = MOBA_TOPK * PAGES_PER_BLOCK
    k_pages, v_pages, o_ref = refs[:n_sel], refs[n_sel:2 * n_sel], refs[-1]
    b = pl.program_id(0)
    h = pl.program_id(1)
    keys = MOBA_TOPK * MOBA_BLOCK
    q = jnp.broadcast_to(q_ref[b, h], (_Q_ROWS, HEAD_DIM))
    kt_sel = jnp.concatenate([r[0, 0] for r in k_pages], axis=1)
    vt_sel = jnp.concatenate([r[0, 0] for r in v_pages], axis=1)
    col = lax.broadcasted_iota(jnp.int32, (_Q_ROWS, keys), 1)
    slot = col // MOBA_BLOCK
    base = (b * N_HEADS + h) * MOBA_TOPK
    idx = jnp.zeros((_Q_ROWS, keys), jnp.int32)
    for r in range(MOBA_TOPK):
        idx = jnp.where(slot == r, idx_ref[base + r], idx)
    pos = n_past * MOBA_BLOCK
    dist = (pos - (idx * MOBA_BLOCK + col % MOBA_BLOCK)).astype(F32)
    slope = jnp.exp2(-8.0 * jnp.full((_Q_ROWS, keys), (h + 1).astype(F32), F32) / N_HEADS)
    s = _dot(q.astype(BF16), kt_sel.astype(BF16)) - slope * dist
    s = jnp.where(idx < n_past, s, NEG_INF)
    kvh = h // Q_PER_KV
    s_own = jnp.sum(q * kn_ref[b, kvh], axis=-1, keepdims=True)
    m = jnp.maximum(jnp.max(s, axis=-1, keepdims=True), s_own)
    p = jnp.exp(s - m)
    p_own = jnp.exp(s_own - m)
    l = jnp.sum(p, axis=-1, keepdims=True) + p_own
    out = (_dot_nt(p.astype(BF16), vt_sel.astype(BF16)) + p_own * vn_ref[b, kvh]) / l
    o_ref[b, h] = out[0:1]


def _moba_sample(q, k_new, v_new, cache_kt, cache_vt, idx_flat, pages_flat, n_past):
    n_seq = q.shape[0]
    n_sel = MOBA_TOPK * PAGES_PER_BLOCK

    def page_spec(j):
        return pl.BlockSpec((1, 1, HEAD_DIM, PAGE_SIZE),
                            lambda b, h, idx, pages: (pages[(b * N_HEADS + h) * n_sel + j], h // Q_PER_KV, 0, 0))

    whole = lambda a: pl.BlockSpec(a.shape, lambda b, h, idx, pages: (0, 0, 0, 0))
    return pl.pallas_call(
        functools.partial(_moba_sample_body, n_past=n_past),
        grid_spec=pltpu.PrefetchScalarGridSpec(
            num_scalar_prefetch=2,
            grid=(n_seq, N_HEADS),
            in_specs=[whole(q), whole(k_new), whole(v_new)] + [page_spec(j) for j in range(n_sel)] * 2,
            out_specs=whole(q)),
        out_shape=jax.ShapeDtypeStruct((n_seq, N_HEADS, 1, HEAD_DIM), F32),
        compiler_params=_cparams(2),
        name="moba_sample",
    )(idx_flat, pages_flat, q, k_new, v_new, *([cache_kt] * n_sel), *([cache_vt] * n_sel))


def _out_b_body(a_ref, *refs, prec, gated):
    if gated:
        x1_ref, w_ref, g_ref, y_ref = refs
        a = a_ref[...]
    else:
        gate_ref, x1_ref, w_ref, g_ref, y_ref = refs
        gate = gate_ref[...]
        a = (a_ref[...] * (gate * _sigmoid(gate))).astype(w_ref.dtype)
    o = _dot(a, w_ref[...], prec)
    y_ref[...] = x1_ref[...] + _rms_scale(o) * g_ref[...]


def _out_b(att, gate, x1, w, g_post, tm, prec):
    m, d = x1.shape
    tile = pl.BlockSpec((tm, d), lambda i: (i, 0))
    gated = gate is None
    acts = [att, x1] if gated else [att, gate, x1]
    return pl.pallas_call(
        functools.partial(_out_b_body, prec=prec, gated=gated),
        grid=(m // tm,),
        in_specs=[tile] * len(acts) + [_full((d, d)), _full((1, d))],
        out_specs=tile,
        out_shape=jax.ShapeDtypeStruct((m, d), F32),
        compiler_params=_cparams(1),
        name="out_b",
    )(*acts, w, g_post.reshape(1, d))


TM_IN_A = 1024
TM_POST_A = 512
TM_OUT_B = 1024
TM_PRE_B = 2 * MOBA_BLOCK
S5_CHUNKS_PER_STEP = 64


def _prompt_trunk(x, p):
    bsz, seqlen, d = x.shape
    m = bsz * seqlen
    nb = seqlen // MOBA_BLOCK
    nc = seqlen // S5_CHUNK
    x2 = x.reshape(m, d)
    tm = TM_IN_A
    tile = pl.BlockSpec((tm, d), lambda i: (i, 0))
    u, gate_a = _norm_matmul(x2, p['g_pre_a'], p['w_in_a'].astype(BF16), None, (m // tm,), tile, [tile, tile],
                             [jax.ShapeDtypeStruct((m, d), F32)] * 2)
    ar, ai, bb_re, bb_im = _s5_discretize(p['lambda_re'], p['lambda_im'], p['log_dt'], p['b_re'], p['b_im'])
    ops = _s5_prompt_operators(ar, ai, bb_re, bb_im, p['c_re'], p['c_im'], p['d_skip'])
    y, h_fin = _s5_prompt(u.reshape(bsz, seqlen, d), ops, min(nc, S5_CHUNKS_PER_STEP))
    h_fin = h_fin.reshape(d // LANES, bsz, S5_SUBS, 2, S5_SUB_GROUPS, SSM_STATE).transpose(3, 1, 0, 2, 4, 5)
    h_fin = h_fin.reshape(2, 1, bsz, SSM_GROUPS, SSM_STATE)
    tm = TM_POST_A
    tile = pl.BlockSpec((tm, d), lambda i: (i, 0))
    x1 = _post_a(y.reshape(m, d), gate_a, x2, p['w_glu'].astype(BF16), p['b_glu'], p['w_out_a'].astype(BF16),
                 p['g_post_a'], None, (m // tm,), tile, tile)
    k, v, gate_b, qt, ka, vta, km, stats = _pre_b(x1, p['g_kv'], p['w_kv'].astype(BF16), p['g_pre_b'],
                                                  p['w_in_b'].astype(BF16), TM_PRE_B, None, (bsz, nb))
    km = km.reshape(bsz, nb, KV_HEADS, HEAD_DIM).transpose(0, 2, 1, 3)
    att = _moba_prompt(qt, gate_b.reshape(bsz, seqlen, d), ka, vta, km, _first_needed_pair(stats, bsz, nb))
    y_out = _out_b(att.reshape(m, d), None, x1, p['w_out_b'].astype(BF16), p['g_post_b'], TM_OUT_B, None)
    to_out = lambda t: t.reshape(bsz, KV_HEADS, HEAD_DIM, seqlen).transpose(0, 3, 1, 2)
    return (y_out.reshape(bsz, seqlen, d), h_fin[0], h_fin[1], to_out(k), to_out(v))


def _sample_trunk(x, h0_re, h0_im, cache_k, cache_v, page_table, p):
    n_seq, seqlen, d = x.shape
    assert seqlen == 1
    g, pch = SSM_GROUPS, SSM_GROUP
    n_past = page_table.shape[1] // PAGES_PER_BLOCK
    x2 = x.reshape(n_seq, d)
    whole = pl.BlockSpec((n_seq, d), lambda i: (0, 0))
    u, gate_a = _norm_matmul(x2, p['g_pre_a'], p['w_in_a'], HIGHEST, (1,), whole, [whole, whole],
                             [jax.ShapeDtypeStruct((n_seq, d), F32)] * 2)
    ar, ai, bb_re, bb_im = _s5_discretize(p['lambda_re'], p['lambda_im'], p['log_dt'], p['b_re'], p['b_im'])
    y_g, hr, hi = _s5_step(u.reshape(n_seq, g, pch).transpose(1, 0, 2),
                           h0_re.transpose(1, 0, 2), h0_im.transpose(1, 0, 2),
                           ar, ai, bb_re, bb_im, p['c_re'], p['c_im'], p['d_skip'])
    y = y_g.transpose(1, 0, 2).reshape(n_seq, d)
    x1 = _post_a(y, gate_a, x2, p['w_glu'], p['b_glu'], p['w_out_a'], p['g_post_a'], HIGHEST, (1,), whole, whole)
    k, v, gate_b, q = _pre_b(x1, p['g_kv'], p['w_kv'], p['g_pre_b'], p['w_in_b'], n_seq, HIGHEST)
    cache_kt = cache_k.transpose(0, 2, 3, 1)
    cache_vt = cache_v.transpose(0, 2, 3, 1)
    km_t = _block_means(cache_kt, page_table, n_past)
    sel = _sample_topk(q.reshape(n_seq, KV_HEADS, Q_PER_KV, HEAD_DIM), km_t, page_table)
    n_sel = MOBA_TOPK * PAGES_PER_BLOCK
    att = _moba_sample(q.reshape(n_seq, N_HEADS, 1, HEAD_DIM), k.reshape(n_seq, KV_HEADS, 1, HEAD_DIM),
                       v.reshape(n_seq, KV_HEADS, 1, HEAD_DIM), cache_kt, cache_vt,
                       sel[:, :, :MOBA_TOPK].reshape(-1), sel[:, :, MOBA_TOPK:MOBA_TOPK + n_sel].reshape(-1), n_past)
    y_out = _out_b(att.reshape(n_seq, d), gate_b, x1, p['w_out_b'], p['g_post_b'], n_seq, HIGHEST)
    return (y_out.reshape(n_seq, 1, d), hr.transpose(1, 0, 2)[None], hi.transpose(1, 0, 2)[None],
            k.reshape(n_seq, 1, KV_HEADS, HEAD_DIM), v.reshape(n_seq, 1, KV_HEADS, HEAD_DIM))


def kernel(x_prompt, x_sample, state_ssm_re, state_ssm_im, cache_k, cache_v, page_table, g_pre_a, w_in_a, lambda_re, lambda_im, log_dt, b_re, b_im, c_re, c_im, d_skip, w_glu, b_glu, w_out_a, g_post_a, g_kv, w_kv, g_pre_b, w_in_b, w_out_b, g_post_b):
    p = {'g_pre_a': g_pre_a[0], 'w_in_a': w_in_a[0], 'lambda_re': lambda_re[0], 'lambda_im': lambda_im[0],
         'log_dt': log_dt[0], 'b_re': b_re[0], 'b_im': b_im[0], 'c_re': c_re[0], 'c_im': c_im[0],
         'd_skip': d_skip[0], 'w_glu': w_glu[0], 'b_glu': b_glu[0], 'w_out_a': w_out_a[0],
         'g_post_a': g_post_a[0], 'g_kv': g_kv, 'w_kv': w_kv, 'g_pre_b': g_pre_b[0],
         'w_in_b': w_in_b[0], 'w_out_b': w_out_b[0], 'g_post_b': g_post_b[0]}
    y_p, re_p, im_p, k_p, v_p = _prompt_trunk(x_prompt, p)
    y_s, re_s, im_s, k_s, v_s = _sample_trunk(x_sample, state_ssm_re[0], state_ssm_im[0],
                                              cache_k, cache_v, page_table, p)
    return (y_p, y_s, re_p, im_p, k_p, v_p, re_s, im_s, k_s, v_s)
```

```python
import functools
import math

import jax
import jax.numpy as jnp
from jax import lax
from jax.experimental import pallas as pl
from jax.experimental.pallas import tpu as pltpu

F32 = jnp.float32
BF16 = jnp.bfloat16
HIGHEST = lax.Precision.HIGHEST

D_MODEL = 1024
SSM_GROUP = 16
SSM_GROUPS = D_MODEL // SSM_GROUP
SSM_STATE = 64
HEAD_DIM = 64
N_HEADS = D_MODEL // HEAD_DIM
KV_HEADS = 4
Q_PER_KV = N_HEADS // KV_HEADS
KV_WIDTH = KV_HEADS * HEAD_DIM
MOBA_BLOCK = 256
MOBA_TOPK = 3
PAGE_SIZE = 128
PAGES_PER_BLOCK = MOBA_BLOCK // PAGE_SIZE
EPS = 1e-6
NEG_INF = -1e30
REMOVED = -3e38
SCALE = HEAD_DIM ** -0.5
S5_CHUNK = 16
LANES = 128
VMEM_LIMIT = 56 * 1024 * 1024


def _cparams(n_grid):
    return pltpu.CompilerParams(dimension_semantics=("arbitrary",) * n_grid,
                                vmem_limit_bytes=VMEM_LIMIT)


def _dot(a, b, prec=None):
    return jnp.dot(a, b, preferred_element_type=F32, precision=prec)


def _dot_nt(a, b, prec=None):
    return lax.dot_general(a, b, (((1,), (1,)), ((), ())),
                           preferred_element_type=F32, precision=prec)


def _rms_scale(x):
    return x * lax.rsqrt(jnp.mean(x * x, axis=-1, keepdims=True) + EPS)


def _sigmoid(x):
    return 1.0 / (1.0 + jnp.exp(-x))


def _full(shape):
    zeros = (0,) * len(shape)
    return pl.BlockSpec(shape, lambda *_: zeros)


def _tile(ref):
    x = ref[...]
    return x.reshape(x.shape[-2:])


def _norm_matmul_body(x_ref, g_ref, w_ref, *out_refs, prec):
    h = (_rms_scale(_tile(x_ref)) * g_ref[...]).astype(w_ref.dtype)
    off = 0
    for o_ref in out_refs:
        n = o_ref.shape[-1]
        o_ref[...] = _dot(h, w_ref[:, off:off + n], prec).astype(o_ref.dtype).reshape(o_ref.shape)
        off += n


def _norm_matmul(x, g, w, prec, grid, x_spec, out_specs, out_shape):
    d, n = w.shape
    return pl.pallas_call(
        functools.partial(_norm_matmul_body, prec=prec),
        grid=grid,
        in_specs=[x_spec, _full((1, d)), _full((d, n))],
        out_specs=out_specs,
        out_shape=out_shape,
        compiler_params=_cparams(len(grid)),
        name="norm_matmul",
    )(x, g.reshape(1, d), w)


S5_SUB = 64
S5_SUBS = LANES // S5_SUB
S5_SUB_GROUPS = S5_SUB // SSM_GROUP
S5_SUB_STATE = S5_SUB_GROUPS * SSM_STATE
S5_TILE = 2 * LANES
S5_TILE_TOKENS = S5_TILE // S5_SUB
S5_LAGS = S5_CHUNK // S5_TILE_TOKENS
S5_MC_ROWS = LANES


def _iota2(shape, dim):
    return lax.broadcasted_iota(jnp.int32, shape, dim)


def _expand_block_diagonal(sb, mc_ref, sc_ref, rc_ref, m_s, s_s, r_s):
    p, n, lg = SSM_GROUP, SSM_STATE, S5_SUB_GROUPS
    ss = S5_SUB_STATE
    grp = lambda idx, per: (idx // per) % lg
    _, rows, cols = s_s.shape
    i, j = _iota2((2 * n, cols), 0), _iota2((2 * n, cols), 1)
    spread = (i == (j // ss) * n + j % n).astype(F32).astype(BF16)
    blk = 256
    for r0 in range(0, rows, blk):
        full = _dot(sc_ref[0, sb, r0:r0 + blk, :], spread)
        keep = grp(_iota2((blk, cols), 0) + r0, p) == grp(_iota2((blk, cols), 1), n)
        s_s[sb, r0:r0 + blk, :] = jnp.where(keep, full, 0.0).astype(BF16)
    _, rows, cols = r_s.shape
    i, j = _iota2((rows, 2 * n), 0), _iota2((rows, 2 * n), 1)
    spread = (j == (i // ss) * n + i % n).astype(F32).astype(BF16)
    for c0 in range(0, cols, blk):
        full = _dot(spread, rc_ref[0, sb, :, c0:c0 + blk])
        keep = grp(_iota2((rows, blk), 0), n) == grp(_iota2((rows, blk), 1) + c0, p)
        r_s[sb, :, c0:c0 + blk] = jnp.where(keep, full, 0.0).astype(BF16)
    _, n_lag, rows, cols = m_s.shape
    i, j = _iota2((rows, S5_MC_ROWS), 0), _iota2((rows, S5_MC_ROWS), 1)
    spread = (j == (i // S5_SUB) * p + i % p).astype(F32).astype(BF16)
    keep = grp(_iota2((rows, cols), 0), p) == grp(_iota2((rows, cols), 1), p)
    for dl in range(n_lag):
        m_s[sb, dl] = jnp.where(keep, _dot(spread, mc_ref[0, sb, dl]), 0.0).astype(BF16)


def _s5_prompt_body(u_ref, mc_ref, sc_ref, rc_ref, a_ref, d_ref, y_ref, hfin_ref,
                    m_s, s_s, r_s, e_s, hin_s, h_s):
    bsz, tokens, _ = u_ref.shape
    t_chunk = S5_CHUNK
    rc = tokens // t_chunk
    ss = S5_SUB_STATE
    tk = S5_TILE_TOKENS

    @pl.when(pl.program_id(1) == 0)
    def _():
        h_s[...] = jnp.zeros(h_s.shape, F32)
        for sb in range(S5_SUBS):
            _expand_block_diagonal(sb, mc_ref, sc_ref, rc_ref, m_s, s_s, r_s)

    token = lambda ref, b, s: ref.at[b, pl.ds(s, rc, stride=t_chunk), :]
    xs = [jnp.concatenate([token(u_ref, b, s)[...] for b in range(bsz)], axis=0) for s in range(t_chunk)]
    xb = [x.astype(BF16) for x in xs]
    half = ss // LANES
    cols = lambda j: slice(j * LANES, (j + 1) * LANES)
    y_parts = [[None] * S5_SUBS for _ in range(t_chunk)]
    for sb in range(S5_SUBS):
        lanes = slice(sb * S5_SUB, (sb + 1) * S5_SUB)
        x_tiles = [jnp.concatenate([xb[tk * j + k][:, lanes] for k in range(tk)], axis=1) for j in range(S5_LAGS)]
        e = _dot(jnp.concatenate(x_tiles, axis=1), s_s[sb])
        for j in range(2 * half):
            e_s[sb, j] = e[:, cols(j)]
        ar = [jnp.broadcast_to(a_ref[0, sb, 0:1, cols(j)], (bsz, LANES)) for j in range(half)]
        ai = [jnp.broadcast_to(a_ref[0, sb, 1:2, cols(j)], (bsz, LANES)) for j in range(half)]

        def step(c, carry, sb=sb, ar=ar, ai=ai):
            hr, hi = carry
            at_c = pl.ds(c, bsz, stride=rc)
            new_r, new_i = [], []
            for j in range(half):
                hin_s[sb, j, at_c, :] = hr[j]
                hin_s[sb, half + j, at_c, :] = hi[j]
                new_r.append(ar[j] * hr[j] - ai[j] * hi[j] + e_s[sb, j, at_c, :])
                new_i.append(ar[j] * hi[j] + ai[j] * hr[j] + e_s[sb, half + j, at_c, :])
            return tuple(new_r), tuple(new_i)

        carry0 = (tuple(h_s[sb, 0, j] for j in range(half)), tuple(h_s[sb, 1, j] for j in range(half)))
        hr, hi = lax.fori_loop(0, rc, step, carry0)
        for j in range(half):
            h_s[sb, 0, j] = hr[j]
            h_s[sb, 1, j] = hi[j]
            hfin_ref[0, :, pl.ds(sb * 2 * ss + j * LANES, LANES)] = hr[j]
            hfin_ref[0, :, pl.ds(sb * 2 * ss + ss + j * LANES, LANES)] = hi[j]
        hin = jnp.concatenate([hin_s[sb, j] for j in range(2 * half)], axis=1).astype(BF16)
        for i in range(S5_LAGS):
            acc = _dot(hin, r_s[sb, :, i * S5_TILE:(i + 1) * S5_TILE])
            for j in range(i + 1):
                acc += _dot(x_tiles[j], m_s[sb, i - j])
            for k in range(tk):
                y_parts[tk * i + k][sb] = acc[:, k * S5_SUB:(k + 1) * S5_SUB]
    d_row = d_ref[0]
    for t in range(t_chunk):
        y_t = jnp.concatenate(y_parts[t], axis=1) + d_row * xs[t]
        for b in range(bsz):
            token(y_ref, b, t)[...] = y_t[b * rc:(b + 1) * rc]


def _s5_prompt(u, ops, rc):
    bsz, seqlen, d = u.shape
    n_lb = d // LANES
    mc, sc, rc_op, a16, dl = ops
    ss = S5_SUB_STATE
    t_chunk = S5_CHUNK
    tokens = rc * t_chunk
    rows = bsz * rc
    act = pl.BlockSpec((bsz, tokens, LANES), lambda lb, ct: (0, ct, lb))
    per_lb = lambda shape: pl.BlockSpec((1,) + shape, lambda lb, ct: (lb,) + (0,) * len(shape))
    state_tiles = 2 * ss // LANES
    return pl.pallas_call(
        _s5_prompt_body,
        grid=(n_lb, seqlen // tokens),
        in_specs=[act, per_lb(mc.shape[1:]), per_lb(sc.shape[1:]), per_lb(rc_op.shape[1:]),
                  per_lb(a16.shape[1:]), per_lb((1, LANES))],
        out_specs=[act, per_lb((bsz, S5_SUBS * 2 * ss))],
        out_shape=[jax.ShapeDtypeStruct(u.shape, F32),
                   jax.ShapeDtypeStruct((n_lb, bsz, S5_SUBS * 2 * ss), F32)],
        scratch_shapes=[pltpu.VMEM((S5_SUBS, S5_LAGS, S5_TILE, S5_TILE), BF16),
                        pltpu.VMEM((S5_SUBS, t_chunk * S5_SUB, 2 * ss), BF16),
                        pltpu.VMEM((S5_SUBS, 2 * ss, t_chunk * S5_SUB), BF16),
                        pltpu.VMEM((S5_SUBS, state_tiles, rows, LANES), F32),
                        pltpu.VMEM((S5_SUBS, state_tiles, rows, LANES), F32),
                        pltpu.VMEM((S5_SUBS, 2, ss // LANES, bsz, LANES), F32)],
        compiler_params=_cparams(2),
        name="s5_prompt",
    )(u, mc, sc, rc_op, a16, dl)


def _s5_discretize(lam_re, lam_im, log_dt, b_re, b_im):
    dt = jnp.exp(log_dt)[:, None]
    mag = jnp.exp(lam_re * dt)
    ar = mag * jnp.cos(lam_im * dt)
    ai = mag * jnp.sin(lam_im * dt)
    den = lam_re * lam_re + lam_im * lam_im
    nr = ar - 1.0
    coef_re = (nr * lam_re + ai * lam_im) / den
    coef_im = (ai * lam_re - nr * lam_im) / den
    bb_re = coef_re[..., None] * b_re - coef_im[..., None] * b_im
    bb_im = coef_re[..., None] * b_im + coef_im[..., None] * b_re
    return ar, ai, bb_re, bb_im


def _s5_prompt_operators(ar, ai, bb_re, bb_im, c_re, c_im, d_skip):
    t = S5_CHUNK
    g, n = ar.shape
    p = SSM_GROUP
    lg, tk, n_lag = S5_SUB_GROUPS, S5_TILE_TOKENS, S5_LAGS
    n_lb = g * p // LANES
    pr, pi = [jnp.ones_like(ar)], [jnp.zeros_like(ar)]
    for _ in range(t):
        pr, pi = pr + [pr[-1] * ar - pi[-1] * ai], pi + [pr[-1] * ai + pi[-1] * ar]
    pr = jnp.stack(pr)
    pi = jnp.stack(pi)
    cp_re = c_re[None] * pr[:, :, None, :] - c_im[None] * pi[:, :, None, :]
    cp_im = c_re[None] * pi[:, :, None, :] + c_im[None] * pr[:, :, None, :]
    k = (jnp.einsum('tgpn,gnq->gtpq', cp_re[:t], bb_re, precision=HIGHEST)
         - jnp.einsum('tgpn,gnq->gtpq', cp_im[:t], bb_im, precision=HIGHEST))
    k_lag = lambda lag: k[:, lag] if lag >= 0 else jnp.zeros_like(k[:, 0])
    kt = jnp.stack([jnp.stack([jnp.stack([k_lag(tk * dl + tt - ss) for tt in range(tk)], axis=1)
                               for ss in range(tk)], axis=1) for dl in range(n_lag)], axis=1)
    kt = kt.reshape(n_lb, S5_SUBS, lg, n_lag, tk, tk, p, p)
    m = kt.transpose(0, 1, 3, 4, 7, 5, 2, 6).reshape(n_lb, S5_SUBS, n_lag, tk * p, tk * lg * p)
    m = jnp.pad(m, ((0, 0), (0, 0), (0, 0), (0, S5_MC_ROWS - tk * p), (0, 0)))
    pr_rev = jnp.stack([pr[t - 1 - s] for s in range(t)])
    pi_rev = jnp.stack([pi[t - 1 - s] for s in range(t)])
    s_re = pr_rev[:, :, :, None] * bb_re[None] - pi_rev[:, :, :, None] * bb_im[None]
    s_im = pr_rev[:, :, :, None] * bb_im[None] + pi_rev[:, :, :, None] * bb_re[None]
    s_ri = jnp.stack([s_re, s_im]).reshape(2, t, n_lb, S5_SUBS, lg, n, p)
    s = s_ri.transpose(2, 3, 1, 4, 6, 0, 5).reshape(n_lb, S5_SUBS, t * lg * p, 2 * n)
    r_ri = jnp.stack([cp_re[1:], -cp_im[1:]]).reshape(2, t, n_lb, S5_SUBS, lg, p, n)
    r = r_ri.transpose(2, 3, 0, 6, 1, 4, 5).reshape(n_lb, S5_SUBS, 2 * n, t * lg * p)
    a16 = jnp.stack([pr[t].reshape(n_lb, S5_SUBS, lg * n), pi[t].reshape(n_lb, S5_SUBS, lg * n)], axis=2)
    return (m.astype(BF16), s.astype(BF16), r.astype(BF16), a16, d_skip.reshape(n_lb, 1, LANES))


def _s5_step_body(u_ref, h0r_ref, h0i_ref, ar_ref, ai_ref, bbr_ref, bbi_ref, cr_ref, ci_ref,
                  d_ref, y_ref, hr_ref, hi_ref):
    u = u_ref[0]
    ar, ai = ar_ref[0], ai_ref[0]
    h0r, h0i = h0r_ref[0], h0i_ref[0]
    hr = _dot_nt(u, bbr_ref[0], HIGHEST) + ar * h0r - ai * h0i
    hi = _dot_nt(u, bbi_ref[0], HIGHEST) + ar * h0i + ai * h0r
    hr_ref[0] = hr
    hi_ref[0] = hi
    y_ref[0] = (_dot_nt(hr, cr_ref[0], HIGHEST) - _dot_nt(hi, ci_ref[0], HIGHEST)
                + d_ref[0] * u)


def _s5_step(u_g, h0r, h0i, ar, ai, bb_re, bb_im, c_re, c_im, d_skip):
    g, b, p = u_g.shape
    n = ar.shape[-1]
    per_group = lambda shape: pl.BlockSpec((1,) + shape, lambda i: (i, 0, 0))
    return pl.pallas_call(
        _s5_step_body,
        grid=(g,),
        in_specs=[per_group((b, p)), per_group((b, n)), per_group((b, n)),
                  per_group((1, n)), per_group((1, n)),
                  per_group((n, p)), per_group((n, p)),
                  per_group((p, n)), per_group((p, n)), per_group((1, p))],
        out_specs=[per_group((b, p)), per_group((b, n)), per_group((b, n))],
        out_shape=[jax.ShapeDtypeStruct((g, b, p), F32),
                   jax.ShapeDtypeStruct((g, b, n), F32),
                   jax.ShapeDtypeStruct((g, b, n), F32)],
        compiler_params=_cparams(1),
        name="s5_step",
    )(u_g, h0r, h0i, ar.reshape(g, 1, n), ai.reshape(g, 1, n), bb_re, bb_im, c_re, c_im,
      d_skip.reshape(g, 1, p))


def _post_a_body(y_ref, ga_ref, x_ref, wglu_ref, bglu_ref, wout_ref, gpost_ref, x1_ref, *, prec):
    y = _tile(y_ref)
    wdt = wglu_ref.dtype
    gy = 0.5 * y * (1.0 + lax.erf(y * math.sqrt(0.5)))
    z = _dot(gy.astype(wdt), wglu_ref[...], prec) + bglu_ref[...]
    gate = _tile(ga_ref)
    t = (gy * _sigmoid(z)) * (gate * _sigmoid(gate))
    o = _dot(t.astype(wdt), wout_ref[...], prec)
    x1_ref[...] = (_tile(x_ref) + _rms_scale(o) * gpost_ref[...]).reshape(x1_ref.shape)


def _post_a(y, gate, x, w_glu, b_glu, w_out, g_post, prec, grid, y_spec, x_spec):
    d = w_glu.shape[0]
    return pl.pallas_call(
        functools.partial(_post_a_body, prec=prec),
        grid=grid,
        in_specs=[y_spec, x_spec, x_spec, _full((d, d)), _full((1, d)), _full((d, d)), _full((1, d))],
        out_specs=x_spec,
        out_shape=jax.ShapeDtypeStruct(x.shape, F32),
        compiler_params=_cparams(len(grid)),
        name="post_a",
    )(y, gate, x, w_glu, b_glu.reshape(1, d), w_out, g_post.reshape(1, d))


ALIBI_ROWS = 16
FEAT_OFF = HEAD_DIM
V_ROWS = HEAD_DIM + 16


def _aug_width(nb):
    nbp = -(-nb // 16) * 16
    return -(-(FEAT_OFF + ALIBI_ROWS + nbp) // LANES) * LANES, nbp


def _pre_b_body(x1_ref, gkv_ref, wkv_ref, gb_ref, winb_ref, k_ref, v_ref, gate_ref, *rest, prec, nb):
    xn = _rms_scale(x1_ref[...])
    wdt = wkv_ref.dtype
    hk = (xn * gkv_ref[...]).astype(wdt)
    k = _dot(hk, wkv_ref[:, :KV_WIDTH], prec)
    v = _dot(hk, wkv_ref[:, KV_WIDTH:], prec)
    hq = (xn * gb_ref[...]).astype(wdt)
    q = _dot(hq, winb_ref[:, :D_MODEL], prec) * SCALE
    gate_ref[...] = _dot(hq, winb_ref[:, D_MODEL:], prec)
    if nb is None:
        q_ref, = rest
        q_ref[...] = q
        k_ref[...] = k
        v_ref[...] = v
        return
    qt_ref, ka_ref, vta_ref, km_ref, st_ref = rest
    blk = MOBA_BLOCK
    width = ka_ref.shape[-1] - FEAT_OFF
    lane = lax.broadcasted_iota(jnp.int32, (blk, width), 1)
    rowi = lax.broadcasted_iota(jnp.int32, (blk, width), 0)
    ones_row = (lax.broadcasted_iota(jnp.int32, (V_ROWS - HEAD_DIM, blk), 0) == 0).astype(F32).astype(BF16)
    pick = lambda wd: (_iota2((wd, LANES), 0) // HEAD_DIM == _iota2((wd, LANES), 1)).astype(F32).astype(BF16)
    sq_norms = lambda x: jnp.max(_dot((x * x).astype(BF16), pick(x.shape[1])), axis=0, keepdims=True)
    blocks = q.shape[0] // blk
    for j in range(blocks):
        rows = slice(j * blk, (j + 1) * blk)
        qj, kj, vj = q[rows], k[rows], v[rows]
        qt_ref[0, j] = qj.T.astype(BF16)
        km_ref[j] = jnp.mean(kj, axis=0, keepdims=True)
        st_ref[j, 0:1, :] = sq_norms(qj)
        st_ref[j, 1:2, :] = sq_norms(kj)
        n = (pl.program_id(0) * blocks + j) % nb
        feat = jnp.where(lane < 3, n, jnp.where(lane < 6, rowi, jnp.where(lane < 9, 1, 0)))
        feat = jnp.where(lane - ALIBI_ROWS == n, 1, feat).astype(F32).astype(BF16)
        kb = kj.astype(BF16)
        vt = vj.T
        k_ref[0, :, rows] = kj.T
        v_ref[0, :, rows] = vt
        for h in range(KV_HEADS):
            ka_ref[0, h, rows, :FEAT_OFF] = kb[:, h * HEAD_DIM:(h + 1) * HEAD_DIM]
            ka_ref[0, h, rows, FEAT_OFF:] = feat
            vta_ref[0, j, h * V_ROWS:h * V_ROWS + HEAD_DIM, :] = vt[h * HEAD_DIM:(h + 1) * HEAD_DIM].astype(BF16)
            vta_ref[0, j, h * V_ROWS + HEAD_DIM:(h + 1) * V_ROWS, :] = ones_row


def _pre_b(x1, g_kv, w_kv, g_pre_b, w_in_b, tm, prec, prompt_blocks=None):
    m, d = x1.shape
    row = lambda wd: pl.BlockSpec((tm, wd), lambda i: (i, 0))
    nb = None
    if prompt_blocks is None:
        kv_spec = row(KV_WIDTH)
        kv_shape = jax.ShapeDtypeStruct((m, KV_WIDTH), F32)
        out_specs = [kv_spec, kv_spec, row(d), row(d)]
        out_shape = [kv_shape, kv_shape, jax.ShapeDtypeStruct((m, d), F32), jax.ShapeDtypeStruct((m, d), F32)]
    else:
        bsz, nb = prompt_blocks
        blk = MOBA_BLOCK
        bps = tm // blk
        steps = nb // bps
        assert tm == bps * blk and nb == steps * bps and m == bsz * nb * blk
        aug, _ = _aug_width(nb)
        kv_spec = pl.BlockSpec((1, KV_WIDTH, tm), lambda i: (i // steps, 0, i % steps))
        kv_shape = jax.ShapeDtypeStruct((bsz, KV_WIDTH, nb * blk), F32)
        out_specs = [kv_spec, kv_spec, row(d)]
        out_shape = [kv_shape, kv_shape, jax.ShapeDtypeStruct((m, d), F32)]
        out_specs += [pl.BlockSpec((1, bps, d, blk), lambda i: (i // steps, i % steps, 0, 0)),
                      pl.BlockSpec((1, KV_HEADS, tm, aug), lambda i: (i // steps, 0, i % steps, 0)),
                      pl.BlockSpec((1, bps, KV_HEADS * V_ROWS, blk), lambda i: (i // steps, i % steps, 0, 0)),
                      pl.BlockSpec((bps, 1, KV_WIDTH), lambda i: (i, 0, 0)),
                      pl.BlockSpec((bps, 2, LANES), lambda i: (i, 0, 0))]
        out_shape += [jax.ShapeDtypeStruct((bsz, nb, d, blk), BF16),
                      jax.ShapeDtypeStruct((bsz, KV_HEADS, nb * blk, aug), BF16),
                      jax.ShapeDtypeStruct((bsz, nb, KV_HEADS * V_ROWS, blk), BF16),
                      jax.ShapeDtypeStruct((bsz * nb, 1, KV_WIDTH), F32),
                      jax.ShapeDtypeStruct((bsz * nb, 2, LANES), F32)]
    return pl.pallas_call(
        functools.partial(_pre_b_body, prec=prec, nb=nb),
        grid=(m // tm,),
        in_specs=[row(d), _full((1, d)), _full((d, 2 * KV_WIDTH)), _full((1, d)), _full((d, 2 * d))],
        out_specs=out_specs,
        out_shape=out_shape,
        compiler_params=_cparams(1),
        name="pre_b",
    )(x1, g_kv.reshape(1, d), w_kv, g_pre_b.reshape(1, d), w_in_b)


def _top3_mask(gs, idx, valid, axis):
    n = gs.shape[axis]
    gs = jnp.where(valid, gs, NEG_INF)
    sel = jnp.zeros(gs.shape, F32)
    picks = []
    for _ in range(MOBA_TOPK):
        mx = jnp.max(gs, axis=axis, keepdims=True)
        first = jnp.min(jnp.where(gs == mx, idx, n), axis=axis, keepdims=True)
        pick = idx == first
        sel = jnp.where(pick & valid, 1.0, sel)
        gs = jnp.where(pick, REMOVED, gs)
        picks.append(first)
    return sel, picks


def _split3(x):
    hi = x.astype(BF16).astype(F32)
    mid = (x - hi).astype(BF16).astype(F32)
    lo = (x - hi - mid).astype(BF16).astype(F32)
    return [hi, mid, lo]


def _moba_prompt_body(first_ref, qt_ref, g_ref, ka_ref, vta_ref, km_ref, o_ref, qa_s, qd_s):
    h = pl.program_id(1)
    c = pl.program_id(2)
    blk = MOBA_BLOCK
    nb = km_ref.shape[2]
    aug, nbp = _aug_width(nb)
    km = km_ref[0, 0]
    row = lax.broadcasted_iota(jnp.int32, (nb, blk), 0)
    frow = lax.broadcasted_iota(jnp.int32, (ALIBI_ROWS, blk), 0)
    pos_q = (c * blk + lax.broadcasted_iota(jnp.int32, (1, blk), 1)).astype(F32)
    for g in range(Q_PER_KV):
        lanes = slice(g * blk, (g + 1) * blk)
        qg = qt_ref[0, 0, g * HEAD_DIM:(g + 1) * HEAD_DIM, :]
        sel, _ = _top3_mask(_dot(km, qg.astype(F32), HIGHEST), row, row < c, 0)
        bias = jnp.where(sel > 0.0, 0.0, NEG_INF)
        if nbp > nb:
            bias = jnp.concatenate([bias, jnp.zeros((nbp - nb, blk), F32)], axis=0)
        head1 = (h * Q_PER_KV + g + 1).astype(F32)
        slope = jnp.exp2(-8.0 * jnp.full((1, blk), head1, F32) / N_HEADS)
        terms = _split3(slope * float(blk)) + _split3(slope) + _split3(-slope * pos_q)
        feat = jnp.zeros((ALIBI_ROWS, blk), F32)
        for r, term in enumerate(terms):
            feat = jnp.where(frow == r, term, feat)
        for ref, b_rows in ((qa_s, bias.astype(BF16)), (qd_s, jnp.zeros((nbp, blk), BF16))):
            ref[:FEAT_OFF, lanes] = qg
            ref[FEAT_OFF:FEAT_OFF + ALIBI_ROWS, lanes] = feat.astype(BF16)
            ref[FEAT_OFF + ALIBI_ROWS:FEAT_OFF + ALIBI_ROWS + nbp, lanes] = b_rows
            if FEAT_OFF + ALIBI_ROWS + nbp < aug:
                ref[FEAT_OFF + ALIBI_ROWS + nbp:, lanes] = jnp.zeros((aug - FEAT_OFF - ALIBI_ROWS - nbp, blk), BF16)

    def keys(n):
        return ka_ref[0, 0, pl.ds(pl.multiple_of(n * blk, blk), blk), :]

    s = _dot(keys(c), qd_s[...])
    key = lax.broadcasted_iota(jnp.int32, s.shape, 0)
    qry = lax.broadcasted_iota(jnp.int32, s.shape, 1) & (blk - 1)
    s = jnp.where(key <= qry, s, NEG_INF)
    m = jnp.max(s, axis=0, keepdims=True)
    acc = _dot(vta_ref[0, c], jnp.exp(s - m).astype(BF16))

    def block_pair(i, carry):
        m, acc = carry
        n0, n1 = 2 * i, 2 * i + 1
        s0 = _dot(keys(n0), qa_s[...])
        s1 = _dot(keys(n1), qa_s[...])
        m2 = jnp.maximum(m, jnp.maximum(jnp.max(s0, axis=0, keepdims=True), jnp.max(s1, axis=0, keepdims=True)))
        p0 = jnp.exp(s0 - m2).astype(BF16)
        p1 = jnp.exp(s1 - m2).astype(BF16)
        return m2, jnp.exp(m - m2) * acc + _dot(vta_ref[0, n0], p0) + _dot(vta_ref[0, n1], p1)

    first = first_ref[(pl.program_id(0) * KV_HEADS + h) * nb + c]
    m, acc = lax.fori_loop(first, (c + 1) // 2, block_pair, (m, acc))
    out_t = acc[:HEAD_DIM] / acc[HEAD_DIM:HEAD_DIM + 1]
    att = jnp.concatenate([out_t[:, g * blk:(g + 1) * blk] for g in range(Q_PER_KV)], axis=0).T
    gate = g_ref[0]
    o_ref[0] = (att * (gate * _sigmoid(gate))).astype(o_ref.dtype)


_SKIP_LOGIT = -120.0
_NORM_SLACK = 1.02


def _first_needed_pair(stats, bsz, nb):
    q_max = jnp.sqrt(stats[:, 0, :N_HEADS]).reshape(bsz, nb, KV_HEADS, Q_PER_KV) * _NORM_SLACK
    k_max = jnp.sqrt(stats[:, 1, :KV_HEADS]).reshape(bsz, nb, KV_HEADS) * _NORM_SLACK
    slopes = jnp.exp2(-8.0 * jnp.arange(1, N_HEADS + 1, dtype=F32) / N_HEADS).reshape(KV_HEADS, Q_PER_KV)
    c = jnp.arange(nb)[:, None]
    n = jnp.arange(nb)[None, :]
    min_dist = ((c - n) * MOBA_BLOCK - (MOBA_BLOCK - 1)).astype(F32)
    k_pair = k_max[:, None, :, :] + k_max[:, :, None, :]
    bound = (q_max[:, :, None, :, :] * k_pair[..., None]
             - slopes[None, None, None] * min_dist[None, :, :, None, None])
    dead = jnp.all(bound < _SKIP_LOGIT, axis=-1) & (n < c)[None, :, :, None]
    lead = jnp.min(jnp.where(dead, nb, n[None, :, :, None]), axis=2)
    return (lead // 2).transpose(0, 2, 1).reshape(-1).astype(jnp.int32)


def _moba_prompt(qt, gate, ka, vta, km, first_pair):
    bsz, seqlen, d = gate.shape
    nb = seqlen // MOBA_BLOCK
    width = Q_PER_KV * HEAD_DIM
    aug, _ = _aug_width(nb)
    tile = pl.BlockSpec((1, MOBA_BLOCK, width), lambda b, h, c, fp: (b, c, h))
    return pl.pallas_call(
        _moba_prompt_body,
        grid_spec=pltpu.PrefetchScalarGridSpec(
            num_scalar_prefetch=1,
            grid=(bsz, KV_HEADS, nb),
            in_specs=[pl.BlockSpec((1, 1, width, MOBA_BLOCK), lambda b, h, c, fp: (b, c, h, 0)),
                      tile,
                      pl.BlockSpec((1, 1, seqlen, aug), lambda b, h, c, fp: (b, h, 0, 0)),
                      pl.BlockSpec((1, nb, V_ROWS, MOBA_BLOCK), lambda b, h, c, fp: (b, 0, h, 0)),
                      pl.BlockSpec((1, 1, nb, HEAD_DIM), lambda b, h, c, fp: (b, h, 0, 0))],
            out_specs=tile,
            scratch_shapes=[pltpu.VMEM((aug, Q_PER_KV * MOBA_BLOCK), BF16) for _ in range(2)]),
        out_shape=jax.ShapeDtypeStruct((bsz, seqlen, d), BF16),
        compiler_params=_cparams(3),
        name="moba_prompt",
    )(first_pair, qt, gate, ka, vta, km)


_KM_BLOCKS = 16


def _block_mean_body(pt_ref, *refs):
    del pt_ref
    pages, o_ref = refs[:-1], refs[-1]
    s = pl.program_id(1)

    @pl.when(s == 0)
    def _():
        o_ref[...] = jnp.zeros(o_ref.shape, F32)

    lane = lax.broadcasted_iota(jnp.int32, o_ref.shape[1:], 2)
    acc = o_ref[0]
    for j in range(_KM_BLOCKS):
        tot = sum(pages[PAGES_PER_BLOCK * j + r][0] for r in range(PAGES_PER_BLOCK))
        mean = jnp.sum(tot, axis=-1, keepdims=True) * (1.0 / MOBA_BLOCK)
        acc = jnp.where(lane == s * _KM_BLOCKS + j, mean, acc)
    o_ref[0] = acc


def _block_means(cache_t, page_table, n_blocks):
    n_seq = page_table.shape[0]
    assert n_blocks % _KM_BLOCKS == 0
    per_step = _KM_BLOCKS * PAGES_PER_BLOCK
    page_spec = lambda j: pl.BlockSpec(
        (1, KV_HEADS, HEAD_DIM, PAGE_SIZE), lambda b, s, pt, j=j: (pt[b, s * per_step + j], 0, 0, 0))
    return pl.pallas_call(
        _block_mean_body,
        grid_spec=pltpu.PrefetchScalarGridSpec(
            num_scalar_prefetch=1,
            grid=(n_seq, n_blocks // _KM_BLOCKS),
            in_specs=[page_spec(j) for j in range(per_step)],
            out_specs=pl.BlockSpec((1, KV_HEADS, HEAD_DIM, n_blocks), lambda b, s, pt: (b, 0, 0, 0))),
        out_shape=jax.ShapeDtypeStruct((n_seq, KV_HEADS, HEAD_DIM, n_blocks), F32),
        compiler_params=_cparams(2),
        name="block_means",
    )(page_table, *([cache_t] * per_step))


def _sample_topk_body(q_ref, km_ref, pt_ref, sel_ref, *, n_past):
    gs = jnp.concatenate([_dot(q_ref[0, k], km_ref[0, k], HIGHEST) for k in range(KV_HEADS)], axis=0)
    lane = lax.broadcasted_iota(jnp.int32, gs.shape, 1)
    _, picks = _top3_mask(gs, lane, lane < n_past, 1)
    pages = pt_ref[0].astype(F32)
    page_lane = lax.broadcasted_iota(jnp.int32, (N_HEADS, pages.shape[1]), 1)
    out_lane = lax.broadcasted_iota(jnp.int32, sel_ref.shape[1:], 1)
    out = jnp.zeros(sel_ref.shape[1:], jnp.int32)
    for r, first in enumerate(picks):
        out = jnp.where(out_lane == r, first, out)
        for pg in range(PAGES_PER_BLOCK):
            hit = page_lane == first * PAGES_PER_BLOCK + pg
            page = jnp.sum(jnp.where(hit, pages, 0.0), axis=1, keepdims=True).astype(jnp.int32)
            out = jnp.where(out_lane == MOBA_TOPK + PAGES_PER_BLOCK * r + pg, page, out)
    sel_ref[0] = out


def _sample_topk(q, km_t, page_table):
    n_seq, _, _, n_past = km_t.shape
    n_pages = page_table.shape[1]
    return pl.pallas_call(
        functools.partial(_sample_topk_body, n_past=n_past),
        grid=(n_seq,),
        in_specs=[pl.BlockSpec((1,) + q.shape[1:], lambda b: (b, 0, 0, 0)),
                  pl.BlockSpec((1,) + km_t.shape[1:], lambda b: (b, 0, 0, 0)),
                  pl.BlockSpec((1, 1, n_pages), lambda b: (b, 0, 0))],
        out_specs=pl.BlockSpec((1, N_HEADS, LANES), lambda b: (b, 0, 0)),
        out_shape=jax.ShapeDtypeStruct((n_seq, N_HEADS, LANES), jnp.int32),
        compiler_params=_cparams(1),
        name="sample_topk",
    )(q, km_t, page_table.reshape(n_seq, 1, n_pages))


_Q_ROWS = 8


def _moba_sample_body(idx_ref, pages_ref, q_ref, kn_ref, vn_ref, *refs, n_past):
    del pages_ref
    n_sel = MOBA_TOPK * PAGES_PER_BLOCK
    n_pages = Q_PER_KV * n_sel
    k_pages, v_pages, o_ref = refs[:n_pages], refs[n_pages:2 * n_pages], refs[-1]
    b = pl.program_id(0)
    kvh = pl.program_id(1)
    keys = MOBA_TOPK * MOBA_BLOCK
    col = lax.broadcasted_iota(jnp.int32, (_Q_ROWS, keys), 1)
    slot = col // MOBA_BLOCK
    pos = n_past * MOBA_BLOCK
    for g in range(Q_PER_KV):
        h = kvh * Q_PER_KV + g
        q = jnp.broadcast_to(q_ref[b, h], (_Q_ROWS, HEAD_DIM))
        kt_sel = jnp.concatenate([r[0, 0] for r in k_pages[g * n_sel:(g + 1) * n_sel]], axis=1)
        vt_sel = jnp.concatenate([r[0, 0] for r in v_pages[g * n_sel:(g + 1) * n_sel]], axis=1)
        base = (b * N_HEADS + h) * MOBA_TOPK
        idx = jnp.zeros((_Q_ROWS, keys), jnp.int32)
        for r in range(MOBA_TOPK):
            idx = jnp.where(slot == r, idx_ref[base + r], idx)
        dist = (pos - (idx * MOBA_BLOCK + col % MOBA_BLOCK)).astype(F32)
        slope = jnp.exp2(-8.0 * jnp.full((_Q_ROWS, keys), (h + 1).astype(F32), F32) / N_HEADS)
        s = _dot(q.astype(BF16), kt_sel.astype(BF16)) - slope * dist
        s = jnp.where(idx < n_past, s, NEG_INF)
        s_own = jnp.sum(q * kn_ref[b, kvh], axis=-1, keepdims=True)
        m = jnp.maximum(jnp.max(s, axis=-1, keepdims=True), s_own)
        p = jnp.exp(s - m)
        p_own = jnp.exp(s_own - m)
        l = jnp.sum(p, axis=-1, keepdims=True) + p_own
        out = (_dot_nt(p.astype(BF16), vt_sel.astype(BF16)) + p_own * vn_ref[b, kvh]) / l
        o_ref[b, h] = out[0:1]


def _moba_sample(q, k_new, v_new, cache_kt, cache_vt, idx_flat, pages_flat, n_past):
    n_seq = q.shape[0]
    n_pages = Q_PER_KV * MOBA_TOPK * PAGES_PER_BLOCK

    def page_spec(j):
        return pl.BlockSpec((1, 1, HEAD_DIM, PAGE_SIZE),
                            lambda b, kvh, idx, pages: (pages[(b * KV_HEADS + kvh) * n_pages + j], kvh, 0, 0))

    whole = lambda a: pl.BlockSpec(a.shape, lambda b, kvh, idx, pages: (0, 0, 0, 0))
    return pl.pallas_call(
        functools.partial(_moba_sample_body, n_past=n_past),
        grid_spec=pltpu.PrefetchScalarGridSpec(
            num_scalar_prefetch=2,
            grid=(n_seq, KV_HEADS),
            in_specs=[whole(q), whole(k_new), whole(v_new)] + [page_spec(j) for j in range(n_pages)] * 2,
            out_specs=whole(q)),
        out_shape=jax.ShapeDtypeStruct((n_seq, N_HEADS, 1, HEAD_DIM), F32),
        compiler_params=_cparams(2),
        name="moba_sample",
    )(idx_flat, pages_flat, q, k_new, v_new, *([cache_kt] * n_pages), *([cache_vt] * n_pages))


def _out_b_body(a_ref, *refs, prec, gated):
    if gated:
        x1_ref, w_ref, g_ref, y_ref = refs
        a = a_ref[...]
    else:
        gate_ref, x1_ref, w_ref, g_ref, y_ref = refs
        gate = gate_ref[...]
        a = (a_ref[...] * (gate * _sigmoid(gate))).astype(w_ref.dtype)
    o = _dot(a, w_ref[...], prec)
    y_ref[...] = x1_ref[...] + _rms_scale(o) * g_ref[...]


def _out_b(att, gate, x1, w, g_post, tm, prec):
    m, d = x1.shape
    tile = pl.BlockSpec((tm, d), lambda i: (i, 0))
    gated = gate is None
    acts = [att, x1] if gated else [att, gate, x1]
    return pl.pallas_call(
        functools.partial(_out_b_body, prec=prec, gated=gated),
        grid=(m // tm,),
        in_specs=[tile] * len(acts) + [_full((d, d)), _full((1, d))],
        out_specs=tile,
        out_shape=jax.ShapeDtypeStruct((m, d), F32),
        compiler_params=_cparams(1),
        name="out_b",
    )(*acts, w, g_post.reshape(1, d))


TM_IN_A = 1024
TM_POST_A = 512
TM_OUT_B = 1024
TM_PRE_B = 2 * MOBA_BLOCK
S5_CHUNKS_PER_STEP = 64


def _prompt_trunk(x, p):
    bsz, seqlen, d = x.shape
    m = bsz * seqlen
    nb = seqlen // MOBA_BLOCK
    nc = seqlen // S5_CHUNK
    x2 = x.reshape(m, d)
    tm = TM_IN_A
    tile = pl.BlockSpec((tm, d), lambda i: (i, 0))
    u, gate_a = _norm_matmul(x2, p['g_pre_a'], p['w_in_a'].astype(BF16), None, (m // tm,), tile, [tile, tile],
                             [jax.ShapeDtypeStruct((m, d), F32)] * 2)
    ar, ai, bb_re, bb_im = _s5_discretize(p['lambda_re'], p['lambda_im'], p['log_dt'], p['b_re'], p['b_im'])
    ops = _s5_prompt_operators(ar, ai, bb_re, bb_im, p['c_re'], p['c_im'], p['d_skip'])
    y, h_fin = _s5_prompt(u.reshape(bsz, seqlen, d), ops, min(nc, S5_CHUNKS_PER_STEP))
    h_fin = h_fin.reshape(d // LANES, bsz, S5_SUBS, 2, S5_SUB_GROUPS, SSM_STATE).transpose(3, 1, 0, 2, 4, 5)
    h_fin = h_fin.reshape(2, 1, bsz, SSM_GROUPS, SSM_STATE)
    tm = TM_POST_A
    tile = pl.BlockSpec((tm, d), lambda i: (i, 0))
    x1 = _post_a(y.reshape(m, d), gate_a, x2, p['w_glu'].astype(BF16), p['b_glu'], p['w_out_a'].astype(BF16),
                 p['g_post_a'], None, (m // tm,), tile, tile)
    k, v, gate_b, qt, ka, vta, km, stats = _pre_b(x1, p['g_kv'], p['w_kv'].astype(BF16), p['g_pre_b'],
                                                  p['w_in_b'].astype(BF16), TM_PRE_B, None, (bsz, nb))
    km = km.reshape(bsz, nb, KV_HEADS, HEAD_DIM).transpose(0, 2, 1, 3)
    att = _moba_prompt(qt, gate_b.reshape(bsz, seqlen, d), ka, vta, km, _first_needed_pair(stats, bsz, nb))
    y_out = _out_b(att.reshape(m, d), None, x1, p['w_out_b'].astype(BF16), p['g_post_b'], TM_OUT_B, None)
    to_out = lambda t: t.reshape(bsz, KV_HEADS, HEAD_DIM, seqlen).transpose(0, 3, 1, 2)
    return (y_out.reshape(bsz, seqlen, d), h_fin[0], h_fin[1], to_out(k), to_out(v))


def _sample_trunk(x, h0_re, h0_im, cache_k, cache_v, page_table, p):
    n_seq, seqlen, d = x.shape
    assert seqlen == 1
    g, pch = SSM_GROUPS, SSM_GROUP
    n_past = page_table.shape[1] // PAGES_PER_BLOCK
    x2 = x.reshape(n_seq, d)
    whole = pl.BlockSpec((n_seq, d), lambda i: (0, 0))
    u, gate_a = _norm_matmul(x2, p['g_pre_a'], p['w_in_a'], HIGHEST, (1,), whole, [whole, whole],
                             [jax.ShapeDtypeStruct((n_seq, d), F32)] * 2)
    ar, ai, bb_re, bb_im = _s5_discretize(p['lambda_re'], p['lambda_im'], p['log_dt'], p['b_re'], p['b_im'])
    y_g, hr, hi = _s5_step(u.reshape(n_seq, g, pch).transpose(1, 0, 2),
                           h0_re.transpose(1, 0, 2), h0_im.transpose(1, 0, 2),
                           ar, ai, bb_re, bb_im, p['c_re'], p['c_im'], p['d_skip'])
    y = y_g.transpose(1, 0, 2).reshape(n_seq, d)
    x1 = _post_a(y, gate_a, x2, p['w_glu'], p['b_glu'], p['w_out_a'], p['g_post_a'], HIGHEST, (1,), whole, whole)
    k, v, gate_b, q = _pre_b(x1, p['g_kv'], p['w_kv'], p['g_pre_b'], p['w_in_b'], n_seq, HIGHEST)
    cache_kt = cache_k.transpose(0, 2, 3, 1)
    cache_vt = cache_v.transpose(0, 2, 3, 1)
    km_t = _block_means(cache_kt, page_table, n_past)
    sel = _sample_topk(q.reshape(n_seq, KV_HEADS, Q_PER_KV, HEAD_DIM), km_t, page_table)
    n_sel = MOBA_TOPK * PAGES_PER_BLOCK
    att = _moba_sample(q.reshape(n_seq, N_HEADS, 1, HEAD_DIM), k.reshape(n_seq, KV_HEADS, 1, HEAD_DIM),
                       v.reshape(n_seq, KV_HEADS, 1, HEAD_DIM), cache_kt, cache_vt,
                       sel[:, :, :MOBA_TOPK].reshape(-1), sel[:, :, MOBA_TOPK:MOBA_TOPK + n_sel].reshape(-1), n_past)
    y_out = _out_b(att.reshape(n_seq, d), gate_b, x1, p['w_out_b'], p['g_post_b'], n_seq, HIGHEST)
    return (y_out.reshape(n_seq, 1, d), hr.transpose(1, 0, 2)[None], hi.transpose(1, 0, 2)[None],
            k.reshape(n_seq, 1, KV_HEADS, HEAD_DIM), v.reshape(n_seq, 1, KV_HEADS, HEAD_DIM))


def kernel(x_prompt, x_sample, state_ssm_re, state_ssm_im, cache_k, cache_v, page_table, g_pre_a, w_in_a, lambda_re, lambda_im, log_dt, b_re, b_im, c_re, c_im, d_skip, w_glu, b_glu, w_out_a, g_post_a, g_kv, w_kv, g_pre_b, w_in_b, w_out_b, g_post_b):
    p = {'g_pre_a': g_pre_a[0], 'w_in_a': w_in_a[0], 'lambda_re': lambda_re[0], 'lambda_im': lambda_im[0],
         'log_dt': log_dt[0], 'b_re': b_re[0], 'b_im': b_im[0], 'c_re': c_re[0], 'c_im': c_im[0],
         'd_skip': d_skip[0], 'w_glu': w_glu[0], 'b_glu': b_glu[0], 'w_out_a': w_out_a[0],
         'g_post_a': g_post_a[0], 'g_kv': g_kv, 'w_kv': w_kv, 'g_pre_b': g_pre_b[0],
         'w_in_b': w_in_b[0], 'w_out_b': w_out_b[0], 'g_post_b': g_post_b[0]}
    y_p, re_p, im_p, k_p, v_p = _prompt_trunk(x_prompt, p)
    y_s, re_s, im_s, k_s, v_s = _sample_trunk(x_sample, state_ssm_re[0], state_ssm_im[0],
                                              cache_k, cache_v, page_table, p)
    return (y_p, y_s, re_p, im_p, k_p, v_p, re_s, im_s, k_s, v_s)
```
